```python
import jax, jax.numpy as jnp
from jax import lax
import numpy as np

D_MODEL = 2048
BATCH = 2
SEQ = 8192
DEPTH = 2

N_A_LAYERS = DEPTH // 2
N_B_LAYERS = DEPTH - N_A_LAYERS
PLE_DIM = 256
EPS = 1e-6

M_HEADS = 4
M_QK_DIM = D_MODEL // 2 // M_HEADS
M_V_DIM = D_MODEL // M_HEADS
M_CHUNK = 64
GATE_CAP = 15.0
M_IN_COLS = 2 * M_HEADS * M_QK_DIM + 2 * M_HEADS * M_V_DIM + 2 * M_HEADS

A_HEAD_DIM = 64
A_Q_HEADS = D_MODEL // A_HEAD_DIM
A_KV_HEADS = A_Q_HEADS // 8
A_GROUP = A_Q_HEADS // A_KV_HEADS
WINDOW = 128
A_BLOCK = WINDOW

P_HEADS = 8
P_NKEYS = 128
P_EXPERTS = P_NKEYS * P_NKEYS
P_QDIM = 256
P_HALF = P_QDIM // 2
P_TOPK = 16
P_TOKEN_CHUNK = 128

kernel_name = "yoco_mlstm_swa_sink_peer_ple"


def rms_norm(x, w):
    xf = x.astype(jnp.float32)
    y = xf * lax.rsqrt(jnp.mean(xf * xf, axis=-1, keepdims=True) + EPS)
    return (y * w.astype(jnp.float32)).astype(x.dtype)


def soft_cap(t):
    return GATE_CAP * jnp.tanh(t / GATE_CAP)


def mlstm_mixer(xn, w_in, gate_bias, head_norm, w_out):
    B, S, _ = xn.shape
    H, DK, DV, L = M_HEADS, M_QK_DIM, M_V_DIM, M_CHUNK
    NC = S // L
    proj = (xn @ w_in).astype(jnp.float32)
    o0 = H * DK
    o1 = 2 * H * DK
    o2 = o1 + H * DV
    o3 = o2 + H * DV
    o4 = o3 + H
    q = proj[..., :o0]
    k = proj[..., o0:o1]
    v = proj[..., o1:o2]
    og = proj[..., o2:o3]
    gb = gate_bias.astype(jnp.float32)
    log_i = soft_cap(proj[..., o3:o4] + gb[0])
    log_f = jax.nn.log_sigmoid(soft_cap(proj[..., o4:] + gb[1]))

    def to_chunks(t, d):
        return t.reshape(B, NC, L, H, d).transpose(1, 0, 3, 2, 4)

    def gate_chunks(t):
        return t.reshape(B, NC, L, H).transpose(1, 0, 3, 2)

    qc = to_chunks(q, DK) * (DK ** -0.5)
    kc = to_chunks(k, DK)
    vc = to_chunks(v, DV)
    gi = gate_chunks(log_i)
    gf = gate_chunks(log_f)
    causal = jnp.tril(jnp.ones((L, L), dtype=bool))

    def step(carry, inp):
        C, n, m = carry
        qb, kb, vb, li, lf = inp
        b = jnp.cumsum(lf, axis=-1)
        dmat = b[..., :, None] - b[..., None, :] + li[..., None, :]
        dmat = jnp.where(causal, dmat, -jnp.inf)
        inter = b + m[..., None]
        m_t = jnp.maximum(inter, jnp.max(dmat, axis=-1))
        w_intra = jnp.exp(dmat - m_t[..., None])
        w_inter = jnp.exp(inter - m_t)
        s = jnp.einsum('bhld,bhsd->bhls', qb, kb) * w_intra
        num = (w_inter[..., None] * jnp.einsum('bhld,bhde->bhle', qb, C)
               + jnp.einsum('bhls,bhse->bhle', s, vb))
        den = w_inter * jnp.einsum('bhld,bhd->bhl', qb, n) + jnp.sum(s, axis=-1)
        h = num / jnp.maximum(jnp.abs(den), jnp.exp(-m_t))[..., None]
        m_new = m_t[..., -1]
        w_state = jnp.exp(b[..., -1:] - b + li - m_new[..., None])
        decay = jnp.exp(b[..., -1] + m - m_new)
        C_new = decay[..., None, None] * C + jnp.einsum('bhs,bhsd,bhse->bhde', w_state, kb, vb)
        n_new = decay[..., None] * n + jnp.einsum('bhs,bhsd->bhd', w_state, kb)
        return (C_new, n_new, m_new), h

    init = (jnp.zeros((B, H, DK, DV), jnp.float32),
            jnp.zeros((B, H, DK), jnp.float32),
            jnp.zeros((B, H), jnp.float32))
    _, hs = lax.scan(step, init, (qc, kc, vc, gi, gf))
    h = hs.transpose(1, 0, 3, 2, 4).reshape(B, S, H, DV)
    h = h * lax.rsqrt(jnp.mean(h * h, axis=-1, keepdims=True) + EPS)
    h = h * head_norm.astype(jnp.float32).reshape(H, DV)
    h = h.reshape(B, S, H * DV) * jax.nn.sigmoid(og)
    return h.astype(xn.dtype) @ w_out


def swa_sink_mixer(xn, k, v, w_q, sinks, w_out):
    B, S, _ = xn.shape
    NBLK = S // A_BLOCK
    q = (xn @ w_q).reshape(B, NBLK, A_BLOCK, A_KV_HEADS, A_GROUP, A_HEAD_DIM)
    pad = ((0, 0), (A_BLOCK, 0), (0, 0), (0, 0))
    kb = jnp.pad(k, pad).reshape(B, NBLK + 1, A_BLOCK, A_KV_HEADS, A_HEAD_DIM)
    vb = jnp.pad(v, pad).reshape(B, NBLK + 1, A_BLOCK, A_KV_HEADS, A_HEAD_DIM)
    kw = jnp.concatenate([kb[:, :-1], kb[:, 1:]], axis=2)
    vw = jnp.concatenate([vb[:, :-1], vb[:, 1:]], axis=2)
    s = jnp.einsum('bnqkgd,bnskd->bnkgqs', q, kw).astype(jnp.float32) * (A_HEAD_DIM ** -0.5)
    qpos = jnp.arange(A_BLOCK)[:, None] + A_BLOCK
    kpos = jnp.arange(2 * A_BLOCK)[None, :]
    dist = qpos - kpos
    band = (dist >= 0) & (dist < WINDOW)
    valid = (jnp.arange(NBLK)[:, None, None] * A_BLOCK + kpos[None] - A_BLOCK) >= 0
    mask = band[None] & valid
    s = jnp.where(mask[None, :, None, None], s, -jnp.inf)
    sink = sinks.astype(jnp.float32).reshape(A_KV_HEADS, A_GROUP)[None, None, :, :, None, None]
    mx = jnp.maximum(jnp.max(s, axis=-1, keepdims=True), sink)
    pr = jnp.exp(s - mx)
    pr = pr / (jnp.sum(pr, axis=-1, keepdims=True) + jnp.exp(sink - mx))
    o = jnp.einsum('bnkgqs,bnskd->bnqkgd', pr.astype(vw.dtype), vw)
    return o.reshape(B, S, A_Q_HEADS * A_HEAD_DIM) @ w_out


def peer_mixer(xn, w_q, k1, k2, u, v):
    B, S, D = xn.shape
    xt = xn.reshape((B * S) // P_TOKEN_CHUNK, P_TOKEN_CHUNK, D)

    def chunk(xc):
        q = (xc @ w_q).reshape(-1, P_HEADS, 2, P_HALF)
        s1 = jnp.einsum('chd,hnd->chn', q[:, :, 0], k1).astype(jnp.float32)
        s2 = jnp.einsum('chd,hnd->chn', q[:, :, 1], k2).astype(jnp.float32)
        v1, i1 = lax.top_k(s1, P_TOPK)
        v2, i2 = lax.top_k(s2, P_TOPK)
        cand = (v1[..., :, None] + v2[..., None, :]).reshape(-1, P_HEADS, P_TOPK * P_TOPK)
        cidx = (i1[..., :, None] * P_NKEYS + i2[..., None, :]).reshape(-1, P_HEADS, P_TOPK * P_TOPK)
        top_s, pos = lax.top_k(cand, P_TOPK)
        eidx = jnp.take_along_axis(cidx, pos, axis=-1)
        g = jax.nn.softmax(top_s, axis=-1)
        ue = u[eidx]
        act = jax.nn.gelu(jnp.einsum('cd,chkd->chk', xc, ue).astype(jnp.float32), approximate=False)
        coef = (g * act).astype(xc.dtype)
        ve = v[eidx]
        return jnp.einsum('chk,chkd->cd', coef, ve)

    y = lax.map(chunk, xt)
    return y.reshape(B, S, D)


def setup_inputs(seed: int = 0) -> dict:
    key = jax.random.key(seed)
    ks = jax.random.split(key, 32)
    f32 = jnp.float32
    nrm = lambda k, shape, scale: jax.random.normal(k, shape, f32) * scale
    gain = lambda k, shape: 1.0 + 0.02 * jax.random.normal(k, shape, f32)
    gate_noise = nrm(ks[4], (N_A_LAYERS, 2, M_HEADS), 0.3)
    gate_bias = gate_noise + jnp.array([-1.0, 3.0], f32)[None, :, None]
    return {
        "x": nrm(ks[0], (BATCH, SEQ, D_MODEL), 1.0),
        "p": nrm(ks[1], (DEPTH, BATCH, SEQ, PLE_DIM), 1.0),
        "a_norm": gain(ks[2], (N_A_LAYERS, D_MODEL)),
        "a_w_in": nrm(ks[3], (N_A_LAYERS, D_MODEL, M_IN_COLS), D_MODEL ** -0.5),
        "a_gate_bias": gate_bias,
        "a_head_norm": gain(ks[5], (N_A_LAYERS, M_HEADS * M_V_DIM)),
        "a_w_out": nrm(ks[6], (N_A_LAYERS, M_HEADS * M_V_DIM, D_MODEL), (M_HEADS * M_V_DIM) ** -0.5),
        "kv_norm": gain(ks[7], (D_MODEL,)),
        "w_kv": nrm(ks[8], (D_MODEL, 2 * A_KV_HEADS * A_HEAD_DIM), D_MODEL ** -0.5),
        "b_norm": gain(ks[9], (N_B_LAYERS, D_MODEL)),
        "b_w_q": nrm(ks[10], (N_B_LAYERS, D_MODEL, A_Q_HEADS * A_HEAD_DIM), D_MODEL ** -0.5),
        "b_sinks": nrm(ks[11], (N_B_LAYERS, A_Q_HEADS), 0.5),
        "b_w_out": nrm(ks[12], (N_B_LAYERS, A_Q_HEADS * A_HEAD_DIM, D_MODEL), (A_Q_HEADS * A_HEAD_DIM) ** -0.5),
        "c_norm": gain(ks[13], (DEPTH, D_MODEL)),
        "peer_w_q": nrm(ks[14], (DEPTH, D_MODEL, P_HEADS * P_QDIM), D_MODEL ** -0.5),
        "peer_k1": nrm(ks[15], (DEPTH, P_HEADS, P_NKEYS, P_HALF), P_HALF ** -0.5),
        "peer_k2": nrm(ks[16], (DEPTH, P_HEADS, P_NKEYS, P_HALF), P_HALF ** -0.5),
        "peer_u": nrm(ks[17], (DEPTH, P_EXPERTS, D_MODEL), D_MODEL ** -0.5),
        "peer_v": nrm(ks[18], (DEPTH, P_EXPERTS, D_MODEL), 0.5 * P_HEADS ** -0.5),
        "ple_norm": gain(ks[19], (DEPTH, D_MODEL)),
        "ple_w_gate": nrm(ks[20], (DEPTH, D_MODEL, D_MODEL), D_MODEL ** -0.5),
        "ple_w_proj": nrm(ks[21], (DEPTH, PLE_DIM, D_MODEL), PLE_DIM ** -0.5),
        "final_norm": gain(ks[22], (D_MODEL,)),
    }


def reference(x, p, a_norm, a_w_in, a_gate_bias, a_head_norm, a_w_out, kv_norm, w_kv,
              b_norm, b_w_q, b_sinks, b_w_out, c_norm, peer_w_q, peer_k1, peer_k2,
              peer_u, peer_v, ple_norm, ple_w_gate, ple_w_proj, final_norm):
    B, S, _ = x.shape
    h = x
    k_sh = None
    v_sh = None
    for i in range(DEPTH):
        if i < N_A_LAYERS:
            h = h + mlstm_mixer(rms_norm(h, a_norm[i]), a_w_in[i], a_gate_bias[i],
                                a_head_norm[i], a_w_out[i])
        else:
            if i == N_A_LAYERS:
                kv = rms_norm(h, kv_norm) @ w_kv
                kv = kv.reshape(B, S, 2, A_KV_HEADS, A_HEAD_DIM)
                k_sh = kv[:, :, 0]
                v_sh = kv[:, :, 1]
            j = i - N_A_LAYERS
            h = h + swa_sink_mixer(rms_norm(h, b_norm[j]), k_sh, v_sh, b_w_q[j], b_sinks[j], b_w_out[j])
        h = h + peer_mixer(rms_norm(h, c_norm[i]), peer_w_q[i], peer_k1[i], peer_k2[i],
                           peer_u[i], peer_v[i])
        gate = jax.nn.sigmoid((rms_norm(h, ple_norm[i]) @ ple_w_gate[i]).astype(jnp.float32))
        h = h + (gate * (p[i] @ ple_w_proj[i]).astype(jnp.float32)).astype(h.dtype)
    return rms_norm(h, final_norm)
```

```python
import functools

import jax
import jax.numpy as jnp
from jax import lax
from jax.experimental import pallas as pl
from jax.experimental.pallas import tpu as pltpu

F32 = jnp.float32
BF16 = jnp.bfloat16

EPS = 1e-6
GATE_CAP = 15.0

M_HEADS = 4
A_HEAD_DIM = 64
A_GROUP = 8
WINDOW = 128
P_HEADS = 8
P_NKEYS = 128
P_TOPK = 16

V7X_LANES = 128
V7X_SUBLANES = 8
V7X_VMEM_BYTES = 64 * 1024 * 1024
VMEM_LIMIT = (V7X_VMEM_BYTES * 3) // 4

NT_DIMS = (((1,), (1,)), ((), ()))
TN_DIMS = (((0,), (0,)), ((), ()))


def _params(*sem):
    return pltpu.CompilerParams(dimension_semantics=sem, vmem_limit_bytes=VMEM_LIMIT)


def _rms(x, gain):
    ms = jnp.mean(x * x, axis=-1, keepdims=True)
    return x * lax.rsqrt(ms + EPS) * gain


def _tile(n, pref):
    t = min(n, pref)
    assert n % t == 0, (n, pref)
    return t


def _rms_matmul_body(x_ref, g_ref, w_ref, o_ref, xn_ref):
    @pl.when(pl.program_id(1) == 0)
    def _():
        xn_ref[...] = _rms(x_ref[...], g_ref[...]).astype(BF16)

    o_ref[...] = jnp.dot(xn_ref[...], w_ref[...], preferred_element_type=F32).astype(o_ref.dtype)


def rms_matmul(x, gain, w, out_dtype, tm=512, tn=512):
    T, D = x.shape
    N = w.shape[1]
    tm, tn = _tile(T, tm), _tile(N, tn)
    return pl.pallas_call(
        _rms_matmul_body,
        grid=(T // tm, N // tn),
        in_specs=[pl.BlockSpec((tm, D), lambda i, j: (i, 0)),
                  pl.BlockSpec((1, D), lambda i, j: (0, 0)),
                  pl.BlockSpec((D, tn), lambda i, j: (0, j))],
        out_specs=pl.BlockSpec((tm, tn), lambda i, j: (i, j)),
        out_shape=jax.ShapeDtypeStruct((T, N), out_dtype),
        scratch_shapes=[pltpu.VMEM((tm, D), BF16)],
        compiler_params=_params("parallel", "arbitrary"),
        name="rms_matmul",
    )(x, gain.reshape(1, D), w)


def _matmul_residual_body(y_ref, w_ref, h_ref, o_ref):
    o_ref[...] = h_ref[...] + jnp.dot(y_ref[...], w_ref[...], preferred_element_type=F32)


def matmul_residual(y, w, h, tm=512, tn=512):
    T, K = y.shape
    N = w.shape[1]
    tm, tn = _tile(T, tm), _tile(N, tn)
    return pl.pallas_call(
        _matmul_residual_body,
        grid=(T // tm, N // tn),
        in_specs=[pl.BlockSpec((tm, K), lambda i, j: (i, 0)),
                  pl.BlockSpec((K, tn), lambda i, j: (0, j)),
                  pl.BlockSpec((tm, tn), lambda i, j: (i, j))],
        out_specs=pl.BlockSpec((tm, tn), lambda i, j: (i, j)),
        out_shape=jax.ShapeDtypeStruct((T, N), F32),
        compiler_params=_params("parallel", "arbitrary"),
        name="matmul_residual",
    )(y, w, h)


def _ple_body(h_ref, g_ref, wg_ref, p_ref, wp_ref, fn_ref, o_ref, *, final):
    h = h_ref[...]
    xn = _rms(h, g_ref[...]).astype(BF16)
    gate = jax.nn.sigmoid(jnp.dot(xn, wg_ref[...], preferred_element_type=F32))
    emb = jnp.dot(p_ref[...].astype(BF16), wp_ref[...], preferred_element_type=F32)
    out = h + gate * emb
    if final:
        out = _rms(out, fn_ref[...])
    o_ref[...] = out


def ple_gate(h, gain, wg, p, wp, final_gain, final, tm=256):
    T, D = h.shape
    PD = p.shape[1]
    tm = _tile(T, tm)
    return pl.pallas_call(
        functools.partial(_ple_body, final=final),
        grid=(T // tm,),
        in_specs=[pl.BlockSpec((tm, D), lambda i: (i, 0)),
                  pl.BlockSpec((1, D), lambda i: (0, 0)),
                  pl.BlockSpec((D, D), lambda i: (0, 0)),
                  pl.BlockSpec((tm, PD), lambda i: (i, 0)),
                  pl.BlockSpec((PD, D), lambda i: (0, 0)),
                  pl.BlockSpec((1, D), lambda i: (0, 0))],
        out_specs=pl.BlockSpec((tm, D), lambda i: (i, 0)),
        out_shape=jax.ShapeDtypeStruct((T, D), F32),
        compiler_params=_params("parallel"),
        name="ple_gate",
    )(h, gain.reshape(1, D), wg, p, wp, final_gain.reshape(1, D))


def _soft_cap(t):
    return GATE_CAP * jnp.tanh(t / GATE_CAP)


def _log_sigmoid(t):
    return jnp.minimum(t, 0.0) - jnp.log(1.0 + jnp.exp(-jnp.abs(t)))


def _split3(x):
    hi = x.astype(BF16)
    r = x - hi.astype(F32)
    mid = r.astype(BF16)
    lo = (r - mid.astype(F32)).astype(BF16)
    return hi, mid, lo


def _mlstm_body(q_ref, k_ref, v_ref, og_ref, gc_ref, gr_ref, bc_ref, br_ref, hn_ref, o_ref,
                c_ref, n_ref, m_ref, *, L, DK):
    @pl.when(pl.program_id(1) == 0)
    def _():
        c_ref[...] = jnp.zeros_like(c_ref)
        n_ref[...] = jnp.zeros_like(n_ref)
        m_ref[...] = jnp.zeros_like(m_ref)

    q = (q_ref[...].astype(F32) * (DK ** -0.5)).astype(BF16)
    k = k_ref[...]
    v = v_ref[...]

    gc = gc_ref[...] + bc_ref[...]
    gr = gr_ref[...] + br_ref[...]
    li_c = _soft_cap(gc[:, 0:1])
    lf_c = _log_sigmoid(_soft_cap(gc[:, 1:2]))
    li_r = _soft_cap(gr[0:1, :])
    lf_r = _log_sigmoid(_soft_cap(gr[1:2, :]))

    row = lax.broadcasted_iota(jnp.int32, (L, L), 0)
    col = lax.broadcasted_iota(jnp.int32, (L, L), 1)
    causal = row >= col
    tri_l = causal.astype(BF16)
    tri_u = (row <= col).astype(BF16)
    b_c = sum(jnp.dot(tri_l, part, preferred_element_type=F32)
              for part in _split3(jnp.broadcast_to(lf_c, (L, V7X_LANES))))[:, 0:1]
    b_r = sum(jnp.dot(part, tri_u, preferred_element_type=F32)
              for part in _split3(jnp.broadcast_to(lf_r, (V7X_SUBLANES, L))))[0:1, :]

    m_prev = m_ref[0:1, 0:1]
    dmat = jnp.where(causal, b_c - b_r + li_r, -jnp.inf)
    inter = b_c + m_prev
    m_t = jnp.maximum(inter, jnp.max(dmat, axis=-1, keepdims=True))
    w_intra = jnp.exp(dmat - m_t)
    w_inter = jnp.exp(inter - m_t)

    s = lax.dot_general(q, k, NT_DIMS, preferred_element_type=F32) * w_intra
    c_old = c_ref[...]
    n_old = n_ref[...]
    num = (w_inter * jnp.dot(q, c_old.astype(BF16), preferred_element_type=F32)
           + jnp.dot(s.astype(BF16), v, preferred_element_type=F32))
    qn = lax.dot_general(q, jnp.broadcast_to(n_old, (V7X_LANES, DK)).astype(BF16), NT_DIMS,
                         preferred_element_type=F32)[:, 0:1]
    den = w_inter * qn + jnp.sum(s, axis=-1, keepdims=True)
    hh = num / jnp.maximum(jnp.abs(den), jnp.exp(-m_t))

    m_new = m_t[L - 1:L, :]
    b_last = b_c[L - 1:L, :]
    w_state = jnp.exp(b_last - b_c + li_c - m_new)
    decay = jnp.exp(b_last + m_prev - m_new)
    kw = k.astype(F32) * w_state
    c_ref[...] = decay * c_old + lax.dot_general(kw.astype(BF16), v, TN_DIMS, preferred_element_type=F32)
    n_ref[...] = decay * n_old + jnp.sum(kw, axis=0, keepdims=True)
    m_ref[...] = jnp.broadcast_to(m_new, m_ref.shape)

    ms = jnp.mean(hh * hh, axis=-1, keepdims=True)
    y = hh * lax.rsqrt(ms + EPS) * hn_ref[...] * jax.nn.sigmoid(og_ref[...].astype(F32))
    o_ref[...] = y.astype(o_ref.dtype)


def mlstm_scan(proj, gates, gate_bias, head_norm, B, S, chunk=256):
    T = proj.shape[0]
    H = M_HEADS
    HDV = head_norm.shape[0]
    DV = HDV // H
    DK = (proj.shape[1] - 2 * HDV) // (2 * H)
    L = _tile(S, chunk)
    NC = S // L
    assert DV % DK == 0 or DK % DV == 0
    k_off = (H * DK) // DK
    v_off = (2 * H * DK) // DV
    og_off = (2 * H * DK + H * DV) // DV
    g2 = gates.reshape(T, 2, H)
    g_col = g2.transpose(2, 0, 1)
    g_row = g2.transpose(2, 1, 0)
    b_col = gate_bias.T.reshape(H, 1, 2)
    b_row = gate_bias.T.reshape(H, 2, 1)

    def tok(bh, c):
        return (bh // H) * NC + c

    return pl.pallas_call(
        functools.partial(_mlstm_body, L=L, DK=DK),
        grid=(B * H, NC),
        in_specs=[pl.BlockSpec((L, DK), lambda bh, c: (tok(bh, c), bh % H)),
                  pl.BlockSpec((L, DK), lambda bh, c: (tok(bh, c), k_off + bh % H)),
                  pl.BlockSpec((L, DV), lambda bh, c: (tok(bh, c), v_off + bh % H)),
                  pl.BlockSpec((L, DV), lambda bh, c: (tok(bh, c), og_off + bh % H)),
                  pl.BlockSpec((None, L, 2), lambda bh, c: (bh % H, tok(bh, c), 0)),
                  pl.BlockSpec((None, 2, L), lambda bh, c: (bh % H, 0, tok(bh, c))),
                  pl.BlockSpec((None, 1, 2), lambda bh, c: (bh % H, 0, 0)),
                  pl.BlockSpec((None, 2, 1), lambda bh, c: (bh % H, 0, 0)),
                  pl.BlockSpec((1, DV), lambda bh, c: (0, bh % H))],
        out_specs=pl.BlockSpec((L, DV), lambda bh, c: (tok(bh, c), bh % H)),
        out_shape=jax.ShapeDtypeStruct((T, HDV), BF16),
        scratch_shapes=[pltpu.VMEM((DK, DV), F32),
                        pltpu.VMEM((1, DK), F32),
                        pltpu.VMEM((V7X_SUBLANES, V7X_LANES), F32)],
        compiler_params=_params("parallel", "arbitrary"),
        name="mlstm_scan",
    )(proj, proj, proj, proj, g_col, g_row, b_col, b_row, head_norm.reshape(1, HDV))


def _swa_body(sink_ref, q_ref, kp_ref, kc_ref, o_ref, *, BLK, KVH, GROUP, HD):
    first = pl.program_id(1) == 0
    kv = jnp.concatenate([kp_ref[...], kc_ref[...]], axis=0).astype(F32)
    ri = lax.broadcasted_iota(jnp.int32, (BLK, 2 * BLK), 0)
    ci = lax.broadcasted_iota(jnp.int32, (BLK, 2 * BLK), 1)
    dist = ri + BLK - ci
    mask = (dist >= 0) & (dist < WINDOW) & ((ci >= BLK) | jnp.logical_not(first))
    lo = lax.broadcasted_iota(jnp.int32, (1, 2 * HD), 1) < HD
    scale = HD ** -0.5
    for pair in range(KVH // 2):
        xk = kv[:, pair * 2 * HD:(pair + 1) * 2 * HD]
        xv = kv[:, (KVH + pair * 2) * HD:(KVH + pair * 2 + 2) * HD]
        xk_r = pltpu.roll(xk, HD, 1)
        xv_r = pltpu.roll(xv, HD, 1)
        for sub in range(2):
            kh = 2 * pair + sub
            if sub == 0:
                k2 = jnp.where(lo, xk, xk_r)
                v_lo = jnp.where(lo, xv, 0.0)
                v_hi = jnp.where(lo, 0.0, xv_r)
            else:
                k2 = jnp.where(lo, xk_r, xk)
                v_lo = jnp.where(lo, xv_r, 0.0)
                v_hi = jnp.where(lo, 0.0, xv)
            k2 = k2.astype(BF16)
            v_halves = (v_lo.astype(BF16), v_hi.astype(BF16))
            for gp in range(GROUP // 2):
                h0 = kh * GROUP + 2 * gp
                q2 = q_ref[:, h0 * HD:(h0 + 2) * HD]
                acc = None
                for e in range(2):
                    qm = jnp.where(lo if e == 0 else jnp.logical_not(lo), q2, jnp.zeros_like(q2))
                    s = lax.dot_general(qm, k2, NT_DIMS, preferred_element_type=F32) * scale
                    s = jnp.where(mask, s, -jnp.inf)
                    sink = sink_ref[h0 + e]
                    mx = jnp.maximum(jnp.max(s, axis=-1, keepdims=True), sink)
                    pr = jnp.exp(s - mx)
                    den = jnp.sum(pr, axis=-1, keepdims=True) + jnp.exp(sink - mx)
                    o = jnp.dot(pr.astype(BF16), v_halves[e], preferred_element_type=F32) / den
                    acc = o if acc is None else acc + o
                o_ref[:, h0 * HD:(h0 + 2) * HD] = acc.astype(o_ref.dtype)


def swa_attention(q, kv, sinks, B, S):
    T, QD = q.shape
    HD, GROUP, BLK = A_HEAD_DIM, A_GROUP, WINDOW
    KVH = kv.shape[1] // (2 * HD)
    assert QD == KVH * GROUP * HD and 2 * HD == V7X_LANES and KVH % 2 == 0 and GROUP % 2 == 0
    NB = S // BLK
    return pl.pallas_call(
        functools.partial(_swa_body, BLK=BLK, KVH=KVH, GROUP=GROUP, HD=HD),
        grid=(B, NB),
        in_specs=[pl.BlockSpec(memory_space=pltpu.SMEM),
                  pl.BlockSpec((BLK, QD), lambda b, n: (b * NB + n, 0)),
                  pl.BlockSpec((BLK, 2 * KVH * HD), lambda b, n: (b * NB + jnp.maximum(n - 1, 0), 0)),
                  pl.BlockSpec((BLK, 2 * KVH * HD), lambda b, n: (b * NB + n, 0))],
        out_specs=pl.BlockSpec((BLK, QD), lambda b, n: (b * NB + n, 0)),
        out_shape=jax.ShapeDtypeStruct((T, QD), BF16),
        compiler_params=_params("parallel", "arbitrary"),
        name="swa_attention",
    )(sinks, q, kv, kv)


KEY_BIG = 1 << 30


def _topk_rows(vals, keys, k):
    tm = vals.shape[1]
    slot = lax.broadcasted_iota(jnp.int32, (k, tm), 0)
    rows_v, rows_k = [], []
    arr_v = jnp.zeros((k, tm), vals.dtype)
    arr_k = jnp.zeros((k, tm), keys.dtype)
    for it in range(k):
        m = jnp.max(vals, axis=0, keepdims=True)
        kmin = jnp.min(jnp.where(vals == m, keys, KEY_BIG), axis=0, keepdims=True)
        vals = jnp.where(keys == kmin, -jnp.inf, vals)
        rows_v.append(m)
        rows_k.append(kmin)
        arr_v = jnp.where(slot == it, m, arr_v)
        arr_k = jnp.where(slot == it, kmin, arr_k)
    return rows_v, rows_k, arr_v, arr_k


def _candidates(top1, top2, k):
    r1_v, r1_i, v1, i1 = top1
    r2_v, r2_i, v2, i2 = top2
    tm = v1.shape[1]
    sub = V7X_SUBLANES
    vals, keys = [], []

    def pack(code, e1, e2):
        return (code << 14) | (e1 << 7) | e2

    b_iota = lax.broadcasted_iota(jnp.int32, (k, tm), 0)
    vals.append(r1_v[0] + v2)
    keys.append(pack(b_iota, r1_i[0], i2))
    a = 1
    while a < k and k // (a + 1) >= 2:
        nb = k // (a + 1)
        rows = -(-nb // sub) * sub
        bi = lax.broadcasted_iota(jnp.int32, (rows, tm), 0)
        vals.append(jnp.where(bi < nb, r1_v[a] + v2[0:rows], -jnp.inf))
        keys.append(pack(a * k + bi, r1_i[a], i2[0:rows]))
        a += 1
    if a < k:
        a0 = (a // sub) * sub
        ai = lax.broadcasted_iota(jnp.int32, (k - a0, tm), 0) + a0
        vals.append(jnp.where(ai >= a, v1[a0:k] + r2_v[0], -jnp.inf))
        keys.append(pack(ai * k, i1[a0:k], r2_i[0]))
    return jnp.concatenate(vals, axis=0), jnp.concatenate(keys, axis=0)


def _peer_select_body(h_ref, g_ref, wq_ref, k1_ref, k2_ref, xn_ref, gate_ref, e1_ref, e2_ref, qs_ref,
                      *, NK, HALF, K):
    hd = pl.program_id(1)

    @pl.when(hd == 0)
    def _():
        xn = _rms(h_ref[...], g_ref[...]).astype(BF16)
        xn_ref[...] = xn
        q = jnp.dot(xn, wq_ref[...], preferred_element_type=F32).astype(BF16)
        for c in range(qs_ref.shape[0]):
            qs_ref[c] = q[:, c * HALF:(c + 1) * HALF]

    q1 = qs_ref[2 * hd]
    q2 = qs_ref[2 * hd + 1]
    tm = q1.shape[0]
    s1 = lax.dot_general(k1_ref[...], q1, NT_DIMS, preferred_element_type=F32)
    s2 = lax.dot_general(k2_ref[...], q2, NT_DIMS, preferred_element_type=F32)
    key_iota = lax.broadcasted_iota(jnp.int32, (NK, tm), 0)
    cand, keys = _candidates(_topk_rows(s1, key_iota, K), _topk_rows(s2, key_iota, K), K)
    rows_s, _, top_s, top_key = _topk_rows(cand, keys, K)
    ex = jnp.exp(top_s - rows_s[0])
    gate_ref[...] = ex / jnp.sum(ex, axis=0, keepdims=True)
    e1_ref[...] = ((top_key >> 7) & 127).astype(F32)
    e2_ref[...] = (top_key & 127).astype(F32)


def peer_select(h, gain, wq, k1, k2, tm=256):
    T, D = h.shape
    PH, NK, HALF = k1.shape
    K = P_TOPK
    assert NK == V7X_LANES and HALF % V7X_LANES == 0 and wq.shape[1] == PH * 2 * HALF
    tm = _tile(T, tm)
    sel = jax.ShapeDtypeStruct((PH * K, T), F32)
    sel_spec = pl.BlockSpec((K, tm), lambda i, hd: (hd, i))
    return pl.pallas_call(
        functools.partial(_peer_select_body, NK=NK, HALF=HALF, K=K),
        grid=(T // tm, PH),
        in_specs=[pl.BlockSpec((tm, D), lambda i, hd: (i, 0)),
                  pl.BlockSpec((1, D), lambda i, hd: (0, 0)),
                  pl.BlockSpec((D, PH * 2 * HALF), lambda i, hd: (0, 0)),
                  pl.BlockSpec((None, NK, HALF), lambda i, hd: (hd, 0, 0)),
                  pl.BlockSpec((None, NK, HALF), lambda i, hd: (hd, 0, 0))],
        out_specs=[pl.BlockSpec((tm, D), lambda i, hd: (i, 0)), sel_spec, sel_spec, sel_spec],
        out_shape=[jax.ShapeDtypeStruct((T, D), BF16), sel, sel, sel],
        scratch_shapes=[pltpu.VMEM((2 * PH, tm, HALF), BF16)],
        compiler_params=_params("parallel", "arbitrary"),
        name="peer_select",
    )(h, gain.reshape(1, D), wq, k1, k2)


def _peer_scatter_body(gate_ref, e1_ref, e2_ref, w_ref, scr_ref, gt_ref, e1t_ref, e2t_ref, *, NK):
    tmb = gt_ref.shape[0]
    gt_ref[...] = gate_ref[...].T
    e1t_ref[...] = e1_ref[...].T
    e2t_ref[...] = e2_ref[...].T
    key_iota = lax.broadcasted_iota(jnp.int32, (NK, gt_ref.shape[1]), 0).astype(F32)

    def one_token(t, carry):
        g_row = gt_ref[pl.ds(t, 1), :]
        sel1 = jnp.where(e1t_ref[pl.ds(t, 1), :] == key_iota, g_row, 0.0).astype(BF16)
        sel2 = jnp.where(e2t_ref[pl.ds(t, 1), :] == key_iota, 1.0, 0.0).astype(BF16)
        w_t = lax.dot_general(sel1, sel2, NT_DIMS, preferred_element_type=F32)
        scr_ref[pl.ds(pl.multiple_of(t * NK, NK), NK), :] = w_t
        return carry

    lax.fori_loop(0, tmb, one_token, 0)
    for j in range(NK):
        w_ref[j] = scr_ref[pl.ds(j, tmb, stride=NK), :].astype(w_ref.dtype)


def peer_scatter(gate, e1, e2, NK, tmb=128):
    HK, T = gate.shape
    tmb = _tile(T, tmb)
    spec = pl.BlockSpec((HK, tmb), lambda i: (0, i))
    return pl.pallas_call(
        functools.partial(_peer_scatter_body, NK=NK),
        grid=(T // tmb,),
        in_specs=[spec, spec, spec],
        out_specs=pl.BlockSpec((NK, tmb, NK), lambda i: (0, i, 0)),
        out_shape=jax.ShapeDtypeStruct((NK, T, NK), BF16),
        scratch_shapes=[pltpu.VMEM((tmb * NK, NK), F32),
                        pltpu.VMEM((tmb, HK), F32),
                        pltpu.VMEM((tmb, HK), F32),
                        pltpu.VMEM((tmb, HK), F32)],
        compiler_params=_params("parallel"),
        name="peer_scatter",
    )(gate, e1, e2)


def _gelu(x):
    return 0.5 * x * (1.0 + lax.erf(x * (2.0 ** -0.5)))


def _peer_dense_body(xn_ref, u_ref, v_ref, w_ref, h_ref, o_ref, acc_ref, *, NK):
    j = pl.program_id(1)

    @pl.when(j == 0)
    def _():
        acc_ref[...] = jnp.zeros_like(acc_ref)

    act = lax.dot_general(xn_ref[...], u_ref[...], NT_DIMS, preferred_element_type=F32)
    coef = jnp.concatenate(
        [(w_ref[c].astype(F32) * _gelu(act[:, c * NK:(c + 1) * NK])).astype(BF16)
         for c in range(w_ref.shape[0])], axis=1)
    acc_ref[...] += jnp.dot(coef, v_ref[...], preferred_element_type=F32)

    @pl.when(j == pl.num_programs(1) - 1)
    def _():
        o_ref[...] = h_ref[...] + acc_ref[...]


def peer_dense(xn, u, v, w, h, tm=512, te=512):
    T, D = xn.shape
    E = u.shape[0]
    NK = w.shape[0]
    tm, te = _tile(T, tm), _tile(E, te)
    assert te % NK == 0 and E == NK * NK
    return pl.pallas_call(
        functools.partial(_peer_dense_body, NK=NK),
        grid=(T // tm, E // te),
        in_specs=[pl.BlockSpec((tm, D), lambda i, j: (i, 0)),
                  pl.BlockSpec((te, D), lambda i, j: (j, 0)),
                  pl.BlockSpec((te, D), lambda i, j: (j, 0)),
                  pl.BlockSpec((te // NK, tm, NK), lambda i, j: (j, i, 0)),
                  pl.BlockSpec((tm, D), lambda i, j: (i, 0))],
        out_specs=pl.BlockSpec((tm, D), lambda i, j: (i, 0)),
        out_shape=jax.ShapeDtypeStruct((T, D), F32),
        scratch_shapes=[pltpu.VMEM((tm, D), F32)],
        compiler_params=_params("parallel", "arbitrary"),
        name="peer_dense",
    )(xn, u, v, w, h)


def peer_layer(h, gain, wq, k1, k2, u, v):
    xn, gate, e1, e2 = peer_select(h, gain, wq, k1, k2)
    w = peer_scatter(gate, e1, e2, k1.shape[1])
    return peer_dense(xn, u, v, w, h)


def kernel(x, p, a_norm, a_w_in, a_gate_bias, a_head_norm, a_w_out, kv_norm, w_kv, b_norm, b_w_q, b_sinks,
           b_w_out, c_norm, peer_w_q, peer_k1, peer_k2, peer_u, peer_v, ple_norm, ple_w_gate, ple_w_proj,
           final_norm):
    B, S, D = x.shape
    T = B * S
    depth = p.shape[0]
    n_a = a_norm.shape[0]
    bf = lambda t: t.astype(BF16)
    h = x.reshape(T, D)
    for i in range(depth):
        if i < n_a:
            w_in = a_w_in[i]
            n_main = w_in.shape[1] - 2 * M_HEADS
            w_gates = jnp.pad(w_in[:, n_main:], ((0, 0), (0, V7X_LANES - 2 * M_HEADS)))
            proj = rms_matmul(h, a_norm[i], bf(w_in[:, :n_main]), BF16)
            gates = rms_matmul(h, a_norm[i], bf(w_gates), F32)[:, :2 * M_HEADS]
            y = mlstm_scan(proj, gates, a_gate_bias[i], a_head_norm[i], B, S)
            h = matmul_residual(y, bf(a_w_out[i]), h)
        else:
            j = i - n_a
            if j == 0:
                kv = rms_matmul(h, kv_norm, bf(w_kv), BF16)
            q = rms_matmul(h, b_norm[j], bf(b_w_q[j]), BF16)
            o = swa_attention(q, kv, b_sinks[j], B, S)
            h = matmul_residual(o, bf(b_w_out[j]), h)
        h = peer_layer(h, c_norm[i], bf(peer_w_q[i]), bf(peer_k1[i]), bf(peer_k2[i]),
                       bf(peer_u[i]), bf(peer_v[i]))
        h = ple_gate(h, ple_norm[i], bf(ple_w_gate[i]), p[i].reshape(T, -1), bf(ple_w_proj[i]),
                     final_norm, final=(i == depth - 1))
    return h.reshape(B, S, D)
```

```python
import functools

import jax
import jax.numpy as jnp
from jax import lax
from jax.experimental import pallas as pl
from jax.experimental.pallas import tpu as pltpu

F32 = jnp.float32
BF16 = jnp.bfloat16

EPS = 1e-6
GATE_CAP = 15.0

M_HEADS = 4
A_HEAD_DIM = 64
A_GROUP = 8
WINDOW = 128
P_HEADS = 8
P_NKEYS = 128
P_TOPK = 16

V7X_LANES = 128
V7X_SUBLANES = 8
V7X_VMEM_BYTES = 64 * 1024 * 1024
VMEM_LIMIT = (V7X_VMEM_BYTES * 3) // 4

NT_DIMS = (((1,), (1,)), ((), ()))
TN_DIMS = (((0,), (0,)), ((), ()))


def _params(*sem):
    return pltpu.CompilerParams(dimension_semantics=sem, vmem_limit_bytes=VMEM_LIMIT)


def _rms(x, gain):
    ms = jnp.mean(x * x, axis=-1, keepdims=True)
    return x * lax.rsqrt(ms + EPS) * gain


def _tile(n, pref):
    t = min(n, pref)
    assert n % t == 0, (n, pref)
    return t


def _rms_matmul_body(x_ref, g_ref, w_ref, o_ref, xn_ref):
    @pl.when(pl.program_id(1) == 0)
    def _():
        xn_ref[...] = _rms(x_ref[...], g_ref[...]).astype(BF16)

    o_ref[...] = jnp.dot(xn_ref[...], w_ref[...], preferred_element_type=F32).astype(o_ref.dtype)


def rms_matmul(x, gain, w, out_dtype, tm=512, tn=512):
    T, D = x.shape
    N = w.shape[1]
    tm, tn = _tile(T, tm), _tile(N, tn)
    return pl.pallas_call(
        _rms_matmul_body,
        grid=(T // tm, N // tn),
        in_specs=[pl.BlockSpec((tm, D), lambda i, j: (i, 0)),
                  pl.BlockSpec((1, D), lambda i, j: (0, 0)),
                  pl.BlockSpec((D, tn), lambda i, j: (0, j))],
        out_specs=pl.BlockSpec((tm, tn), lambda i, j: (i, j)),
        out_shape=jax.ShapeDtypeStruct((T, N), out_dtype),
        scratch_shapes=[pltpu.VMEM((tm, D), BF16)],
        compiler_params=_params("parallel", "arbitrary"),
        name="rms_matmul",
    )(x, gain.reshape(1, D), w)


def _matmul_residual_body(y_ref, w_ref, h_ref, o_ref):
    o_ref[...] = h_ref[...] + jnp.dot(y_ref[...], w_ref[...], preferred_element_type=F32)


def matmul_residual(y, w, h, tm=512, tn=512):
    T, K = y.shape
    N = w.shape[1]
    tm, tn = _tile(T, tm), _tile(N, tn)
    return pl.pallas_call(
        _matmul_residual_body,
        grid=(T // tm, N // tn),
        in_specs=[pl.BlockSpec((tm, K), lambda i, j: (i, 0)),
                  pl.BlockSpec((K, tn), lambda i, j: (0, j)),
                  pl.BlockSpec((tm, tn), lambda i, j: (i, j))],
        out_specs=pl.BlockSpec((tm, tn), lambda i, j: (i, j)),
        out_shape=jax.ShapeDtypeStruct((T, N), F32),
        compiler_params=_params("parallel", "arbitrary"),
        name="matmul_residual",
    )(y, w, h)


def _ple_body(h_ref, g_ref, wg_ref, p_ref, wp_ref, fn_ref, o_ref, *, final):
    h = h_ref[...]
    xn = _rms(h, g_ref[...]).astype(BF16)
    gate = jax.nn.sigmoid(jnp.dot(xn, wg_ref[...], preferred_element_type=F32))
    emb = jnp.dot(p_ref[...].astype(BF16), wp_ref[...], preferred_element_type=F32)
    out = h + gate * emb
    if final:
        out = _rms(out, fn_ref[...])
    o_ref[...] = out


def ple_gate(h, gain, wg, p, wp, final_gain, final, tm=256):
    T, D = h.shape
    PD = p.shape[1]
    tm = _tile(T, tm)
    return pl.pallas_call(
        functools.partial(_ple_body, final=final),
        grid=(T // tm,),
        in_specs=[pl.BlockSpec((tm, D), lambda i: (i, 0)),
                  pl.BlockSpec((1, D), lambda i: (0, 0)),
                  pl.BlockSpec((D, D), lambda i: (0, 0)),
                  pl.BlockSpec((tm, PD), lambda i: (i, 0)),
                  pl.BlockSpec((PD, D), lambda i: (0, 0)),
                  pl.BlockSpec((1, D), lambda i: (0, 0))],
        out_specs=pl.BlockSpec((tm, D), lambda i: (i, 0)),
        out_shape=jax.ShapeDtypeStruct((T, D), F32),
        compiler_params=_params("parallel"),
        name="ple_gate",
    )(h, gain.reshape(1, D), wg, p, wp, final_gain.reshape(1, D))


def _soft_cap(t):
    return GATE_CAP * jnp.tanh(t / GATE_CAP)


def _log_sigmoid(t):
    return jnp.minimum(t, 0.0) - jnp.log(1.0 + jnp.exp(-jnp.abs(t)))


def _split3(x):
    hi = x.astype(BF16)
    r = x - hi.astype(F32)
    mid = r.astype(BF16)
    lo = (r - mid.astype(F32)).astype(BF16)
    return hi, mid, lo


def _mlstm_body(q_ref, k_ref, v_ref, og_ref, gc_ref, gr_ref, bc_ref, br_ref, hn_ref, o_ref,
                c_ref, n_ref, m_ref, *, L, DK):
    @pl.when(pl.program_id(1) == 0)
    def _():
        c_ref[...] = jnp.zeros_like(c_ref)
        n_ref[...] = jnp.zeros_like(n_ref)
        m_ref[...] = jnp.zeros_like(m_ref)

    q = (q_ref[...].astype(F32) * (DK ** -0.5)).astype(BF16)
    k = k_ref[...]
    v = v_ref[...]

    gc = gc_ref[...] + bc_ref[...]
    gr = gr_ref[...] + br_ref[...]
    li_c = _soft_cap(gc[:, 0:1])
    lf_c = _log_sigmoid(_soft_cap(gc[:, 1:2]))
    li_r = _soft_cap(gr[0:1, :])
    lf_r = _log_sigmoid(_soft_cap(gr[1:2, :]))

    row = lax.broadcasted_iota(jnp.int32, (L, L), 0)
    col = lax.broadcasted_iota(jnp.int32, (L, L), 1)
    causal = row >= col
    tri_l = causal.astype(BF16)
    tri_u = (row <= col).astype(BF16)
    b_c = sum(jnp.dot(tri_l, part, preferred_element_type=F32)
              for part in _split3(jnp.broadcast_to(lf_c, (L, V7X_LANES))))[:, 0:1]
    b_r = sum(jnp.dot(part, tri_u, preferred_element_type=F32)
              for part in _split3(jnp.broadcast_to(lf_r, (V7X_SUBLANES, L))))[0:1, :]

    m_prev = m_ref[0:1, 0:1]
    dmat = jnp.where(causal, b_c - b_r + li_r, -jnp.inf)
    inter = b_c + m_prev
    m_t = jnp.maximum(inter, jnp.max(dmat, axis=-1, keepdims=True))
    w_intra = jnp.exp(dmat - m_t)
    w_inter = jnp.exp(inter - m_t)

    s = lax.dot_general(q, k, NT_DIMS, preferred_element_type=F32) * w_intra
    c_old = c_ref[...]
    n_old = n_ref[...]
    num = (w_inter * jnp.dot(q, c_old.astype(BF16), preferred_element_type=F32)
           + jnp.dot(s.astype(BF16), v, preferred_element_type=F32))
    qn = lax.dot_general(q, jnp.broadcast_to(n_old, (V7X_LANES, DK)).astype(BF16), NT_DIMS,
                         preferred_element_type=F32)[:, 0:1]
    den = w_inter * qn + jnp.sum(s, axis=-1, keepdims=True)
    hh = num / jnp.maximum(jnp.abs(den), jnp.exp(-m_t))

    m_new = m_t[L - 1:L, :]
    b_last = b_c[L - 1:L, :]
    w_state = jnp.exp(b_last - b_c + li_c - m_new)
    decay = jnp.exp(b_last + m_prev - m_new)
    kw = k.astype(F32) * w_state
    c_ref[...] = decay * c_old + lax.dot_general(kw.astype(BF16), v, TN_DIMS, preferred_element_type=F32)
    n_ref[...] = decay * n_old + jnp.sum(kw, axis=0, keepdims=True)
    m_ref[...] = jnp.broadcast_to(m_new, m_ref.shape)

    ms = jnp.mean(hh * hh, axis=-1, keepdims=True)
    y = hh * lax.rsqrt(ms + EPS) * hn_ref[...] * jax.nn.sigmoid(og_ref[...].astype(F32))
    o_ref[...] = y.astype(o_ref.dtype)


def mlstm_scan(proj, gates, gate_bias, head_norm, B, S, chunk=256):
    T = proj.shape[0]
    H = M_HEADS
    HDV = head_norm.shape[0]
    DV = HDV // H
    DK = (proj.shape[1] - 2 * HDV) // (2 * H)
    L = _tile(S, chunk)
    NC = S // L
    assert DV % DK == 0 or DK % DV == 0
    k_off = (H * DK) // DK
    v_off = (2 * H * DK) // DV
    og_off = (2 * H * DK + H * DV) // DV
    g2 = gates.reshape(T, 2, H)
    g_col = g2.transpose(2, 0, 1)
    g_row = g2.transpose(2, 1, 0)
    b_col = gate_bias.T.reshape(H, 1, 2)
    b_row = gate_bias.T.reshape(H, 2, 1)

    def tok(bh, c):
        return (bh // H) * NC + c

    return pl.pallas_call(
        functools.partial(_mlstm_body, L=L, DK=DK),
        grid=(B * H, NC),
        in_specs=[pl.BlockSpec((L, DK), lambda bh, c: (tok(bh, c), bh % H)),
                  pl.BlockSpec((L, DK), lambda bh, c: (tok(bh, c), k_off + bh % H)),
                  pl.BlockSpec((L, DV), lambda bh, c: (tok(bh, c), v_off + bh % H)),
                  pl.BlockSpec((L, DV), lambda bh, c: (tok(bh, c), og_off + bh % H)),
                  pl.BlockSpec((None, L, 2), lambda bh, c: (bh % H, tok(bh, c), 0)),
                  pl.BlockSpec((None, 2, L), lambda bh, c: (bh % H, 0, tok(bh, c))),
                  pl.BlockSpec((None, 1, 2), lambda bh, c: (bh % H, 0, 0)),
                  pl.BlockSpec((None, 2, 1), lambda bh, c: (bh % H, 0, 0)),
                  pl.BlockSpec((1, DV), lambda bh, c: (0, bh % H))],
        out_specs=pl.BlockSpec((L, DV), lambda bh, c: (tok(bh, c), bh % H)),
        out_shape=jax.ShapeDtypeStruct((T, HDV), BF16),
        scratch_shapes=[pltpu.VMEM((DK, DV), F32),
                        pltpu.VMEM((1, DK), F32),
                        pltpu.VMEM((V7X_SUBLANES, V7X_LANES), F32)],
        compiler_params=_params("parallel", "arbitrary"),
        name="mlstm_scan",
    )(proj, proj, proj, proj, g_col, g_row, b_col, b_row, head_norm.reshape(1, HDV))


def _swa_body(sink_ref, q_ref, kp_ref, kc_ref, o_ref, *, BLK, KVH, GROUP, HD):
    first = pl.program_id(1) == 0
    kv = jnp.concatenate([kp_ref[...], kc_ref[...]], axis=0).astype(F32)
    ri = lax.broadcasted_iota(jnp.int32, (BLK, 2 * BLK), 0)
    ci = lax.broadcasted_iota(jnp.int32, (BLK, 2 * BLK), 1)
    dist = ri + BLK - ci
    mask = (dist >= 0) & (dist < WINDOW) & ((ci >= BLK) | jnp.logical_not(first))
    lo = lax.broadcasted_iota(jnp.int32, (1, 2 * HD), 1) < HD
    scale = HD ** -0.5
    for pair in range(KVH // 2):
        xk = kv[:, pair * 2 * HD:(pair + 1) * 2 * HD]
        xv = kv[:, (KVH + pair * 2) * HD:(KVH + pair * 2 + 2) * HD]
        xk_r = pltpu.roll(xk, HD, 1)
        xv_r = pltpu.roll(xv, HD, 1)
        for sub in range(2):
            kh = 2 * pair + sub
            if sub == 0:
                k2 = jnp.where(lo, xk, xk_r)
                v_lo = jnp.where(lo, xv, 0.0)
                v_hi = jnp.where(lo, 0.0, xv_r)
            else:
                k2 = jnp.where(lo, xk_r, xk)
                v_lo = jnp.where(lo, xv_r, 0.0)
                v_hi = jnp.where(lo, 0.0, xv)
            k2 = k2.astype(BF16)
            v_halves = (v_lo.astype(BF16), v_hi.astype(BF16))
            for gp in range(GROUP // 2):
                h0 = kh * GROUP + 2 * gp
                q2 = q_ref[:, h0 * HD:(h0 + 2) * HD]
                acc = None
                for e in range(2):
                    qm = jnp.where(lo if e == 0 else jnp.logical_not(lo), q2, jnp.zeros_like(q2))
                    s = lax.dot_general(qm, k2, NT_DIMS, preferred_element_type=F32) * scale
                    s = jnp.where(mask, s, -jnp.inf)
                    sink = sink_ref[h0 + e]
                    mx = jnp.maximum(jnp.max(s, axis=-1, keepdims=True), sink)
                    pr = jnp.exp(s - mx)
                    den = jnp.sum(pr, axis=-1, keepdims=True) + jnp.exp(sink - mx)
                    o = jnp.dot(pr.astype(BF16), v_halves[e], preferred_element_type=F32) / den
                    acc = o if acc is None else acc + o
                o_ref[:, h0 * HD:(h0 + 2) * HD] = acc.astype(o_ref.dtype)


def swa_attention(q, kv, sinks, B, S):
    T, QD = q.shape
    HD, GROUP, BLK = A_HEAD_DIM, A_GROUP, WINDOW
    KVH = kv.shape[1] // (2 * HD)
    assert QD == KVH * GROUP * HD and 2 * HD == V7X_LANES and KVH % 2 == 0 and GROUP % 2 == 0
    NB = S // BLK
    return pl.pallas_call(
        functools.partial(_swa_body, BLK=BLK, KVH=KVH, GROUP=GROUP, HD=HD),
        grid=(B, NB),
        in_specs=[pl.BlockSpec(memory_space=pltpu.SMEM),
                  pl.BlockSpec((BLK, QD), lambda b, n: (b * NB + n, 0)),
                  pl.BlockSpec((BLK, 2 * KVH * HD), lambda b, n: (b * NB + jnp.maximum(n - 1, 0), 0)),
                  pl.BlockSpec((BLK, 2 * KVH * HD), lambda b, n: (b * NB + n, 0))],
        out_specs=pl.BlockSpec((BLK, QD), lambda b, n: (b * NB + n, 0)),
        out_shape=jax.ShapeDtypeStruct((T, QD), BF16),
        compiler_params=_params("parallel", "arbitrary"),
        name="swa_attention",
    )(sinks, q, kv, kv)


KEY_BIG = 1 << 30


def _topk_rows(vals, keys, k):
    tm = vals.shape[1]
    slot = lax.broadcasted_iota(jnp.int32, (k, tm), 0)
    rows_v, rows_k = [], []
    arr_v = jnp.zeros((k, tm), vals.dtype)
    arr_k = jnp.zeros((k, tm), keys.dtype)
    for it in range(k):
        m = jnp.max(vals, axis=0, keepdims=True)
        kmin = jnp.min(jnp.where(vals == m, keys, KEY_BIG), axis=0, keepdims=True)
        vals = jnp.where(keys == kmin, -jnp.inf, vals)
        rows_v.append(m)
        rows_k.append(kmin)
        arr_v = jnp.where(slot == it, m, arr_v)
        arr_k = jnp.where(slot == it, kmin, arr_k)
    return rows_v, rows_k, arr_v, arr_k


def _candidates(top1, top2, k):
    r1_v, r1_i, v1, i1 = top1
    r2_v, r2_i, v2, i2 = top2
    tm = v1.shape[1]
    sub = V7X_SUBLANES
    vals, keys = [], []

    def pack(code, e1, e2):
        return (code << 14) | (e1 << 7) | e2

    b_iota = lax.broadcasted_iota(jnp.int32, (k, tm), 0)
    vals.append(r1_v[0] + v2)
    keys.append(pack(b_iota, r1_i[0], i2))
    a = 1
    while a < k and k // (a + 1) >= 2:
        nb = k // (a + 1)
        rows = -(-nb // sub) * sub
        bi = lax.broadcasted_iota(jnp.int32, (rows, tm), 0)
        vals.append(jnp.where(bi < nb, r1_v[a] + v2[0:rows], -jnp.inf))
        keys.append(pack(a * k + bi, r1_i[a], i2[0:rows]))
        a += 1
    if a < k:
        a0 = (a // sub) * sub
        ai = lax.broadcasted_iota(jnp.int32, (k - a0, tm), 0) + a0
        vals.append(jnp.where(ai >= a, v1[a0:k] + r2_v[0], -jnp.inf))
        keys.append(pack(ai * k, i1[a0:k], r2_i[0]))
    return jnp.concatenate(vals, axis=0), jnp.concatenate(keys, axis=0)


def _peer_select_body(h_ref, g_ref, wq_ref, k1_ref, k2_ref, xn_ref, gate_ref, e1_ref, e2_ref, qs_ref,
                      *, NK, HALF, K):
    hd = pl.program_id(1)

    @pl.when(hd == 0)
    def _():
        xn = _rms(h_ref[...], g_ref[...]).astype(BF16)
        xn_ref[...] = xn
        q = jnp.dot(xn, wq_ref[...], preferred_element_type=F32).astype(BF16)
        for c in range(qs_ref.shape[0]):
            qs_ref[c] = q[:, c * HALF:(c + 1) * HALF]

    q1 = qs_ref[2 * hd]
    q2 = qs_ref[2 * hd + 1]
    tm = q1.shape[0]
    s1 = lax.dot_general(k1_ref[...], q1, NT_DIMS, preferred_element_type=F32)
    s2 = lax.dot_general(k2_ref[...], q2, NT_DIMS, preferred_element_type=F32)
    key_iota = lax.broadcasted_iota(jnp.int32, (NK, tm), 0)
    cand, keys = _candidates(_topk_rows(s1, key_iota, K), _topk_rows(s2, key_iota, K), K)
    rows_s, _, top_s, top_key = _topk_rows(cand, keys, K)
    ex = jnp.exp(top_s - rows_s[0])
    gate_ref[...] = ex / jnp.sum(ex, axis=0, keepdims=True)
    e1_ref[...] = ((top_key >> 7) & 127).astype(F32)
    e2_ref[...] = (top_key & 127).astype(F32)


def peer_select(h, gain, wq, k1, k2, tm=256):
    T, D = h.shape
    PH, NK, HALF = k1.shape
    K = P_TOPK
    assert NK == V7X_LANES and HALF % V7X_LANES == 0 and wq.shape[1] == PH * 2 * HALF
    tm = _tile(T, tm)
    sel = jax.ShapeDtypeStruct((PH * K, T), F32)
    sel_spec = pl.BlockSpec((K, tm), lambda i, hd: (hd, i))
    return pl.pallas_call(
        functools.partial(_peer_select_body, NK=NK, HALF=HALF, K=K),
        grid=(T // tm, PH),
        in_specs=[pl.BlockSpec((tm, D), lambda i, hd: (i, 0)),
                  pl.BlockSpec((1, D), lambda i, hd: (0, 0)),
                  pl.BlockSpec((D, PH * 2 * HALF), lambda i, hd: (0, 0)),
                  pl.BlockSpec((None, NK, HALF), lambda i, hd: (hd, 0, 0)),
                  pl.BlockSpec((None, NK, HALF), lambda i, hd: (hd, 0, 0))],
        out_specs=[pl.BlockSpec((tm, D), lambda i, hd: (i, 0)), sel_spec, sel_spec, sel_spec],
        out_shape=[jax.ShapeDtypeStruct((T, D), BF16), sel, sel, sel],
        scratch_shapes=[pltpu.VMEM((2 * PH, tm, HALF), BF16)],
        compiler_params=_params("parallel", "arbitrary"),
        name="peer_select",
    )(h, gain.reshape(1, D), wq, k1, k2)


SCATTER_UNROLL = 16
SCATTER_PITCH_PAD = V7X_SUBLANES


def _peer_scatter_body(gate_ref, e1_ref, e2_ref, w_ref, scr_ref, gt_ref, e1t_ref, e2t_ref, *, NK):
    tmb = gt_ref.shape[0]
    pitch = NK + SCATTER_PITCH_PAD
    gt_ref[...] = gate_ref[...].T
    e1t_ref[...] = e1_ref[...].T
    e2t_ref[...] = e2_ref[...].T
    key_iota = lax.broadcasted_iota(jnp.int32, (NK, gt_ref.shape[1]), 0).astype(F32)

    def token_group(grp, carry):
        for u in range(SCATTER_UNROLL):
            t = grp * SCATTER_UNROLL + u
            g_row = gt_ref[pl.ds(t, 1), :]
            sel1 = jnp.where(e1t_ref[pl.ds(t, 1), :] == key_iota, g_row, 0.0).astype(BF16)
            sel2 = jnp.where(e2t_ref[pl.ds(t, 1), :] == key_iota, 1.0, 0.0).astype(BF16)
            w_t = lax.dot_general(sel1, sel2, NT_DIMS, preferred_element_type=F32)
            scr_ref[pl.ds(pl.multiple_of(t * pitch, V7X_SUBLANES), NK), :] = w_t
        return carry

    lax.fori_loop(0, tmb // SCATTER_UNROLL, token_group, 0)
    for j in range(NK):
        w_ref[j] = scr_ref[pl.ds(j, tmb, stride=pitch), :].astype(w_ref.dtype)


def peer_scatter(gate, e1, e2, NK, tmb=128):
    HK, T = gate.shape
    tmb = _tile(T, tmb)
    spec = pl.BlockSpec((HK, tmb), lambda i: (0, i))
    return pl.pallas_call(
        functools.partial(_peer_scatter_body, NK=NK),
        grid=(T // tmb,),
        in_specs=[spec, spec, spec],
        out_specs=pl.BlockSpec((NK, tmb, NK), lambda i: (0, i, 0)),
        out_shape=jax.ShapeDtypeStruct((NK, T, NK), BF16),
        scratch_shapes=[pltpu.VMEM((tmb * (NK + SCATTER_PITCH_PAD), NK), F32),
                        pltpu.VMEM((tmb, HK), F32),
                        pltpu.VMEM((tmb, HK), F32),
                        pltpu.VMEM((tmb, HK), F32)],
        compiler_params=_params("parallel"),
        name="peer_scatter",
    )(gate, e1, e2)


def _gelu(x):
    return 0.5 * x * (1.0 + lax.erf(x * (2.0 ** -0.5)))


def _peer_dense_body(xn_ref, u_ref, v_ref, w_ref, h_ref, o_ref, acc_ref, *, NK):
    j = pl.program_id(1)

    @pl.when(j == 0)
    def _():
        acc_ref[...] = jnp.zeros_like(acc_ref)

    act = lax.dot_general(xn_ref[...], u_ref[...], NT_DIMS, preferred_element_type=F32)
    coef = jnp.concatenate(
        [(w_ref[c].astype(F32) * _gelu(act[:, c * NK:(c + 1) * NK])).astype(BF16)
         for c in range(w_ref.shape[0])], axis=1)
    acc_ref[...] += jnp.dot(coef, v_ref[...], preferred_element_type=F32)

    @pl.when(j == pl.num_programs(1) - 1)
    def _():
        o_ref[...] = h_ref[...] + acc_ref[...]


def peer_dense(xn, u, v, w, h, tm=512, te=512):
    T, D = xn.shape
    E = u.shape[0]
    NK = w.shape[0]
    tm, te = _tile(T, tm), _tile(E, te)
    assert te % NK == 0 and E == NK * NK
    return pl.pallas_call(
        functools.partial(_peer_dense_body, NK=NK),
        grid=(T // tm, E // te),
        in_specs=[pl.BlockSpec((tm, D), lambda i, j: (i, 0)),
                  pl.BlockSpec((te, D), lambda i, j: (j, 0)),
                  pl.BlockSpec((te, D), lambda i, j: (j, 0)),
                  pl.BlockSpec((te // NK, tm, NK), lambda i, j: (j, i, 0)),
                  pl.BlockSpec((tm, D), lambda i, j: (i, 0))],
        out_specs=pl.BlockSpec((tm, D), lambda i, j: (i, 0)),
        out_shape=jax.ShapeDtypeStruct((T, D), F32),
        scratch_shapes=[pltpu.VMEM((tm, D), F32)],
        compiler_params=_params("parallel", "arbitrary"),
        name="peer_dense",
    )(xn, u, v, w, h)


def peer_layer(h, gain, wq, k1, k2, u, v):
    xn, gate, e1, e2 = peer_select(h, gain, wq, k1, k2)
    w = peer_scatter(gate, e1, e2, k1.shape[1])
    return peer_dense(xn, u, v, w, h)


def kernel(x, p, a_norm, a_w_in, a_gate_bias, a_head_norm, a_w_out, kv_norm, w_kv, b_norm, b_w_q, b_sinks,
           b_w_out, c_norm, peer_w_q, peer_k1, peer_k2, peer_u, peer_v, ple_norm, ple_w_gate, ple_w_proj,
           final_norm):
    B, S, D = x.shape
    T = B * S
    depth = p.shape[0]
    n_a = a_norm.shape[0]
    bf = lambda t: t.astype(BF16)
    h = x.reshape(T, D)
    for i in range(depth):
        if i < n_a:
            w_in = a_w_in[i]
            n_main = w_in.shape[1] - 2 * M_HEADS
            w_gates = jnp.pad(w_in[:, n_main:], ((0, 0), (0, V7X_LANES - 2 * M_HEADS)))
            proj = rms_matmul(h, a_norm[i], bf(w_in[:, :n_main]), BF16)
            gates = rms_matmul(h, a_norm[i], bf(w_gates), F32)[:, :2 * M_HEADS]
            y = mlstm_scan(proj, gates, a_gate_bias[i], a_head_norm[i], B, S)
            h = matmul_residual(y, bf(a_w_out[i]), h)
        else:
            j = i - n_a
            if j == 0:
                kv = rms_matmul(h, kv_norm, bf(w_kv), BF16)
            q = rms_matmul(h, b_norm[j], bf(b_w_q[j]), BF16)
            o = swa_attention(q, kv, b_sinks[j], B, S)
            h = matmul_residual(o, bf(b_w_out[j]), h)
        h = peer_layer(h, c_norm[i], bf(peer_w_q[i]), bf(peer_k1[i]), bf(peer_k2[i]),
                       bf(peer_u[i]), bf(peer_v[i]))
        h = ple_gate(h, ple_norm[i], bf(ple_w_gate[i]), p[i].reshape(T, -1), bf(ple_w_proj[i]),
                     final_norm, final=(i == depth - 1))
    return h.reshape(B, S, D)
```

```python
import functools

import jax
import jax.numpy as jnp
from jax import lax
from jax.experimental import pallas as pl
from jax.experimental.pallas import tpu as pltpu

F32 = jnp.float32
BF16 = jnp.bfloat16

EPS = 1e-6
GATE_CAP = 15.0

M_HEADS = 4
A_HEAD_DIM = 64
A_GROUP = 8
WINDOW = 128
P_HEADS = 8
P_NKEYS = 128
P_TOPK = 16

V7X_LANES = 128
V7X_SUBLANES = 8
V7X_VMEM_BYTES = 64 * 1024 * 1024
VMEM_LIMIT = (V7X_VMEM_BYTES * 3) // 4

NT_DIMS = (((1,), (1,)), ((), ()))
TN_DIMS = (((0,), (0,)), ((), ()))


def _params(*sem):
    return pltpu.CompilerParams(dimension_semantics=sem, vmem_limit_bytes=VMEM_LIMIT)


def _rms(x, gain):
    ms = jnp.mean(x * x, axis=-1, keepdims=True)
    return x * lax.rsqrt(ms + EPS) * gain


def _tile(n, pref):
    t = min(n, pref)
    assert n % t == 0, (n, pref)
    return t


def _rms_matmul_body(x_ref, g_ref, w_ref, o_ref, xn_ref):
    @pl.when(pl.program_id(1) == 0)
    def _():
        xn_ref[...] = _rms(x_ref[...], g_ref[...]).astype(BF16)

    o_ref[...] = jnp.dot(xn_ref[...], w_ref[...], preferred_element_type=F32).astype(o_ref.dtype)


def rms_matmul(x, gain, w, out_dtype, tm=512, tn=512):
    T, D = x.shape
    N = w.shape[1]
    tm, tn = _tile(T, tm), _tile(N, tn)
    return pl.pallas_call(
        _rms_matmul_body,
        grid=(T // tm, N // tn),
        in_specs=[pl.BlockSpec((tm, D), lambda i, j: (i, 0)),
                  pl.BlockSpec((1, D), lambda i, j: (0, 0)),
                  pl.BlockSpec((D, tn), lambda i, j: (0, j))],
        out_specs=pl.BlockSpec((tm, tn), lambda i, j: (i, j)),
        out_shape=jax.ShapeDtypeStruct((T, N), out_dtype),
        scratch_shapes=[pltpu.VMEM((tm, D), BF16)],
        compiler_params=_params("parallel", "arbitrary"),
        name="rms_matmul",
    )(x, gain.reshape(1, D), w)


def _matmul_residual_body(y_ref, w_ref, h_ref, o_ref):
    o_ref[...] = h_ref[...] + jnp.dot(y_ref[...], w_ref[...], preferred_element_type=F32)


def matmul_residual(y, w, h, tm=512, tn=512):
    T, K = y.shape
    N = w.shape[1]
    tm, tn = _tile(T, tm), _tile(N, tn)
    return pl.pallas_call(
        _matmul_residual_body,
        grid=(T // tm, N // tn),
        in_specs=[pl.BlockSpec((tm, K), lambda i, j: (i, 0)),
                  pl.BlockSpec((K, tn), lambda i, j: (0, j)),
                  pl.BlockSpec((tm, tn), lambda i, j: (i, j))],
        out_specs=pl.BlockSpec((tm, tn), lambda i, j: (i, j)),
        out_shape=jax.ShapeDtypeStruct((T, N), F32),
        compiler_params=_params("parallel", "arbitrary"),
        name="matmul_residual",
    )(y, w, h)


def _ple_body(h_ref, d_ref, g_ref, wg_ref, p_ref, wp_ref, fn_ref, o_ref, *, final):
    h = h_ref[...] + d_ref[...]
    xn = _rms(h, g_ref[...]).astype(BF16)
    gate = jax.nn.sigmoid(jnp.dot(xn, wg_ref[...], preferred_element_type=F32))
    emb = jnp.dot(p_ref[...].astype(BF16), wp_ref[...], preferred_element_type=F32)
    out = h + gate * emb
    if final:
        out = _rms(out, fn_ref[...])
    o_ref[...] = out


def ple_gate(h, peer_out, gain, wg, p, wp, final_gain, final, tm=256):
    T, D = h.shape
    PD = p.shape[1]
    tm = _tile(T, tm)
    return pl.pallas_call(
        functools.partial(_ple_body, final=final),
        grid=(T // tm,),
        in_specs=[pl.BlockSpec((tm, D), lambda i: (i, 0)),
                  pl.BlockSpec((tm, D), lambda i: (i, 0)),
                  pl.BlockSpec((1, D), lambda i: (0, 0)),
                  pl.BlockSpec((D, D), lambda i: (0, 0)),
                  pl.BlockSpec((tm, PD), lambda i: (i, 0)),
                  pl.BlockSpec((PD, D), lambda i: (0, 0)),
                  pl.BlockSpec((1, D), lambda i: (0, 0))],
        out_specs=pl.BlockSpec((tm, D), lambda i: (i, 0)),
        out_shape=jax.ShapeDtypeStruct((T, D), F32),
        compiler_params=_params("parallel"),
        name="ple_gate",
    )(h, peer_out, gain.reshape(1, D), wg, p, wp, final_gain.reshape(1, D))


def _soft_cap(t):
    return GATE_CAP * jnp.tanh(t / GATE_CAP)


def _log_sigmoid(t):
    return jnp.minimum(t, 0.0) - jnp.log(1.0 + jnp.exp(-jnp.abs(t)))


def _split3(x):
    hi = x.astype(BF16)
    r = x - hi.astype(F32)
    mid = r.astype(BF16)
    lo = (r - mid.astype(F32)).astype(BF16)
    return hi, mid, lo


def _mlstm_body(q_ref, k_ref, v_ref, og_ref, gc_ref, gr_ref, bc_ref, br_ref, hn_ref, o_ref,
                c_ref, n_ref, m_ref, *, L, DK):
    @pl.when(pl.program_id(1) == 0)
    def _():
        c_ref[...] = jnp.zeros_like(c_ref)
        n_ref[...] = jnp.zeros_like(n_ref)
        m_ref[...] = jnp.zeros_like(m_ref)

    q = (q_ref[...].astype(F32) * (DK ** -0.5)).astype(BF16)
    k = k_ref[...]
    v = v_ref[...]

    gc = gc_ref[...] + bc_ref[...]
    gr = gr_ref[...] + br_ref[...]
    li_c = _soft_cap(gc[:, 0:1])
    lf_c = _log_sigmoid(_soft_cap(gc[:, 1:2]))
    li_r = _soft_cap(gr[0:1, :])
    lf_r = _log_sigmoid(_soft_cap(gr[1:2, :]))

    row = lax.broadcasted_iota(jnp.int32, (L, L), 0)
    col = lax.broadcasted_iota(jnp.int32, (L, L), 1)
    causal = row >= col
    tri_l = causal.astype(BF16)
    tri_u = (row <= col).astype(BF16)
    b_c = sum(jnp.dot(tri_l, part, preferred_element_type=F32)
              for part in _split3(jnp.broadcast_to(lf_c, (L, V7X_LANES))))[:, 0:1]
    b_r = sum(jnp.dot(part, tri_u, preferred_element_type=F32)
              for part in _split3(jnp.broadcast_to(lf_r, (V7X_SUBLANES, L))))[0:1, :]

    m_prev = m_ref[0:1, 0:1]
    dmat = jnp.where(causal, b_c - b_r + li_r, -jnp.inf)
    inter = b_c + m_prev
    m_t = jnp.maximum(inter, jnp.max(dmat, axis=-1, keepdims=True))
    w_intra = jnp.exp(dmat - m_t)
    w_inter = jnp.exp(inter - m_t)

    s = lax.dot_general(q, k, NT_DIMS, preferred_element_type=F32) * w_intra
    c_old = c_ref[...]
    n_old = n_ref[...]
    num = (w_inter * jnp.dot(q, c_old.astype(BF16), preferred_element_type=F32)
           + jnp.dot(s.astype(BF16), v, preferred_element_type=F32))
    qn = lax.dot_general(q, jnp.broadcast_to(n_old, (V7X_LANES, DK)).astype(BF16), NT_DIMS,
                         preferred_element_type=F32)[:, 0:1]
    den = w_inter * qn + jnp.sum(s, axis=-1, keepdims=True)
    hh = num / jnp.maximum(jnp.abs(den), jnp.exp(-m_t))

    m_new = m_t[L - 1:L, :]
    b_last = b_c[L - 1:L, :]
    w_state = jnp.exp(b_last - b_c + li_c - m_new)
    decay = jnp.exp(b_last + m_prev - m_new)
    kw = k.astype(F32) * w_state
    c_ref[...] = decay * c_old + lax.dot_general(kw.astype(BF16), v, TN_DIMS, preferred_element_type=F32)
    n_ref[...] = decay * n_old + jnp.sum(kw, axis=0, keepdims=True)
    m_ref[...] = jnp.broadcast_to(m_new, m_ref.shape)

    ms = jnp.mean(hh * hh, axis=-1, keepdims=True)
    y = hh * lax.rsqrt(ms + EPS) * hn_ref[...] * jax.nn.sigmoid(og_ref[...].astype(F32))
    o_ref[...] = y.astype(o_ref.dtype)


def mlstm_scan(proj, gates, gate_bias, head_norm, B, S, chunk=256):
    T = proj.shape[0]
    H = M_HEADS
    HDV = head_norm.shape[0]
    DV = HDV // H
    DK = (proj.shape[1] - 2 * HDV) // (2 * H)
    L = _tile(S, chunk)
    NC = S // L
    assert DV % DK == 0 or DK % DV == 0
    k_off = (H * DK) // DK
    v_off = (2 * H * DK) // DV
    og_off = (2 * H * DK + H * DV) // DV
    g2 = gates.reshape(T, 2, H)
    g_col = g2.transpose(2, 0, 1)
    g_row = g2.transpose(2, 1, 0)
    b_col = gate_bias.T.reshape(H, 1, 2)
    b_row = gate_bias.T.reshape(H, 2, 1)

    def tok(bh, c):
        return (bh // H) * NC + c

    return pl.pallas_call(
        functools.partial(_mlstm_body, L=L, DK=DK),
        grid=(B * H, NC),
        in_specs=[pl.BlockSpec((L, DK), lambda bh, c: (tok(bh, c), bh % H)),
                  pl.BlockSpec((L, DK), lambda bh, c: (tok(bh, c), k_off + bh % H)),
                  pl.BlockSpec((L, DV), lambda bh, c: (tok(bh, c), v_off + bh % H)),
                  pl.BlockSpec((L, DV), lambda bh, c: (tok(bh, c), og_off + bh % H)),
                  pl.BlockSpec((None, L, 2), lambda bh, c: (bh % H, tok(bh, c), 0)),
                  pl.BlockSpec((None, 2, L), lambda bh, c: (bh % H, 0, tok(bh, c))),
                  pl.BlockSpec((None, 1, 2), lambda bh, c: (bh % H, 0, 0)),
                  pl.BlockSpec((None, 2, 1), lambda bh, c: (bh % H, 0, 0)),
                  pl.BlockSpec((1, DV), lambda bh, c: (0, bh % H))],
        out_specs=pl.BlockSpec((L, DV), lambda bh, c: (tok(bh, c), bh % H)),
        out_shape=jax.ShapeDtypeStruct((T, HDV), BF16),
        scratch_shapes=[pltpu.VMEM((DK, DV), F32),
                        pltpu.VMEM((1, DK), F32),
                        pltpu.VMEM((V7X_SUBLANES, V7X_LANES), F32)],
        compiler_params=_params("parallel", "arbitrary"),
        name="mlstm_scan",
    )(proj, proj, proj, proj, g_col, g_row, b_col, b_row, head_norm.reshape(1, HDV))


def _swa_body(sink_ref, q_ref, kp_ref, kc_ref, o_ref, *, BLK, KVH, GROUP, HD):
    first = pl.program_id(1) == 0
    kv = jnp.concatenate([kp_ref[...], kc_ref[...]], axis=0).astype(F32)
    ri = lax.broadcasted_iota(jnp.int32, (BLK, 2 * BLK), 0)
    ci = lax.broadcasted_iota(jnp.int32, (BLK, 2 * BLK), 1)
    dist = ri + BLK - ci
    mask = (dist >= 0) & (dist < WINDOW) & ((ci >= BLK) | jnp.logical_not(first))
    lo = lax.broadcasted_iota(jnp.int32, (1, 2 * HD), 1) < HD
    scale = HD ** -0.5
    for pair in range(KVH // 2):
        xk = kv[:, pair * 2 * HD:(pair + 1) * 2 * HD]
        xv = kv[:, (KVH + pair * 2) * HD:(KVH + pair * 2 + 2) * HD]
        xk_r = pltpu.roll(xk, HD, 1)
        xv_r = pltpu.roll(xv, HD, 1)
        for sub in range(2):
            kh = 2 * pair + sub
            if sub == 0:
                k2 = jnp.where(lo, xk, xk_r)
                v_lo = jnp.where(lo, xv, 0.0)
                v_hi = jnp.where(lo, 0.0, xv_r)
            else:
                k2 = jnp.where(lo, xk_r, xk)
                v_lo = jnp.where(lo, xv_r, 0.0)
                v_hi = jnp.where(lo, 0.0, xv)
            k2 = k2.astype(BF16)
            v_halves = (v_lo.astype(BF16), v_hi.astype(BF16))
            for gp in range(GROUP // 2):
                h0 = kh * GROUP + 2 * gp
                q2 = q_ref[:, h0 * HD:(h0 + 2) * HD]
                acc = None
                for e in range(2):
                    qm = jnp.where(lo if e == 0 else jnp.logical_not(lo), q2, jnp.zeros_like(q2))
                    s = lax.dot_general(qm, k2, NT_DIMS, preferred_element_type=F32) * scale
                    s = jnp.where(mask, s, -jnp.inf)
                    sink = sink_ref[h0 + e]
                    mx = jnp.maximum(jnp.max(s, axis=-1, keepdims=True), sink)
                    pr = jnp.exp(s - mx)
                    den = jnp.sum(pr, axis=-1, keepdims=True) + jnp.exp(sink - mx)
                    o = jnp.dot(pr.astype(BF16), v_halves[e], preferred_element_type=F32) / den
                    acc = o if acc is None else acc + o
                o_ref[:, h0 * HD:(h0 + 2) * HD] = acc.astype(o_ref.dtype)


def swa_attention(q, kv, sinks, B, S):
    T, QD = q.shape
    HD, GROUP, BLK = A_HEAD_DIM, A_GROUP, WINDOW
    KVH = kv.shape[1] // (2 * HD)
    assert QD == KVH * GROUP * HD and 2 * HD == V7X_LANES and KVH % 2 == 0 and GROUP % 2 == 0
    NB = S // BLK
    return pl.pallas_call(
        functools.partial(_swa_body, BLK=BLK, KVH=KVH, GROUP=GROUP, HD=HD),
        grid=(B, NB),
        in_specs=[pl.BlockSpec(memory_space=pltpu.SMEM),
                  pl.BlockSpec((BLK, QD), lambda b, n: (b * NB + n, 0)),
                  pl.BlockSpec((BLK, 2 * KVH * HD), lambda b, n: (b * NB + jnp.maximum(n - 1, 0), 0)),
                  pl.BlockSpec((BLK, 2 * KVH * HD), lambda b, n: (b * NB + n, 0))],
        out_specs=pl.BlockSpec((BLK, QD), lambda b, n: (b * NB + n, 0)),
        out_shape=jax.ShapeDtypeStruct((T, QD), BF16),
        compiler_params=_params("parallel", "arbitrary"),
        name="swa_attention",
    )(sinks, q, kv, kv)


KEY_BIG = 1 << 30


def _topk_rows(problems, k):
    tm = problems[0][0].shape[1]
    slot = lax.broadcasted_iota(jnp.int32, (k, tm), 0)
    state = [[vals, keys, [], [], jnp.zeros((k, tm), vals.dtype), jnp.zeros((k, tm), keys.dtype)]
             for vals, keys in problems]
    for it in range(k):
        for st in state:
            vals, keys = st[0], st[1]
            m = jnp.max(vals, axis=0, keepdims=True)
            kmin = jnp.min(jnp.where(vals == m, keys, KEY_BIG), axis=0, keepdims=True)
            st[0] = jnp.where(keys == kmin, -jnp.inf, vals)
            st[2].append(m)
            st[3].append(kmin)
            st[4] = jnp.where(slot == it, m, st[4])
            st[5] = jnp.where(slot == it, kmin, st[5])
    return [tuple(st[2:]) for st in state]


def _candidates(top1, top2, k):
    r1_v, r1_i, v1, i1 = top1
    r2_v, r2_i, v2, i2 = top2
    tm = v1.shape[1]
    sub = V7X_SUBLANES
    vals, keys = [], []

    def pack(code, e1, e2):
        return (code << 14) | (e1 << 7) | e2

    b_iota = lax.broadcasted_iota(jnp.int32, (k, tm), 0)
    vals.append(r1_v[0] + v2)
    keys.append(pack(b_iota, r1_i[0], i2))
    a = 1
    while a < k and k // (a + 1) >= 2:
        nb = k // (a + 1)
        rows = -(-nb // sub) * sub
        bi = lax.broadcasted_iota(jnp.int32, (rows, tm), 0)
        vals.append(jnp.where(bi < nb, r1_v[a] + v2[0:rows], -jnp.inf))
        keys.append(pack(a * k + bi, r1_i[a], i2[0:rows]))
        a += 1
    if a < k:
        a0 = (a // sub) * sub
        ai = lax.broadcasted_iota(jnp.int32, (k - a0, tm), 0) + a0
        vals.append(jnp.where(ai >= a, v1[a0:k] + r2_v[0], -jnp.inf))
        keys.append(pack(ai * k, i1[a0:k], r2_i[0]))
    return jnp.concatenate(vals, axis=0), jnp.concatenate(keys, axis=0)


def _peer_select_body(h_ref, g_ref, wq_ref, k1_ref, k2_ref, xn_ref, gate_ref, e1_ref, e2_ref, qs_ref,
                      *, NK, HALF, K):
    hd = pl.program_id(1)

    @pl.when(hd == 0)
    def _():
        xn = _rms(h_ref[...], g_ref[...]).astype(BF16)
        xn_ref[...] = xn
        q = jnp.dot(xn, wq_ref[...], preferred_element_type=F32).astype(BF16)
        for c in range(qs_ref.shape[0]):
            qs_ref[c] = q[:, c * HALF:(c + 1) * HALF]

    tl = V7X_LANES
    key_iota = lax.broadcasted_iota(jnp.int32, (NK, tl), 0)
    for c in range(qs_ref.shape[1] // tl):
        q1 = qs_ref[2 * hd, pl.ds(c * tl, tl), :]
        q2 = qs_ref[2 * hd + 1, pl.ds(c * tl, tl), :]
        s1 = lax.dot_general(k1_ref[...], q1, NT_DIMS, preferred_element_type=F32)
        s2 = lax.dot_general(k2_ref[...], q2, NT_DIMS, preferred_element_type=F32)
        top1, top2 = _topk_rows([(s1, key_iota), (s2, key_iota)], K)
        rows_s, _, top_s, top_key = _topk_rows([_candidates(top1, top2, K)], K)[0]
        ex = jnp.exp(top_s - rows_s[0])
        gate_ref[:, c * tl:(c + 1) * tl] = ex / jnp.sum(ex, axis=0, keepdims=True)
        e1_ref[:, c * tl:(c + 1) * tl] = ((top_key >> 7) & 127).astype(F32)
        e2_ref[:, c * tl:(c + 1) * tl] = (top_key & 127).astype(F32)


def peer_select(h, gain, wq, k1, k2, tm=256):
    T, D = h.shape
    PH, NK, HALF = k1.shape
    K = P_TOPK
    assert NK == V7X_LANES and HALF % V7X_LANES == 0 and wq.shape[1] == PH * 2 * HALF
    tm = _tile(T, tm)
    sel = jax.ShapeDtypeStruct((PH * K, T), F32)
    sel_spec = pl.BlockSpec((K, tm), lambda i, hd: (hd, i))
    return pl.pallas_call(
        functools.partial(_peer_select_body, NK=NK, HALF=HALF, K=K),
        grid=(T // tm, PH),
        in_specs=[pl.BlockSpec((tm, D), lambda i, hd: (i, 0)),
                  pl.BlockSpec((1, D), lambda i, hd: (0, 0)),
                  pl.BlockSpec((D, PH * 2 * HALF), lambda i, hd: (0, 0)),
                  pl.BlockSpec((None, NK, HALF), lambda i, hd: (hd, 0, 0)),
                  pl.BlockSpec((None, NK, HALF), lambda i, hd: (hd, 0, 0))],
        out_specs=[pl.BlockSpec((tm, D), lambda i, hd: (i, 0)), sel_spec, sel_spec, sel_spec],
        out_shape=[jax.ShapeDtypeStruct((T, D), BF16), sel, sel, sel],
        scratch_shapes=[pltpu.VMEM((2 * PH, tm, HALF), BF16)],
        compiler_params=_params("parallel", "arbitrary"),
        name="peer_select",
    )(h, gain.reshape(1, D), wq, k1, k2)


SCATTER_UNROLL = 16
SCATTER_PITCH_PAD = V7X_SUBLANES


def _peer_scatter_body(gate_ref, e1_ref, e2_ref, w_ref, scr_ref, gt_ref, e1t_ref, e2t_ref, *, NK):
    tmb = gt_ref.shape[0]
    pitch = NK + SCATTER_PITCH_PAD
    gt_ref[...] = gate_ref[...].T
    e1t_ref[...] = e1_ref[...].T
    e2t_ref[...] = e2_ref[...].T
    key_iota = lax.broadcasted_iota(jnp.int32, (NK, gt_ref.shape[1]), 0).astype(F32)

    def token_group(grp, carry):
        for u in range(SCATTER_UNROLL):
            t = grp * SCATTER_UNROLL + u
            g_row = gt_ref[pl.ds(t, 1), :]
            sel1 = jnp.where(e1t_ref[pl.ds(t, 1), :] == key_iota, g_row, 0.0).astype(BF16)
            sel2 = jnp.where(e2t_ref[pl.ds(t, 1), :] == key_iota, 1.0, 0.0).astype(BF16)
            w_t = lax.dot_general(sel1, sel2, NT_DIMS, preferred_element_type=F32)
            scr_ref[pl.ds(pl.multiple_of(t * pitch, V7X_SUBLANES), NK), :] = w_t
        return carry

    lax.fori_loop(0, tmb // SCATTER_UNROLL, token_group, 0)
    for j in range(NK):
        w_ref[j] = scr_ref[pl.ds(j, tmb, stride=pitch), :].astype(w_ref.dtype)


def peer_scatter(gate, e1, e2, NK, tmb=128):
    HK, T = gate.shape
    tmb = _tile(T, tmb)
    spec = pl.BlockSpec((HK, tmb), lambda i: (0, i))
    return pl.pallas_call(
        functools.partial(_peer_scatter_body, NK=NK),
        grid=(T // tmb,),
        in_specs=[spec, spec, spec],
        out_specs=pl.BlockSpec((NK, tmb, NK), lambda i: (0, i, 0)),
        out_shape=jax.ShapeDtypeStruct((NK, T, NK), BF16),
        scratch_shapes=[pltpu.VMEM((tmb * (NK + SCATTER_PITCH_PAD), NK), F32),
                        pltpu.VMEM((tmb, HK), F32),
                        pltpu.VMEM((tmb, HK), F32),
                        pltpu.VMEM((tmb, HK), F32)],
        compiler_params=_params("parallel"),
        name="peer_scatter",
    )(gate, e1, e2)


def _gelu(x):
    return 0.5 * x * (1.0 + lax.erf(x * (2.0 ** -0.5)))


def _peer_dense_body(xn_ref, u_ref, v_ref, w_ref, o_ref, *, NK):
    @pl.when(pl.program_id(1) == 0)
    def _():
        o_ref[...] = jnp.zeros_like(o_ref)

    act = lax.dot_general(xn_ref[...], u_ref[...], NT_DIMS, preferred_element_type=F32)
    coef = jnp.concatenate(
        [(w_ref[c].astype(F32) * _gelu(act[:, c * NK:(c + 1) * NK])).astype(BF16)
         for c in range(w_ref.shape[0])], axis=1)
    o_ref[...] += jnp.dot(coef, v_ref[...], preferred_element_type=F32)


def peer_dense(xn, u, v, w, tm=1024, te=512):
    T, D = xn.shape
    E = u.shape[0]
    NK = w.shape[0]
    tm, te = _tile(T, tm), _tile(E, te)
    assert te % NK == 0 and E == NK * NK
    return pl.pallas_call(
        functools.partial(_peer_dense_body, NK=NK),
        grid=(T // tm, E // te),
        in_specs=[pl.BlockSpec((tm, D), lambda i, j: (i, 0)),
                  pl.BlockSpec((te, D), lambda i, j: (j, 0)),
                  pl.BlockSpec((te, D), lambda i, j: (j, 0)),
                  pl.BlockSpec((te // NK, tm, NK), lambda i, j: (j, i, 0))],
        out_specs=pl.BlockSpec((tm, D), lambda i, j: (i, 0)),
        out_shape=jax.ShapeDtypeStruct((T, D), F32),
        compiler_params=_params("parallel", "arbitrary"),
        name="peer_dense",
    )(xn, u, v, w)


def peer_layer(h, gain, wq, k1, k2, u, v):
    xn, gate, e1, e2 = peer_select(h, gain, wq, k1, k2)
    w = peer_scatter(gate, e1, e2, k1.shape[1])
    return peer_dense(xn, u, v, w)


def kernel(x, p, a_norm, a_w_in, a_gate_bias, a_head_norm, a_w_out, kv_norm, w_kv, b_norm, b_w_q, b_sinks,
           b_w_out, c_norm, peer_w_q, peer_k1, peer_k2, peer_u, peer_v, ple_norm, ple_w_gate, ple_w_proj,
           final_norm):
    B, S, D = x.shape
    T = B * S
    depth = p.shape[0]
    n_a = a_norm.shape[0]
    bf = lambda t: t.astype(BF16)
    h = x.reshape(T, D)
    for i in range(depth):
        if i < n_a:
            w_in = a_w_in[i]
            n_main = w_in.shape[1] - 2 * M_HEADS
            w_gates = jnp.pad(w_in[:, n_main:], ((0, 0), (0, V7X_LANES - 2 * M_HEADS)))
            proj = rms_matmul(h, a_norm[i], bf(w_in[:, :n_main]), BF16, tm=1024)
            gates = rms_matmul(h, a_norm[i], bf(w_gates), F32)[:, :2 * M_HEADS]
            y = mlstm_scan(proj, gates, a_gate_bias[i], a_head_norm[i], B, S)
            h = matmul_residual(y, bf(a_w_out[i]), h)
        else:
            j = i - n_a
            if j == 0:
                kv = rms_matmul(h, kv_norm, bf(w_kv), BF16)
            q = rms_matmul(h, b_norm[j], bf(b_w_q[j]), BF16)
            o = swa_attention(q, kv, b_sinks[j], B, S)
            h = matmul_residual(o, bf(b_w_out[j]), h)
        peer_out = peer_layer(h, c_norm[i], bf(peer_w_q[i]), bf(peer_k1[i]), bf(peer_k2[i]),
                              bf(peer_u[i]), bf(peer_v[i]))
        h = ple_gate(h, peer_out, ple_norm[i], bf(ple_w_gate[i]), p[i].reshape(T, -1), bf(ple_w_proj[i]),
                     final_norm, final=(i == depth - 1))
    return h.reshape(B, S, D)
```

```python
import functools

import jax
import jax.numpy as jnp
from jax import lax
from jax.experimental import pallas as pl
from jax.experimental.pallas import tpu as pltpu

F32 = jnp.float32
BF16 = jnp.bfloat16

EPS = 1e-6
GATE_CAP = 15.0

M_HEADS = 4
A_HEAD_DIM = 64
A_GROUP = 8
WINDOW = 128
P_HEADS = 8
P_NKEYS = 128
P_TOPK = 16

V7X_LANES = 128
V7X_SUBLANES = 8
V7X_VMEM_BYTES = 64 * 1024 * 1024
VMEM_LIMIT = (V7X_VMEM_BYTES * 3) // 4

NT_DIMS = (((1,), (1,)), ((), ()))
TN_DIMS = (((0,), (0,)), ((), ()))


def _params(*sem):
    return pltpu.CompilerParams(dimension_semantics=sem, vmem_limit_bytes=VMEM_LIMIT)


def _rms(x, gain):
    ms = jnp.mean(x * x, axis=-1, keepdims=True)
    return x * lax.rsqrt(ms + EPS) * gain


def _tile(n, pref):
    t = min(n, pref)
    assert n % t == 0, (n, pref)
    return t


def _rms_matmul_body(x_ref, g_ref, w_ref, o_ref, xn_ref):
    @pl.when(pl.program_id(1) == 0)
    def _():
        xn_ref[...] = _rms(x_ref[...], g_ref[...]).astype(BF16)

    o_ref[...] = jnp.dot(xn_ref[...], w_ref[...], preferred_element_type=F32).astype(o_ref.dtype)


def rms_matmul(x, gain, w, out_dtype, tm=512, tn=512):
    T, D = x.shape
    N = w.shape[1]
    tm, tn = _tile(T, tm), _tile(N, tn)
    return pl.pallas_call(
        _rms_matmul_body,
        grid=(T // tm, N // tn),
        in_specs=[pl.BlockSpec((tm, D), lambda i, j: (i, 0)),
                  pl.BlockSpec((1, D), lambda i, j: (0, 0)),
                  pl.BlockSpec((D, tn), lambda i, j: (0, j))],
        out_specs=pl.BlockSpec((tm, tn), lambda i, j: (i, j)),
        out_shape=jax.ShapeDtypeStruct((T, N), out_dtype),
        scratch_shapes=[pltpu.VMEM((tm, D), BF16)],
        compiler_params=_params("parallel", "arbitrary"),
        name="rms_matmul",
    )(x, gain.reshape(1, D), w)


def _matmul_residual_body(y_ref, w_ref, h_ref, o_ref):
    o_ref[...] = h_ref[...] + jnp.dot(y_ref[...], w_ref[...], preferred_element_type=F32)


def matmul_residual(y, w, h, tm=1024, tn=1024):
    T, K = y.shape
    N = w.shape[1]
    tm, tn = _tile(T, tm), _tile(N, tn)
    return pl.pallas_call(
        _matmul_residual_body,
        grid=(T // tm, N // tn),
        in_specs=[pl.BlockSpec((tm, K), lambda i, j: (i, 0)),
                  pl.BlockSpec((K, tn), lambda i, j: (0, j)),
                  pl.BlockSpec((tm, tn), lambda i, j: (i, j))],
        out_specs=pl.BlockSpec((tm, tn), lambda i, j: (i, j)),
        out_shape=jax.ShapeDtypeStruct((T, N), F32),
        compiler_params=_params("parallel", "arbitrary"),
        name="matmul_residual",
    )(y, w, h)


def _ple_body(h_ref, d_ref, g_ref, wg_ref, p_ref, wp_ref, fn_ref, o_ref, *, final):
    h = h_ref[...] + d_ref[...]
    xn = _rms(h, g_ref[...]).astype(BF16)
    gate = jax.nn.sigmoid(jnp.dot(xn, wg_ref[...], preferred_element_type=F32))
    emb = jnp.dot(p_ref[...].astype(BF16), wp_ref[...], preferred_element_type=F32)
    out = h + gate * emb
    if final:
        out = _rms(out, fn_ref[...])
    o_ref[...] = out


def ple_gate(h, peer_out, gain, wg, p, wp, final_gain, final, tm=256):
    T, D = h.shape
    PD = p.shape[1]
    tm = _tile(T, tm)
    return pl.pallas_call(
        functools.partial(_ple_body, final=final),
        grid=(T // tm,),
        in_specs=[pl.BlockSpec((tm, D), lambda i: (i, 0)),
                  pl.BlockSpec((tm, D), lambda i: (i, 0)),
                  pl.BlockSpec((1, D), lambda i: (0, 0)),
                  pl.BlockSpec((D, D), lambda i: (0, 0)),
                  pl.BlockSpec((tm, PD), lambda i: (i, 0)),
                  pl.BlockSpec((PD, D), lambda i: (0, 0)),
                  pl.BlockSpec((1, D), lambda i: (0, 0))],
        out_specs=pl.BlockSpec((tm, D), lambda i: (i, 0)),
        out_shape=jax.ShapeDtypeStruct((T, D), F32),
        compiler_params=_params("parallel"),
        name="ple_gate",
    )(h, peer_out, gain.reshape(1, D), wg, p, wp, final_gain.reshape(1, D))


def _soft_cap(t):
    return GATE_CAP * jnp.tanh(t / GATE_CAP)


def _log_sigmoid(t):
    return jnp.minimum(t, 0.0) - jnp.log(1.0 + jnp.exp(-jnp.abs(t)))


def _split3(x):
    hi = x.astype(BF16)
    r = x - hi.astype(F32)
    mid = r.astype(BF16)
    lo = (r - mid.astype(F32)).astype(BF16)
    return hi, mid, lo


def _mlstm_body(q_ref, k_ref, v_ref, og_ref, gc_ref, gr_ref, bc_ref, br_ref, hn_ref, o_ref,
                c_ref, n_ref, m_ref, *, L, DK):
    @pl.when(pl.program_id(1) == 0)
    def _():
        c_ref[...] = jnp.zeros_like(c_ref)
        n_ref[...] = jnp.zeros_like(n_ref)
        m_ref[...] = jnp.zeros_like(m_ref)

    q = (q_ref[...].astype(F32) * (DK ** -0.5)).astype(BF16)
    k = k_ref[...]
    v = v_ref[...]

    gc = gc_ref[...] + bc_ref[...]
    gr = gr_ref[...] + br_ref[...]
    li_c = _soft_cap(gc[:, 0:1])
    lf_c = _log_sigmoid(_soft_cap(gc[:, 1:2]))
    li_r = _soft_cap(gr[0:1, :])
    lf_r = _log_sigmoid(_soft_cap(gr[1:2, :]))

    row = lax.broadcasted_iota(jnp.int32, (L, L), 0)
    col = lax.broadcasted_iota(jnp.int32, (L, L), 1)
    causal = row >= col
    tri_l = causal.astype(BF16)
    tri_u = (row <= col).astype(BF16)
    b_c = sum(jnp.dot(tri_l, part, preferred_element_type=F32)
              for part in _split3(jnp.broadcast_to(lf_c, (L, V7X_LANES))))[:, 0:1]
    b_r = sum(jnp.dot(part, tri_u, preferred_element_type=F32)
              for part in _split3(jnp.broadcast_to(lf_r, (V7X_SUBLANES, L))))[0:1, :]

    m_prev = m_ref[0:1, 0:1]
    dmat = jnp.where(causal, b_c - b_r + li_r, -jnp.inf)
    inter = b_c + m_prev
    m_t = jnp.maximum(inter, jnp.max(dmat, axis=-1, keepdims=True))
    w_intra = jnp.exp(dmat - m_t)
    w_inter = jnp.exp(inter - m_t)

    s = lax.dot_general(q, k, NT_DIMS, preferred_element_type=F32) * w_intra
    c_old = c_ref[...]
    n_old = n_ref[...]
    num = (w_inter * jnp.dot(q, c_old.astype(BF16), preferred_element_type=F32)
           + jnp.dot(s.astype(BF16), v, preferred_element_type=F32))
    qn = lax.dot_general(q, jnp.broadcast_to(n_old, (V7X_LANES, DK)).astype(BF16), NT_DIMS,
                         preferred_element_type=F32)[:, 0:1]
    den = w_inter * qn + jnp.sum(s, axis=-1, keepdims=True)
    hh = num / jnp.maximum(jnp.abs(den), jnp.exp(-m_t))

    m_new = m_t[L - 1:L, :]
    b_last = b_c[L - 1:L, :]
    w_state = jnp.exp(b_last - b_c + li_c - m_new)
    decay = jnp.exp(b_last + m_prev - m_new)
    kw = k.astype(F32) * w_state
    c_ref[...] = decay * c_old + lax.dot_general(kw.astype(BF16), v, TN_DIMS, preferred_element_type=F32)
    n_ref[...] = decay * n_old + jnp.sum(kw, axis=0, keepdims=True)
    m_ref[...] = jnp.broadcast_to(m_new, m_ref.shape)

    ms = jnp.mean(hh * hh, axis=-1, keepdims=True)
    y = hh * lax.rsqrt(ms + EPS) * hn_ref[...] * jax.nn.sigmoid(og_ref[...].astype(F32))
    o_ref[...] = y.astype(o_ref.dtype)


def mlstm_scan(proj, gates, gate_bias, head_norm, B, S, chunk=256):
    T = proj.shape[0]
    H = M_HEADS
    HDV = head_norm.shape[0]
    DV = HDV // H
    DK = (proj.shape[1] - 2 * HDV) // (2 * H)
    L = _tile(S, chunk)
    NC = S // L
    assert DV % DK == 0 or DK % DV == 0
    k_off = (H * DK) // DK
    v_off = (2 * H * DK) // DV
    og_off = (2 * H * DK + H * DV) // DV
    g2 = gates.reshape(T, 2, H)
    g_col = g2.transpose(2, 0, 1)
    g_row = g2.transpose(2, 1, 0)
    b_col = gate_bias.T.reshape(H, 1, 2)
    b_row = gate_bias.T.reshape(H, 2, 1)

    def tok(bh, c):
        return (bh // H) * NC + c

    return pl.pallas_call(
        functools.partial(_mlstm_body, L=L, DK=DK),
        grid=(B * H, NC),
        in_specs=[pl.BlockSpec((L, DK), lambda bh, c: (tok(bh, c), bh % H)),
                  pl.BlockSpec((L, DK), lambda bh, c: (tok(bh, c), k_off + bh % H)),
                  pl.BlockSpec((L, DV), lambda bh, c: (tok(bh, c), v_off + bh % H)),
                  pl.BlockSpec((L, DV), lambda bh, c: (tok(bh, c), og_off + bh % H)),
                  pl.BlockSpec((None, L, 2), lambda bh, c: (bh % H, tok(bh, c), 0)),
                  pl.BlockSpec((None, 2, L), lambda bh, c: (bh % H, 0, tok(bh, c))),
                  pl.BlockSpec((None, 1, 2), lambda bh, c: (bh % H, 0, 0)),
                  pl.BlockSpec((None, 2, 1), lambda bh, c: (bh % H, 0, 0)),
                  pl.BlockSpec((1, DV), lambda bh, c: (0, bh % H))],
        out_specs=pl.BlockSpec((L, DV), lambda bh, c: (tok(bh, c), bh % H)),
        out_shape=jax.ShapeDtypeStruct((T, HDV), BF16),
        scratch_shapes=[pltpu.VMEM((DK, DV), F32),
                        pltpu.VMEM((1, DK), F32),
                        pltpu.VMEM((V7X_SUBLANES, V7X_LANES), F32)],
        compiler_params=_params("parallel", "arbitrary"),
        name="mlstm_scan",
    )(proj, proj, proj, proj, g_col, g_row, b_col, b_row, head_norm.reshape(1, HDV))


def _swa_body(sink_ref, q_ref, kp_ref, kc_ref, o_ref, *, BLK, KVH, GROUP, HD):
    first = pl.program_id(1) == 0
    kv = jnp.concatenate([kp_ref[...], kc_ref[...]], axis=0).astype(F32)
    ri = lax.broadcasted_iota(jnp.int32, (BLK, 2 * BLK), 0)
    ci = lax.broadcasted_iota(jnp.int32, (BLK, 2 * BLK), 1)
    dist = ri + BLK - ci
    mask = (dist >= 0) & (dist < WINDOW) & ((ci >= BLK) | jnp.logical_not(first))
    lo = lax.broadcasted_iota(jnp.int32, (1, 2 * HD), 1) < HD
    scale = HD ** -0.5
    for pair in range(KVH // 2):
        xk = kv[:, pair * 2 * HD:(pair + 1) * 2 * HD]
        xv = kv[:, (KVH + pair * 2) * HD:(KVH + pair * 2 + 2) * HD]
        xk_r = pltpu.roll(xk, HD, 1)
        xv_r = pltpu.roll(xv, HD, 1)
        for sub in range(2):
            kh = 2 * pair + sub
            if sub == 0:
                k2 = jnp.where(lo, xk, xk_r)
                v_lo = jnp.where(lo, xv, 0.0)
                v_hi = jnp.where(lo, 0.0, xv_r)
            else:
                k2 = jnp.where(lo, xk_r, xk)
                v_lo = jnp.where(lo, xv_r, 0.0)
                v_hi = jnp.where(lo, 0.0, xv)
            k2 = k2.astype(BF16)
            v_halves = (v_lo.astype(BF16), v_hi.astype(BF16))
            for gp in range(GROUP // 2):
                h0 = kh * GROUP + 2 * gp
                q2 = q_ref[:, h0 * HD:(h0 + 2) * HD]
                acc = None
                for e in range(2):
                    qm = jnp.where(lo if e == 0 else jnp.logical_not(lo), q2, jnp.zeros_like(q2))
                    s = lax.dot_general(qm, k2, NT_DIMS, preferred_element_type=F32) * scale
                    s = jnp.where(mask, s, -jnp.inf)
                    sink = sink_ref[h0 + e]
                    mx = jnp.maximum(jnp.max(s, axis=-1, keepdims=True), sink)
                    pr = jnp.exp(s - mx)
                    den = jnp.sum(pr, axis=-1, keepdims=True) + jnp.exp(sink - mx)
                    o = jnp.dot(pr.astype(BF16), v_halves[e], preferred_element_type=F32) / den
                    acc = o if acc is None else acc + o
                o_ref[:, h0 * HD:(h0 + 2) * HD] = acc.astype(o_ref.dtype)


def swa_attention(q, kv, sinks, B, S):
    T, QD = q.shape
    HD, GROUP, BLK = A_HEAD_DIM, A_GROUP, WINDOW
    KVH = kv.shape[1] // (2 * HD)
    assert QD == KVH * GROUP * HD and 2 * HD == V7X_LANES and KVH % 2 == 0 and GROUP % 2 == 0
    NB = S // BLK
    return pl.pallas_call(
        functools.partial(_swa_body, BLK=BLK, KVH=KVH, GROUP=GROUP, HD=HD),
        grid=(B, NB),
        in_specs=[pl.BlockSpec(memory_space=pltpu.SMEM),
                  pl.BlockSpec((BLK, QD), lambda b, n: (b * NB + n, 0)),
                  pl.BlockSpec((BLK, 2 * KVH * HD), lambda b, n: (b * NB + jnp.maximum(n - 1, 0), 0)),
                  pl.BlockSpec((BLK, 2 * KVH * HD), lambda b, n: (b * NB + n, 0))],
        out_specs=pl.BlockSpec((BLK, QD), lambda b, n: (b * NB + n, 0)),
        out_shape=jax.ShapeDtypeStruct((T, QD), BF16),
        compiler_params=_params("parallel", "arbitrary"),
        name="swa_attention",
    )(sinks, q, kv, kv)


KEY_BIG = 1 << 30


def _topk_rows(problems, k):
    tm = problems[0][0].shape[1]
    slot = lax.broadcasted_iota(jnp.int32, (k, tm), 0)
    state = [[vals, keys, [], [], jnp.zeros((k, tm), vals.dtype), jnp.zeros((k, tm), keys.dtype)]
             for vals, keys in problems]
    for it in range(k):
        for st in state:
            vals, keys = st[0], st[1]
            m = jnp.max(vals, axis=0, keepdims=True)
            kmin = jnp.min(jnp.where(vals == m, keys, KEY_BIG), axis=0, keepdims=True)
            st[0] = jnp.where(keys == kmin, -jnp.inf, vals)
            st[2].append(m)
            st[3].append(kmin)
            st[4] = jnp.where(slot == it, m, st[4])
            st[5] = jnp.where(slot == it, kmin, st[5])
    return [tuple(st[2:]) for st in state]


def _candidates(top1, top2, k):
    r1_v, r1_i, v1, i1 = top1
    r2_v, r2_i, v2, i2 = top2
    tm = v1.shape[1]
    sub = V7X_SUBLANES
    vals, keys = [], []

    def pack(code, e1, e2):
        return (code << 14) | (e1 << 7) | e2

    b_iota = lax.broadcasted_iota(jnp.int32, (k, tm), 0)
    vals.append(r1_v[0] + v2)
    keys.append(pack(b_iota, r1_i[0], i2))
    a = 1
    while a < k and k // (a + 1) >= 2:
        nb = k // (a + 1)
        rows = -(-nb // sub) * sub
        bi = lax.broadcasted_iota(jnp.int32, (rows, tm), 0)
        vals.append(jnp.where(bi < nb, r1_v[a] + v2[0:rows], -jnp.inf))
        keys.append(pack(a * k + bi, r1_i[a], i2[0:rows]))
        a += 1
    if a < k:
        a0 = (a // sub) * sub
        ai = lax.broadcasted_iota(jnp.int32, (k - a0, tm), 0) + a0
        vals.append(jnp.where(ai >= a, v1[a0:k] + r2_v[0], -jnp.inf))
        keys.append(pack(ai * k, i1[a0:k], r2_i[0]))
    return jnp.concatenate(vals, axis=0), jnp.concatenate(keys, axis=0)


def _peer_qproj_body(x_ref, g_ref, w_ref, xn_ref, q_ref):
    @pl.when(pl.program_id(1) == 0)
    def _():
        xn_ref[...] = _rms(x_ref[...], g_ref[...]).astype(BF16)

    q_ref[...] = jnp.dot(xn_ref[...], w_ref[...], preferred_element_type=F32).astype(q_ref.dtype)


def peer_qproj(h, gain, wq, tm=1024, tn=1024):
    T, D = h.shape
    N = wq.shape[1]
    tm, tn = _tile(T, tm), _tile(N, tn)
    return pl.pallas_call(
        _peer_qproj_body,
        grid=(T // tm, N // tn),
        in_specs=[pl.BlockSpec((tm, D), lambda i, j: (i, 0)),
                  pl.BlockSpec((1, D), lambda i, j: (0, 0)),
                  pl.BlockSpec((D, tn), lambda i, j: (0, j))],
        out_specs=[pl.BlockSpec((tm, D), lambda i, j: (i, 0)),
                   pl.BlockSpec((tm, tn), lambda i, j: (i, j))],
        out_shape=[jax.ShapeDtypeStruct((T, D), BF16), jax.ShapeDtypeStruct((T, N), BF16)],
        compiler_params=_params("parallel", "arbitrary"),
        name="peer_qproj",
    )(h, gain.reshape(1, D), wq)


TOPK_TOKENS = 256


def _key_scores(q_ref, k1, k2, s1_ref, s2_ref):
    half = k1.shape[1]
    s1_ref[...] = lax.dot_general(k1, q_ref[:, 0:half], NT_DIMS, preferred_element_type=F32)
    s2_ref[...] = lax.dot_general(k2, q_ref[:, half:2 * half], NT_DIMS, preferred_element_type=F32)


def _select_from_scores(s1_ref, s2_ref, gate_ref, e1_ref, e2_ref, *, K):
    NK = s1_ref.shape[0]
    tl = V7X_LANES
    key_iota = lax.broadcasted_iota(jnp.int32, (NK, tl), 0)
    for c in range(s1_ref.shape[1] // tl):
        lanes = slice(c * tl, (c + 1) * tl)
        top1, top2 = _topk_rows([(s1_ref[:, lanes], key_iota), (s2_ref[:, lanes], key_iota)], K)
        rows_s, _, top_s, top_key = _topk_rows([_candidates(top1, top2, K)], K)[0]
        ex = jnp.exp(top_s - rows_s[0])
        gate_ref[:, lanes] = ex / jnp.sum(ex, axis=0, keepdims=True)
        e1_ref[:, lanes] = ((top_key >> 7) & 127).astype(F32)
        e2_ref[:, lanes] = (top_key & 127).astype(F32)


def _peer_topk_body(q_ref, k1_ref, k2_ref, gate_ref, e1_ref, e2_ref, s1_ref, s2_ref, *, K):
    hd = pl.program_id(1)
    _key_scores(q_ref, k1_ref[hd], k2_ref[hd], s1_ref, s2_ref)
    _select_from_scores(s1_ref, s2_ref, gate_ref, e1_ref, e2_ref, K=K)


def peer_topk(q, k1, k2):
    T = q.shape[0]
    PH, NK, HALF = k1.shape
    K = P_TOPK
    assert NK == V7X_LANES and HALF == V7X_LANES and q.shape[1] == PH * 2 * HALF
    tq = _tile(T, TOPK_TOKENS)
    k_spec = pl.BlockSpec((PH, NK, HALF), lambda c, hd: (0, 0, 0))
    sel_spec = pl.BlockSpec((K, tq), lambda c, hd: (hd, c))
    sel = jax.ShapeDtypeStruct((PH * K, T), F32)
    return pl.pallas_call(
        functools.partial(_peer_topk_body, K=K),
        grid=(T // tq, PH),
        in_specs=[pl.BlockSpec((tq, 2 * HALF), lambda c, hd: (c, hd)), k_spec, k_spec],
        out_specs=[sel_spec] * 3, out_shape=[sel] * 3,
        scratch_shapes=[pltpu.VMEM((NK, tq), F32)] * 2,
        compiler_params=_params("parallel", "arbitrary"),
        name="peer_topk",
    )(q, k1, k2)


SCATTER_UNROLL = 16
SCATTER_PITCH_PAD = V7X_SUBLANES


def _peer_scatter_body(gate_ref, e1_ref, e2_ref, w_ref, scr_ref, gt_ref, e1t_ref, e2t_ref, *, NK):
    tmb = gt_ref.shape[0]
    pitch = NK + SCATTER_PITCH_PAD
    gt_ref[...] = gate_ref[...].T
    e1t_ref[...] = e1_ref[...].T
    e2t_ref[...] = e2_ref[...].T
    key_iota = lax.broadcasted_iota(jnp.int32, (NK, gt_ref.shape[1]), 0).astype(F32)

    def token_group(grp, carry):
        for u in range(SCATTER_UNROLL):
            t = grp * SCATTER_UNROLL + u
            g_row = gt_ref[pl.ds(t, 1), :]
            sel1 = jnp.where(e1t_ref[pl.ds(t, 1), :] == key_iota, g_row, 0.0).astype(BF16)
            sel2 = jnp.where(e2t_ref[pl.ds(t, 1), :] == key_iota, 1.0, 0.0).astype(BF16)
            w_t = lax.dot_general(sel1, sel2, NT_DIMS, preferred_element_type=F32)
            scr_ref[pl.ds(pl.multiple_of(t * pitch, V7X_SUBLANES), NK), :] = w_t
        return carry

    lax.fori_loop(0, tmb // SCATTER_UNROLL, token_group, 0)
    for j in range(NK):
        w_ref[j] = scr_ref[pl.ds(j, tmb, stride=pitch), :].astype(w_ref.dtype)


def peer_scatter(gate, e1, e2, NK, tmb=128):
    HK, T = gate.shape
    tmb = _tile(T, tmb)
    spec = pl.BlockSpec((HK, tmb), lambda i: (0, i))
    return pl.pallas_call(
        functools.partial(_peer_scatter_body, NK=NK),
        grid=(T // tmb,),
        in_specs=[spec, spec, spec],
        out_specs=pl.BlockSpec((NK, tmb, NK), lambda i: (0, i, 0)),
        out_shape=jax.ShapeDtypeStruct((NK, T, NK), BF16),
        scratch_shapes=[pltpu.VMEM((tmb * (NK + SCATTER_PITCH_PAD), NK), F32),
                        pltpu.VMEM((tmb, HK), F32),
                        pltpu.VMEM((tmb, HK), F32),
                        pltpu.VMEM((tmb, HK), F32)],
        compiler_params=_params("parallel"),
        name="peer_scatter",
    )(gate, e1, e2)


def _gelu(x):
    return 0.5 * x * (1.0 + lax.erf(x * (2.0 ** -0.5)))


def _peer_dense_body(xn_ref, u_ref, v_ref, w_ref, o_ref, *, NK):
    @pl.when(pl.program_id(1) == 0)
    def _():
        o_ref[...] = jnp.zeros_like(o_ref)

    act = lax.dot_general(xn_ref[...], u_ref[...], NT_DIMS, preferred_element_type=F32)
    coef = jnp.concatenate(
        [(w_ref[c].astype(F32) * _gelu(act[:, c * NK:(c + 1) * NK])).astype(BF16)
         for c in range(w_ref.shape[0])], axis=1)
    o_ref[...] += jnp.dot(coef, v_ref[...], preferred_element_type=F32)


def peer_dense(xn, u_all, v_all, layer, w, tm=1024, te=512):
    T, D = xn.shape
    E = u_all.shape[1]
    NK = w.shape[0]
    tm, te = _tile(T, tm), _tile(E, te)
    assert te % NK == 0 and E == NK * NK
    return pl.pallas_call(
        functools.partial(_peer_dense_body, NK=NK),
        grid=(T // tm, E // te),
        in_specs=[pl.BlockSpec((tm, D), lambda i, j: (i, 0)),
                  pl.BlockSpec((None, te, D), lambda i, j: (layer, j, 0)),
                  pl.BlockSpec((None, te, D), lambda i, j: (layer, j, 0)),
                  pl.BlockSpec((te // NK, tm, NK), lambda i, j: (j, i, 0))],
        out_specs=pl.BlockSpec((tm, D), lambda i, j: (i, 0)),
        out_shape=jax.ShapeDtypeStruct((T, D), F32),
        compiler_params=_params("parallel", "arbitrary"),
        name="peer_dense",
    )(xn, u_all, v_all, w)


def peer_layer(h, gain, wq, k1, k2, u_all, v_all, layer):
    xn, q = peer_qproj(h, gain, wq)
    gate, e1, e2 = peer_topk(q, k1, k2)
    w = peer_scatter(gate, e1, e2, k1.shape[1])
    return peer_dense(xn, u_all, v_all, layer, w)


def kernel(x, p, a_norm, a_w_in, a_gate_bias, a_head_norm, a_w_out, kv_norm, w_kv, b_norm, b_w_q, b_sinks,
           b_w_out, c_norm, peer_w_q, peer_k1, peer_k2, peer_u, peer_v, ple_norm, ple_w_gate, ple_w_proj,
           final_norm):
    B, S, D = x.shape
    T = B * S
    depth = p.shape[0]
    n_a = a_norm.shape[0]
    bf = lambda t: t.astype(BF16)
    h = x.reshape(T, D)
    u_all, v_all = bf(peer_u), bf(peer_v)
    for i in range(depth):
        if i < n_a:
            w_in = a_w_in[i]
            n_main = w_in.shape[1] - 2 * M_HEADS
            w_gates = jnp.pad(w_in[:, n_main:], ((0, 0), (0, V7X_LANES - 2 * M_HEADS)))
            proj = rms_matmul(h, a_norm[i], bf(w_in[:, :n_main]), BF16, tm=1024)
            gates = rms_matmul(h, a_norm[i], bf(w_gates), F32)[:, :2 * M_HEADS]
            y = mlstm_scan(proj, gates, a_gate_bias[i], a_head_norm[i], B, S)
            h = matmul_residual(y, bf(a_w_out[i]), h)
        else:
            j = i - n_a
            if j == 0:
                kv = rms_matmul(h, kv_norm, bf(w_kv), BF16)
            q = rms_matmul(h, b_norm[j], bf(b_w_q[j]), BF16, tm=1024, tn=1024)
            o = swa_attention(q, kv, b_sinks[j], B, S)
            h = matmul_residual(o, bf(b_w_out[j]), h)
        peer_out = peer_layer(h, c_norm[i], bf(peer_w_q[i]), bf(peer_k1[i]), bf(peer_k2[i]), u_all, v_all, i)
        h = ple_gate(h, peer_out, ple_norm[i], bf(ple_w_gate[i]), p[i].reshape(T, -1), bf(ple_w_proj[i]),
                     final_norm, final=(i == depth - 1))
    return h.reshape(B, S, D)
```

```python
import functools

import jax
import jax.numpy as jnp
from jax import lax
from jax.experimental import pallas as pl
from jax.experimental.pallas import tpu as pltpu

F32 = jnp.float32
BF16 = jnp.bfloat16

EPS = 1e-6
GATE_CAP = 15.0

M_HEADS = 4
A_HEAD_DIM = 64
A_GROUP = 8
WINDOW = 128
P_HEADS = 8
P_NKEYS = 128
P_TOPK = 16

V7X_LANES = 128
V7X_SUBLANES = 8
V7X_VMEM_BYTES = 64 * 1024 * 1024
VMEM_LIMIT = (V7X_VMEM_BYTES * 3) // 4

NT_DIMS = (((1,), (1,)), ((), ()))
TN_DIMS = (((0,), (0,)), ((), ()))


def _params(*sem):
    return pltpu.CompilerParams(dimension_semantics=sem, vmem_limit_bytes=VMEM_LIMIT)


def _rms(x, gain):
    ms = jnp.mean(x * x, axis=-1, keepdims=True)
    return x * lax.rsqrt(ms + EPS) * gain


def _tile(n, pref):
    t = min(n, pref)
    assert n % t == 0, (n, pref)
    return t


def _rms_matmul_body(x_ref, g_ref, w_ref, o_ref, xn_ref):
    @pl.when(pl.program_id(1) == 0)
    def _():
        xn_ref[...] = _rms(x_ref[...], g_ref[...]).astype(BF16)

    o_ref[...] = jnp.dot(xn_ref[...], w_ref[...], preferred_element_type=F32).astype(o_ref.dtype)


def rms_matmul(x, gain, w, out_dtype, tm=512, tn=512):
    T, D = x.shape
    N = w.shape[1]
    tm, tn = _tile(T, tm), _tile(N, tn)
    return pl.pallas_call(
        _rms_matmul_body,
        grid=(T // tm, N // tn),
        in_specs=[pl.BlockSpec((tm, D), lambda i, j: (i, 0)),
                  pl.BlockSpec((1, D), lambda i, j: (0, 0)),
                  pl.BlockSpec((D, tn), lambda i, j: (0, j))],
        out_specs=pl.BlockSpec((tm, tn), lambda i, j: (i, j)),
        out_shape=jax.ShapeDtypeStruct((T, N), out_dtype),
        scratch_shapes=[pltpu.VMEM((tm, D), BF16)],
        compiler_params=_params("parallel", "arbitrary"),
        name="rms_matmul",
    )(x, gain.reshape(1, D), w)


def _matmul_residual_body(y_ref, w_ref, h_ref, o_ref):
    o_ref[...] = h_ref[...] + jnp.dot(y_ref[...], w_ref[...], preferred_element_type=F32)


def matmul_residual(y, w, h, tm=1024, tn=1024):
    T, K = y.shape
    N = w.shape[1]
    tm, tn = _tile(T, tm), _tile(N, tn)
    return pl.pallas_call(
        _matmul_residual_body,
        grid=(T // tm, N // tn),
        in_specs=[pl.BlockSpec((tm, K), lambda i, j: (i, 0)),
                  pl.BlockSpec((K, tn), lambda i, j: (0, j)),
                  pl.BlockSpec((tm, tn), lambda i, j: (i, j))],
        out_specs=pl.BlockSpec((tm, tn), lambda i, j: (i, j)),
        out_shape=jax.ShapeDtypeStruct((T, N), F32),
        compiler_params=_params("parallel", "arbitrary"),
        name="matmul_residual",
    )(y, w, h)


def _ple_body(h_ref, d_ref, g_ref, wg_ref, p_ref, wp_ref, fn_ref, o_ref, *, final):
    h = h_ref[...] + d_ref[...]
    xn = _rms(h, g_ref[...]).astype(BF16)
    gate = jax.nn.sigmoid(jnp.dot(xn, wg_ref[...], preferred_element_type=F32))
    emb = jnp.dot(p_ref[...].astype(BF16), wp_ref[...], preferred_element_type=F32)
    out = h + gate * emb
    if final:
        out = _rms(out, fn_ref[...])
    o_ref[...] = out


def ple_gate(h, peer_out, gain, wg, p, wp, final_gain, final, tm=256):
    T, D = h.shape
    PD = p.shape[1]
    tm = _tile(T, tm)
    return pl.pallas_call(
        functools.partial(_ple_body, final=final),
        grid=(T // tm,),
        in_specs=[pl.BlockSpec((tm, D), lambda i: (i, 0)),
                  pl.BlockSpec((tm, D), lambda i: (i, 0)),
                  pl.BlockSpec((1, D), lambda i: (0, 0)),
                  pl.BlockSpec((D, D), lambda i: (0, 0)),
                  pl.BlockSpec((tm, PD), lambda i: (i, 0)),
                  pl.BlockSpec((PD, D), lambda i: (0, 0)),
                  pl.BlockSpec((1, D), lambda i: (0, 0))],
        out_specs=pl.BlockSpec((tm, D), lambda i: (i, 0)),
        out_shape=jax.ShapeDtypeStruct((T, D), F32),
        compiler_params=_params("parallel"),
        name="ple_gate",
    )(h, peer_out, gain.reshape(1, D), wg, p, wp, final_gain.reshape(1, D))


def _soft_cap(t):
    return GATE_CAP * jnp.tanh(t / GATE_CAP)


def _log_sigmoid(t):
    return jnp.minimum(t, 0.0) - jnp.log(1.0 + jnp.exp(-jnp.abs(t)))


def _split3(x):
    hi = x.astype(BF16)
    r = x - hi.astype(F32)
    mid = r.astype(BF16)
    lo = (r - mid.astype(F32)).astype(BF16)
    return hi, mid, lo


def _mlstm_body(q_ref, k_ref, v_ref, og_ref, gc_ref, gr_ref, bc_ref, br_ref, hn_ref, o_ref,
                c_ref, n_ref, m_ref, *, L, DK):
    @pl.when(pl.program_id(1) == 0)
    def _():
        c_ref[...] = jnp.zeros_like(c_ref)
        n_ref[...] = jnp.zeros_like(n_ref)
        m_ref[...] = jnp.zeros_like(m_ref)

    q = (q_ref[...].astype(F32) * (DK ** -0.5)).astype(BF16)
    k = k_ref[...]
    v = v_ref[...]

    gc = gc_ref[...] + bc_ref[...]
    gr = gr_ref[...] + br_ref[...]
    li_c = _soft_cap(gc[:, 0:1])
    lf_c = _log_sigmoid(_soft_cap(gc[:, 1:2]))
    li_r = _soft_cap(gr[0:1, :])
    lf_r = _log_sigmoid(_soft_cap(gr[1:2, :]))

    row = lax.broadcasted_iota(jnp.int32, (L, L), 0)
    col = lax.broadcasted_iota(jnp.int32, (L, L), 1)
    causal = row >= col
    tri_l = causal.astype(BF16)
    tri_u = (row <= col).astype(BF16)
    b_c = sum(jnp.dot(tri_l, part, preferred_element_type=F32)
              for part in _split3(jnp.broadcast_to(lf_c, (L, V7X_LANES))))[:, 0:1]
    b_r = sum(jnp.dot(part, tri_u, preferred_element_type=F32)
              for part in _split3(jnp.broadcast_to(lf_r, (V7X_SUBLANES, L))))[0:1, :]

    m_prev = m_ref[0:1, 0:1]
    dmat = jnp.where(causal, b_c - b_r + li_r, -jnp.inf)
    inter = b_c + m_prev
    m_t = jnp.maximum(inter, jnp.max(dmat, axis=-1, keepdims=True))
    w_intra = jnp.exp(dmat - m_t)
    w_inter = jnp.exp(inter - m_t)

    s = lax.dot_general(q, k, NT_DIMS, preferred_element_type=F32) * w_intra
    c_old = c_ref[...]
    n_old = n_ref[...]
    num = (w_inter * jnp.dot(q, c_old.astype(BF16), preferred_element_type=F32)
           + jnp.dot(s.astype(BF16), v, preferred_element_type=F32))
    qn = lax.dot_general(q, jnp.broadcast_to(n_old, (V7X_LANES, DK)).astype(BF16), NT_DIMS,
                         preferred_element_type=F32)[:, 0:1]
    den = w_inter * qn + jnp.sum(s, axis=-1, keepdims=True)
    hh = num / jnp.maximum(jnp.abs(den), jnp.exp(-m_t))

    m_new = m_t[L - 1:L, :]
    b_last = b_c[L - 1:L, :]
    w_state = jnp.exp(b_last - b_c + li_c - m_new)
    decay = jnp.exp(b_last + m_prev - m_new)
    kw = k.astype(F32) * w_state
    c_ref[...] = decay * c_old + lax.dot_general(kw.astype(BF16), v, TN_DIMS, preferred_element_type=F32)
    n_ref[...] = decay * n_old + jnp.sum(kw, axis=0, keepdims=True)
    m_ref[...] = jnp.broadcast_to(m_new, m_ref.shape)

    ms = jnp.mean(hh * hh, axis=-1, keepdims=True)
    y = hh * lax.rsqrt(ms + EPS) * hn_ref[...] * jax.nn.sigmoid(og_ref[...].astype(F32))
    o_ref[...] = y.astype(o_ref.dtype)


def mlstm_scan(proj, gates, gate_bias, head_norm, B, S, chunk=256):
    T = proj.shape[0]
    H = M_HEADS
    HDV = head_norm.shape[0]
    DV = HDV // H
    DK = (proj.shape[1] - 2 * HDV) // (2 * H)
    L = _tile(S, chunk)
    NC = S // L
    assert DV % DK == 0 or DK % DV == 0
    k_off = (H * DK) // DK
    v_off = (2 * H * DK) // DV
    og_off = (2 * H * DK + H * DV) // DV
    g2 = gates.reshape(T, 2, H)
    g_col = g2.transpose(2, 0, 1)
    g_row = g2.transpose(2, 1, 0)
    b_col = gate_bias.T.reshape(H, 1, 2)
    b_row = gate_bias.T.reshape(H, 2, 1)

    def tok(bh, c):
        return (bh // H) * NC + c

    return pl.pallas_call(
        functools.partial(_mlstm_body, L=L, DK=DK),
        grid=(B * H, NC),
        in_specs=[pl.BlockSpec((L, DK), lambda bh, c: (tok(bh, c), bh % H)),
                  pl.BlockSpec((L, DK), lambda bh, c: (tok(bh, c), k_off + bh % H)),
                  pl.BlockSpec((L, DV), lambda bh, c: (tok(bh, c), v_off + bh % H)),
                  pl.BlockSpec((L, DV), lambda bh, c: (tok(bh, c), og_off + bh % H)),
                  pl.BlockSpec((None, L, 2), lambda bh, c: (bh % H, tok(bh, c), 0)),
                  pl.BlockSpec((None, 2, L), lambda bh, c: (bh % H, 0, tok(bh, c))),
                  pl.BlockSpec((None, 1, 2), lambda bh, c: (bh % H, 0, 0)),
                  pl.BlockSpec((None, 2, 1), lambda bh, c: (bh % H, 0, 0)),
                  pl.BlockSpec((1, DV), lambda bh, c: (0, bh % H))],
        out_specs=pl.BlockSpec((L, DV), lambda bh, c: (tok(bh, c), bh % H)),
        out_shape=jax.ShapeDtypeStruct((T, HDV), BF16),
        scratch_shapes=[pltpu.VMEM((DK, DV), F32),
                        pltpu.VMEM((1, DK), F32),
                        pltpu.VMEM((V7X_SUBLANES, V7X_LANES), F32)],
        compiler_params=_params("parallel", "arbitrary"),
        name="mlstm_scan",
    )(proj, proj, proj, proj, g_col, g_row, b_col, b_row, head_norm.reshape(1, HDV))


def _swa_body(sink_ref, q_ref, kp_ref, kc_ref, o_ref, *, BLK, KVH, GROUP, HD):
    first = pl.program_id(1) == 0
    kv = jnp.concatenate([kp_ref[...], kc_ref[...]], axis=0).astype(F32)
    ri = lax.broadcasted_iota(jnp.int32, (BLK, 2 * BLK), 0)
    ci = lax.broadcasted_iota(jnp.int32, (BLK, 2 * BLK), 1)
    dist = ri + BLK - ci
    mask = (dist >= 0) & (dist < WINDOW) & ((ci >= BLK) | jnp.logical_not(first))
    lo = lax.broadcasted_iota(jnp.int32, (1, 2 * HD), 1) < HD
    scale = HD ** -0.5
    for pair in range(KVH // 2):
        xk = kv[:, pair * 2 * HD:(pair + 1) * 2 * HD]
        xv = kv[:, (KVH + pair * 2) * HD:(KVH + pair * 2 + 2) * HD]
        xk_r = pltpu.roll(xk, HD, 1)
        xv_r = pltpu.roll(xv, HD, 1)
        for sub in range(2):
            kh = 2 * pair + sub
            if sub == 0:
                k2 = jnp.where(lo, xk, xk_r)
                v_lo = jnp.where(lo, xv, 0.0)
                v_hi = jnp.where(lo, 0.0, xv_r)
            else:
                k2 = jnp.where(lo, xk_r, xk)
                v_lo = jnp.where(lo, xv_r, 0.0)
                v_hi = jnp.where(lo, 0.0, xv)
            k2 = k2.astype(BF16)
            v_halves = (v_lo.astype(BF16), v_hi.astype(BF16))
            for gp in range(GROUP // 2):
                h0 = kh * GROUP + 2 * gp
                q2 = q_ref[:, h0 * HD:(h0 + 2) * HD]
                acc = None
                for e in range(2):
                    qm = jnp.where(lo if e == 0 else jnp.logical_not(lo), q2, jnp.zeros_like(q2))
                    s = lax.dot_general(qm, k2, NT_DIMS, preferred_element_type=F32) * scale
                    s = jnp.where(mask, s, -jnp.inf)
                    sink = sink_ref[h0 + e]
                    mx = jnp.maximum(jnp.max(s, axis=-1, keepdims=True), sink)
                    pr = jnp.exp(s - mx)
                    den = jnp.sum(pr, axis=-1, keepdims=True) + jnp.exp(sink - mx)
                    o = jnp.dot(pr.astype(BF16), v_halves[e], preferred_element_type=F32) / den
                    acc = o if acc is None else acc + o
                o_ref[:, h0 * HD:(h0 + 2) * HD] = acc.astype(o_ref.dtype)


def swa_attention(q, kv, sinks, B, S):
    T, QD = q.shape
    HD, GROUP, BLK = A_HEAD_DIM, A_GROUP, WINDOW
    KVH = kv.shape[1] // (2 * HD)
    assert QD == KVH * GROUP * HD and 2 * HD == V7X_LANES and KVH % 2 == 0 and GROUP % 2 == 0
    NB = S // BLK
    return pl.pallas_call(
        functools.partial(_swa_body, BLK=BLK, KVH=KVH, GROUP=GROUP, HD=HD),
        grid=(B, NB),
        in_specs=[pl.BlockSpec(memory_space=pltpu.SMEM),
                  pl.BlockSpec((BLK, QD), lambda b, n: (b * NB + n, 0)),
                  pl.BlockSpec((BLK, 2 * KVH * HD), lambda b, n: (b * NB + jnp.maximum(n - 1, 0), 0)),
                  pl.BlockSpec((BLK, 2 * KVH * HD), lambda b, n: (b * NB + n, 0))],
        out_specs=pl.BlockSpec((BLK, QD), lambda b, n: (b * NB + n, 0)),
        out_shape=jax.ShapeDtypeStruct((T, QD), BF16),
        compiler_params=_params("parallel", "arbitrary"),
        name="swa_attention",
    )(sinks, q, kv, kv)


KEY_BIG = 1 << 30


def _topk_rows(problems, k):
    tm = problems[0][0].shape[1]
    slot = lax.broadcasted_iota(jnp.int32, (k, tm), 0)
    state = [[vals, keys, [], [], jnp.zeros((k, tm), vals.dtype), jnp.zeros((k, tm), keys.dtype)]
             for vals, keys in problems]
    for it in range(k):
        for st in state:
            vals, keys = st[0], st[1]
            m = jnp.max(vals, axis=0, keepdims=True)
            kmin = jnp.min(jnp.where(vals == m, keys, KEY_BIG), axis=0, keepdims=True)
            st[0] = jnp.where(keys == kmin, -jnp.inf, vals)
            st[2].append(m)
            st[3].append(kmin)
            st[4] = jnp.where(slot == it, m, st[4])
            st[5] = jnp.where(slot == it, kmin, st[5])
    return [tuple(st[2:]) for st in state]


def _oddeven_merge_sort(n):
    pairs = []
    p = 1
    while p < n:
        k = p
        while k >= 1:
            for j in range(k % p, n - k, 2 * k):
                for i in range(min(k, n - j - k)):
                    if (i + j) // (2 * p) == (i + j + k) // (2 * p):
                        pairs.append((i + j, i + j + k))
            k //= 2
        p *= 2
    return pairs


def _topk_rows_distinct(problems, k):
    sub = V7X_SUBLANES
    tl = problems[0][0].shape[1]
    slot = lax.broadcasted_iota(jnp.int32, (k, tl), 0)
    state = []
    for vals, keys in problems:
        n = vals.shape[0] // sub
        v = [vals[j * sub:(j + 1) * sub] for j in range(n)]
        q = [keys[j * sub:(j + 1) * sub] for j in range(n)]
        size = 1
        while size < n:
            size *= 2
        for i, j in _oddeven_merge_sort(size):
            if j < n:
                first = v[i] >= v[j]
                v[i], v[j] = jnp.maximum(v[i], v[j]), jnp.minimum(v[i], v[j])
                q[i], q[j] = jnp.where(first, q[i], q[j]), jnp.where(first, q[j], q[i])
        state.append(dict(v=v, q=q, n=n, rows_v=[], rows_k=[], arr_v=jnp.zeros((k, tl), vals.dtype),
                          arr_k=jnp.zeros((k, tl), keys.dtype), tie=jnp.zeros((1, tl), jnp.bool_)))
    for it in range(k):
        for st in state:
            v, q, n = st["v"], st["q"], st["n"]
            m = jnp.max(v[0], axis=0, keepdims=True)
            pop = v[0] == m
            key = jnp.max(jnp.where(pop, q[0], -1), axis=0, keepdims=True)
            hits = jnp.sum(pop.astype(F32), axis=0, keepdims=True)
            st["tie"] = st["tie"] | (hits != 1.0)
            if st["rows_v"]:
                st["tie"] = st["tie"] | (m == st["rows_v"][-1])
            depth = min(n, k - it)
            for j in range(depth):
                v[j] = jnp.where(pop, v[j + 1] if j + 1 < n else -jnp.inf, v[j])
                if j + 1 < n:
                    q[j] = jnp.where(pop, q[j + 1], q[j])
            st["rows_v"].append(m)
            st["rows_k"].append(key)
            st["arr_v"] = jnp.where(slot == it, m, st["arr_v"])
            st["arr_k"] = jnp.where(slot == it, key, st["arr_k"])
    out = []
    for st in state:
        tie = st["tie"] | (jnp.max(st["v"][0], axis=0, keepdims=True) == st["rows_v"][-1])
        out.append(((st["rows_v"], st["rows_k"], st["arr_v"], st["arr_k"]), tie))
    return out


def _candidates(top1, top2, k):
    r1_v, r1_i, v1, i1 = top1
    r2_v, r2_i, v2, i2 = top2
    tm = v1.shape[1]
    sub = V7X_SUBLANES
    vals, keys = [], []

    def pack(code, e1, e2):
        return (code << 14) | (e1 << 7) | e2

    b_iota = lax.broadcasted_iota(jnp.int32, (k, tm), 0)
    vals.append(r1_v[0] + v2)
    keys.append(pack(b_iota, r1_i[0], i2))
    a = 1
    while a < k and k // (a + 1) >= 2:
        nb = k // (a + 1)
        rows = -(-nb // sub) * sub
        bi = lax.broadcasted_iota(jnp.int32, (rows, tm), 0)
        vals.append(jnp.where(bi < nb, r1_v[a] + v2[0:rows], -jnp.inf))
        keys.append(pack(a * k + bi, r1_i[a], i2[0:rows]))
        a += 1
    if a < k:
        a0 = (a // sub) * sub
        ai = lax.broadcasted_iota(jnp.int32, (k - a0, tm), 0) + a0
        vals.append(jnp.where(ai >= a, v1[a0:k] + r2_v[0], -jnp.inf))
        keys.append(pack(ai * k, i1[a0:k], r2_i[0]))
    return jnp.concatenate(vals, axis=0), jnp.concatenate(keys, axis=0)


def _peer_qproj_body(x_ref, g_ref, w_ref, xn_ref, q_ref):
    @pl.when(pl.program_id(1) == 0)
    def _():
        xn_ref[...] = _rms(x_ref[...], g_ref[...]).astype(BF16)

    q_ref[...] = jnp.dot(xn_ref[...], w_ref[...], preferred_element_type=F32).astype(q_ref.dtype)


def peer_qproj(h, gain, wq, tm=1024, tn=1024):
    T, D = h.shape
    N = wq.shape[1]
    tm, tn = _tile(T, tm), _tile(N, tn)
    return pl.pallas_call(
        _peer_qproj_body,
        grid=(T // tm, N // tn),
        in_specs=[pl.BlockSpec((tm, D), lambda i, j: (i, 0)),
                  pl.BlockSpec((1, D), lambda i, j: (0, 0)),
                  pl.BlockSpec((D, tn), lambda i, j: (0, j))],
        out_specs=[pl.BlockSpec((tm, D), lambda i, j: (i, 0)),
                   pl.BlockSpec((tm, tn), lambda i, j: (i, j))],
        out_shape=[jax.ShapeDtypeStruct((T, D), BF16), jax.ShapeDtypeStruct((T, N), BF16)],
        compiler_params=_params("parallel", "arbitrary"),
        name="peer_qproj",
    )(h, gain.reshape(1, D), wq)


TOPK_TOKENS = 256


def _key_scores(q_ref, k1, k2, s1_ref, s2_ref):
    half = k1.shape[1]
    s1_ref[...] = lax.dot_general(k1, q_ref[:, 0:half], NT_DIMS, preferred_element_type=F32)
    s2_ref[...] = lax.dot_general(k2, q_ref[:, half:2 * half], NT_DIMS, preferred_element_type=F32)


def _select_from_scores(s1_ref, s2_ref, gate_ref, e1_ref, e2_ref, *, K):
    NK = s1_ref.shape[0]
    tl = V7X_LANES
    key_iota = lax.broadcasted_iota(jnp.int32, (NK, tl), 0)
    groups = [slice(c * tl, (c + 1) * tl) for c in range(s1_ref.shape[1] // tl)]

    def emit(lanes, top):
        rows_s, _, top_s, top_key = top
        ex = jnp.exp(top_s - rows_s[0])
        gate_ref[:, lanes] = ex / jnp.sum(ex, axis=0, keepdims=True)
        e1_ref[:, lanes] = ((top_key >> 7) & 127).astype(F32)
        e2_ref[:, lanes] = (top_key & 127).astype(F32)

    stage1 = _topk_rows_distinct(
        [(ref[:, lanes], key_iota) for lanes in groups for ref in (s1_ref, s2_ref)], K)
    stage2 = _topk_rows_distinct(
        [_candidates(stage1[2 * c][0], stage1[2 * c + 1][0], K) for c in range(len(groups))], K)
    for c, lanes in enumerate(groups):
        emit(lanes, stage2[c][0])

    for c, lanes in enumerate(groups):
        any_tie = jnp.max((stage1[2 * c][1] | stage1[2 * c + 1][1] | stage2[c][1]).astype(jnp.int32))

        @pl.when(any_tie > 0)
        def _():
            top1, top2 = _topk_rows([(s1_ref[:, lanes], key_iota), (s2_ref[:, lanes], key_iota)], K)
            emit(lanes, _topk_rows([_candidates(top1, top2, K)], K)[0])


def _peer_topk_body(q_ref, k1_ref, k2_ref, gate_ref, e1_ref, e2_ref, s1_ref, s2_ref, *, K):
    hd = pl.program_id(1)
    _key_scores(q_ref, k1_ref[hd], k2_ref[hd], s1_ref, s2_ref)
    _select_from_scores(s1_ref, s2_ref, gate_ref, e1_ref, e2_ref, K=K)


def peer_topk(q, k1, k2):
    T = q.shape[0]
    PH, NK, HALF = k1.shape
    K = P_TOPK
    assert NK == V7X_LANES and HALF == V7X_LANES and q.shape[1] == PH * 2 * HALF
    tq = _tile(T, TOPK_TOKENS)
    k_spec = pl.BlockSpec((PH, NK, HALF), lambda c, hd: (0, 0, 0))
    sel_spec = pl.BlockSpec((K, tq), lambda c, hd: (hd, c))
    sel = jax.ShapeDtypeStruct((PH * K, T), F32)
    return pl.pallas_call(
        functools.partial(_peer_topk_body, K=K),
        grid=(T // tq, PH),
        in_specs=[pl.BlockSpec((tq, 2 * HALF), lambda c, hd: (c, hd)), k_spec, k_spec],
        out_specs=[sel_spec] * 3, out_shape=[sel] * 3,
        scratch_shapes=[pltpu.VMEM((NK, tq), F32)] * 2,
        compiler_params=_params("parallel", "arbitrary"),
        name="peer_topk",
    )(q, k1, k2)


SCATTER_UNROLL = 64
SCATTER_PITCH_PAD = V7X_SUBLANES


def _peer_scatter_body(gate_ref, e1_ref, e2_ref, w_ref, scr_ref, gt_ref, e1t_ref, e2t_ref, *, NK):
    tmb = gt_ref.shape[0]
    pitch = NK + SCATTER_PITCH_PAD
    gt_ref[...] = gate_ref[...].T
    e1t_ref[...] = e1_ref[...].T
    e2t_ref[...] = e2_ref[...].T
    key_iota = lax.broadcasted_iota(jnp.int32, (NK, gt_ref.shape[1]), 0).astype(F32)

    def token_group(grp, carry):
        for u in range(SCATTER_UNROLL):
            t = grp * SCATTER_UNROLL + u
            g_row = gt_ref[pl.ds(t, 1), :]
            sel1 = jnp.where(e1t_ref[pl.ds(t, 1), :] == key_iota, g_row, 0.0).astype(BF16)
            sel2 = jnp.where(e2t_ref[pl.ds(t, 1), :] == key_iota, 1.0, 0.0).astype(BF16)
            w_t = lax.dot_general(sel1, sel2, NT_DIMS, preferred_element_type=F32)
            scr_ref[pl.ds(pl.multiple_of(t * pitch, V7X_SUBLANES), NK), :] = w_t
        return carry

    lax.fori_loop(0, tmb // SCATTER_UNROLL, token_group, 0)
    for j in range(NK):
        w_ref[j] = scr_ref[pl.ds(j, tmb, stride=pitch), :].astype(w_ref.dtype)


def peer_scatter(gate, e1, e2, NK, tmb=128):
    HK, T = gate.shape
    tmb = _tile(T, tmb)
    spec = pl.BlockSpec((HK, tmb), lambda i: (0, i))
    return pl.pallas_call(
        functools.partial(_peer_scatter_body, NK=NK),
        grid=(T // tmb,),
        in_specs=[spec, spec, spec],
        out_specs=pl.BlockSpec((NK, tmb, NK), lambda i: (0, i, 0)),
        out_shape=jax.ShapeDtypeStruct((NK, T, NK), BF16),
        scratch_shapes=[pltpu.VMEM((tmb * (NK + SCATTER_PITCH_PAD), NK), F32),
                        pltpu.VMEM((tmb, HK), F32),
                        pltpu.VMEM((tmb, HK), F32),
                        pltpu.VMEM((tmb, HK), F32)],
        compiler_params=_params("parallel"),
        name="peer_scatter",
    )(gate, e1, e2)


def _gelu(x):
    return 0.5 * x * (1.0 + lax.erf(x * (2.0 ** -0.5)))


def _peer_dense_body(xn_ref, u_ref, v_ref, w_ref, o_ref, *, NK):
    @pl.when(pl.program_id(1) == 0)
    def _():
        o_ref[...] = jnp.zeros_like(o_ref)

    act = lax.dot_general(xn_ref[...], u_ref[...], NT_DIMS, preferred_element_type=F32)
    coef = jnp.concatenate(
        [(w_ref[c].astype(F32) * _gelu(act[:, c * NK:(c + 1) * NK])).astype(BF16)
         for c in range(w_ref.shape[0])], axis=1)
    o_ref[...] += jnp.dot(coef, v_ref[...], preferred_element_type=F32)


def peer_dense(xn, u_all, v_all, layer, w, tm=1024, te=512):
    T, D = xn.shape
    E = u_all.shape[1]
    NK = w.shape[0]
    tm, te = _tile(T, tm), _tile(E, te)
    assert te % NK == 0 and E == NK * NK
    return pl.pallas_call(
        functools.partial(_peer_dense_body, NK=NK),
        grid=(T // tm, E // te),
        in_specs=[pl.BlockSpec((tm, D), lambda i, j: (i, 0)),
                  pl.BlockSpec((None, te, D), lambda i, j: (layer, j, 0)),
                  pl.BlockSpec((None, te, D), lambda i, j: (layer, j, 0)),
                  pl.BlockSpec((te // NK, tm, NK), lambda i, j: (j, i, 0))],
        out_specs=pl.BlockSpec((tm, D), lambda i, j: (i, 0)),
        out_shape=jax.ShapeDtypeStruct((T, D), F32),
        compiler_params=_params("parallel", "arbitrary"),
        name="peer_dense",
    )(xn, u_all, v_all, w)


def peer_layer(h, gain, wq, k1, k2, u_all, v_all, layer):
    xn, q = peer_qproj(h, gain, wq)
    gate, e1, e2 = peer_topk(q, k1, k2)
    w = peer_scatter(gate, e1, e2, k1.shape[1])
    return peer_dense(xn, u_all, v_all, layer, w)


def kernel(x, p, a_norm, a_w_in, a_gate_bias, a_head_norm, a_w_out, kv_norm, w_kv, b_norm, b_w_q, b_sinks,
           b_w_out, c_norm, peer_w_q, peer_k1, peer_k2, peer_u, peer_v, ple_norm, ple_w_gate, ple_w_proj,
           final_norm):
    B, S, D = x.shape
    T = B * S
    depth = p.shape[0]
    n_a = a_norm.shape[0]
    bf = lambda t: t.astype(BF16)
    h = x.reshape(T, D)
    u_all, v_all = bf(peer_u), bf(peer_v)
    for i in range(depth):
        if i < n_a:
            w_in = a_w_in[i]
            n_main = w_in.shape[1] - 2 * M_HEADS
            w_gates = jnp.pad(w_in[:, n_main:], ((0, 0), (0, V7X_LANES - 2 * M_HEADS)))
            proj = rms_matmul(h, a_norm[i], bf(w_in[:, :n_main]), BF16, tm=1024)
            gates = rms_matmul(h, a_norm[i], bf(w_gates), F32)[:, :2 * M_HEADS]
            y = mlstm_scan(proj, gates, a_gate_bias[i], a_head_norm[i], B, S)
            h = matmul_residual(y, bf(a_w_out[i]), h)
        else:
            j = i - n_a
            if j == 0:
                kv = rms_matmul(h, kv_norm, bf(w_kv), BF16)
            q = rms_matmul(h, b_norm[j], bf(b_w_q[j]), BF16, tm=1024, tn=1024)
            o = swa_attention(q, kv, b_sinks[j], B, S)
            h = matmul_residual(o, bf(b_w_out[j]), h)
        peer_out = peer_layer(h, c_norm[i], bf(peer_w_q[i]), bf(peer_k1[i]), bf(peer_k2[i]), u_all, v_all, i)
        h = ple_gate(h, peer_out, ple_norm[i], bf(ple_w_gate[i]), p[i].reshape(T, -1), bf(ple_w_proj[i]),
                     final_norm, final=(i == depth - 1))
    return h.reshape(B, S, D)
```

```python
import functools

import jax
import jax.numpy as jnp
from jax import lax
from jax.experimental import pallas as pl
from jax.experimental.pallas import tpu as pltpu

F32 = jnp.float32
BF16 = jnp.bfloat16

EPS = 1e-6
GATE_CAP = 15.0

M_HEADS = 4
A_HEAD_DIM = 64
A_GROUP = 8
WINDOW = 128
P_HEADS = 8
P_NKEYS = 128
P_TOPK = 16

V7X_LANES = 128
V7X_SUBLANES = 8
V7X_VMEM_BYTES = 64 * 1024 * 1024
VMEM_LIMIT = (V7X_VMEM_BYTES * 3) // 4

NT_DIMS = (((1,), (1,)), ((), ()))
TN_DIMS = (((0,), (0,)), ((), ()))


def _params(*sem):
    return pltpu.CompilerParams(dimension_semantics=sem, vmem_limit_bytes=VMEM_LIMIT)


def _rms(x, gain):
    ms = jnp.mean(x * x, axis=-1, keepdims=True)
    return x * lax.rsqrt(ms + EPS) * gain


def _tile(n, pref):
    t = min(n, pref)
    assert n % t == 0, (n, pref)
    return t


def _rms_matmul_body(x_ref, g_ref, w_ref, o_ref, xn_ref):
    @pl.when(pl.program_id(1) == 0)
    def _():
        xn_ref[...] = _rms(x_ref[...], g_ref[...]).astype(BF16)

    o_ref[...] = jnp.dot(xn_ref[...], w_ref[...], preferred_element_type=F32).astype(o_ref.dtype)


def rms_matmul(x, gain, w, out_dtype, tm=512, tn=512):
    T, D = x.shape
    N = w.shape[1]
    tm, tn = _tile(T, tm), _tile(N, tn)
    return pl.pallas_call(
        _rms_matmul_body,
        grid=(T // tm, N // tn),
        in_specs=[pl.BlockSpec((tm, D), lambda i, j: (i, 0)),
                  pl.BlockSpec((1, D), lambda i, j: (0, 0)),
                  pl.BlockSpec((D, tn), lambda i, j: (0, j))],
        out_specs=pl.BlockSpec((tm, tn), lambda i, j: (i, j)),
        out_shape=jax.ShapeDtypeStruct((T, N), out_dtype),
        scratch_shapes=[pltpu.VMEM((tm, D), BF16)],
        compiler_params=_params("parallel", "arbitrary"),
        name="rms_matmul",
    )(x, gain.reshape(1, D), w)


def _matmul_residual_body(y_ref, w_ref, h_ref, o_ref):
    o_ref[...] = h_ref[...] + jnp.dot(y_ref[...], w_ref[...], preferred_element_type=F32)


def matmul_residual(y, w, h, tm=1024, tn=1024):
    T, K = y.shape
    N = w.shape[1]
    tm, tn = _tile(T, tm), _tile(N, tn)
    return pl.pallas_call(
        _matmul_residual_body,
        grid=(T // tm, N // tn),
        in_specs=[pl.BlockSpec((tm, K), lambda i, j: (i, 0)),
                  pl.BlockSpec((K, tn), lambda i, j: (0, j)),
                  pl.BlockSpec((tm, tn), lambda i, j: (i, j))],
        out_specs=pl.BlockSpec((tm, tn), lambda i, j: (i, j)),
        out_shape=jax.ShapeDtypeStruct((T, N), F32),
        compiler_params=_params("parallel", "arbitrary"),
        name="matmul_residual",
    )(y, w, h)


def _ple_body(h_ref, d_ref, g_ref, wg_ref, p_ref, wp_ref, fn_ref, o_ref, *, final):
    h = h_ref[...] + d_ref[...]
    xn = _rms(h, g_ref[...]).astype(BF16)
    gate = jax.nn.sigmoid(jnp.dot(xn, wg_ref[...], preferred_element_type=F32))
    emb = jnp.dot(p_ref[...].astype(BF16), wp_ref[...], preferred_element_type=F32)
    out = h + gate * emb
    if final:
        out = _rms(out, fn_ref[...])
    o_ref[...] = out


def ple_gate(h, peer_out, gain, wg, p, wp, final_gain, final, tm=256):
    T, D = h.shape
    PD = p.shape[1]
    tm = _tile(T, tm)
    return pl.pallas_call(
        functools.partial(_ple_body, final=final),
        grid=(T // tm,),
        in_specs=[pl.BlockSpec((tm, D), lambda i: (i, 0)),
                  pl.BlockSpec((tm, D), lambda i: (i, 0)),
                  pl.BlockSpec((1, D), lambda i: (0, 0)),
                  pl.BlockSpec((D, D), lambda i: (0, 0)),
                  pl.BlockSpec((tm, PD), lambda i: (i, 0)),
                  pl.BlockSpec((PD, D), lambda i: (0, 0)),
                  pl.BlockSpec((1, D), lambda i: (0, 0))],
        out_specs=pl.BlockSpec((tm, D), lambda i: (i, 0)),
        out_shape=jax.ShapeDtypeStruct((T, D), F32),
        compiler_params=_params("parallel"),
        name="ple_gate",
    )(h, peer_out, gain.reshape(1, D), wg, p, wp, final_gain.reshape(1, D))


def _soft_cap(t):
    return GATE_CAP * jnp.tanh(t / GATE_CAP)


def _log_sigmoid(t):
    return jnp.minimum(t, 0.0) - jnp.log(1.0 + jnp.exp(-jnp.abs(t)))


def _split3(x):
    hi = x.astype(BF16)
    r = x - hi.astype(F32)
    mid = r.astype(BF16)
    lo = (r - mid.astype(F32)).astype(BF16)
    return hi, mid, lo


def _mlstm_body(q_ref, k_ref, v_ref, og_ref, gc_ref, gr_ref, bc_ref, br_ref, hn_ref, o_ref,
                c_ref, n_ref, m_ref, *, L, DK):
    @pl.when(pl.program_id(1) == 0)
    def _():
        c_ref[...] = jnp.zeros_like(c_ref)
        n_ref[...] = jnp.zeros_like(n_ref)
        m_ref[...] = jnp.zeros_like(m_ref)

    q = (q_ref[...].astype(F32) * (DK ** -0.5)).astype(BF16)
    k = k_ref[...]
    v = v_ref[...]

    gc = gc_ref[...] + bc_ref[...]
    gr = gr_ref[...] + br_ref[...]
    li_c = _soft_cap(gc[:, 0:1])
    lf_c = _log_sigmoid(_soft_cap(gc[:, 1:2]))
    li_r = _soft_cap(gr[0:1, :])
    lf_r = _log_sigmoid(_soft_cap(gr[1:2, :]))

    row = lax.broadcasted_iota(jnp.int32, (L, L), 0)
    col = lax.broadcasted_iota(jnp.int32, (L, L), 1)
    causal = row >= col
    tri_l = causal.astype(BF16)
    tri_u = (row <= col).astype(BF16)
    b_c = sum(jnp.dot(tri_l, part, preferred_element_type=F32)
              for part in _split3(jnp.broadcast_to(lf_c, (L, V7X_LANES))))[:, 0:1]
    b_r = sum(jnp.dot(part, tri_u, preferred_element_type=F32)
              for part in _split3(jnp.broadcast_to(lf_r, (V7X_SUBLANES, L))))[0:1, :]

    m_prev = m_ref[0:1, 0:1]
    dmat = jnp.where(causal, b_c - b_r + li_r, -jnp.inf)
    inter = b_c + m_prev
    m_t = jnp.maximum(inter, jnp.max(dmat, axis=-1, keepdims=True))
    w_intra = jnp.exp(dmat - m_t)
    w_inter = jnp.exp(inter - m_t)

    s = lax.dot_general(q, k, NT_DIMS, preferred_element_type=F32) * w_intra
    c_old = c_ref[...]
    n_old = n_ref[...]
    num = (w_inter * jnp.dot(q, c_old.astype(BF16), preferred_element_type=F32)
           + jnp.dot(s.astype(BF16), v, preferred_element_type=F32))
    qn = lax.dot_general(q, jnp.broadcast_to(n_old, (V7X_LANES, DK)).astype(BF16), NT_DIMS,
                         preferred_element_type=F32)[:, 0:1]
    den = w_inter * qn + jnp.sum(s, axis=-1, keepdims=True)
    hh = num / jnp.maximum(jnp.abs(den), jnp.exp(-m_t))

    m_new = m_t[L - 1:L, :]
    b_last = b_c[L - 1:L, :]
    w_state = jnp.exp(b_last - b_c + li_c - m_new)
    decay = jnp.exp(b_last + m_prev - m_new)
    kw = k.astype(F32) * w_state
    c_ref[...] = decay * c_old + lax.dot_general(kw.astype(BF16), v, TN_DIMS, preferred_element_type=F32)
    n_ref[...] = decay * n_old + jnp.sum(kw, axis=0, keepdims=True)
    m_ref[...] = jnp.broadcast_to(m_new, m_ref.shape)

    ms = jnp.mean(hh * hh, axis=-1, keepdims=True)
    y = hh * lax.rsqrt(ms + EPS) * hn_ref[...] * jax.nn.sigmoid(og_ref[...].astype(F32))
    o_ref[...] = y.astype(o_ref.dtype)


def mlstm_scan(proj, gates, gate_bias, head_norm, B, S, chunk=256):
    T = proj.shape[0]
    H = M_HEADS
    HDV = head_norm.shape[0]
    DV = HDV // H
    DK = (proj.shape[1] - 2 * HDV) // (2 * H)
    L = _tile(S, chunk)
    NC = S // L
    assert DV % DK == 0 or DK % DV == 0
    k_off = (H * DK) // DK
    v_off = (2 * H * DK) // DV
    og_off = (2 * H * DK + H * DV) // DV
    g2 = gates.reshape(T, 2, H)
    g_col = g2.transpose(2, 0, 1)
    g_row = g2.transpose(2, 1, 0)
    b_col = gate_bias.T.reshape(H, 1, 2)
    b_row = gate_bias.T.reshape(H, 2, 1)

    def tok(bh, c):
        return (bh // H) * NC + c

    return pl.pallas_call(
        functools.partial(_mlstm_body, L=L, DK=DK),
        grid=(B * H, NC),
        in_specs=[pl.BlockSpec((L, DK), lambda bh, c: (tok(bh, c), bh % H)),
                  pl.BlockSpec((L, DK), lambda bh, c: (tok(bh, c), k_off + bh % H)),
                  pl.BlockSpec((L, DV), lambda bh, c: (tok(bh, c), v_off + bh % H)),
                  pl.BlockSpec((L, DV), lambda bh, c: (tok(bh, c), og_off + bh % H)),
                  pl.BlockSpec((None, L, 2), lambda bh, c: (bh % H, tok(bh, c), 0)),
                  pl.BlockSpec((None, 2, L), lambda bh, c: (bh % H, 0, tok(bh, c))),
                  pl.BlockSpec((None, 1, 2), lambda bh, c: (bh % H, 0, 0)),
                  pl.BlockSpec((None, 2, 1), lambda bh, c: (bh % H, 0, 0)),
                  pl.BlockSpec((1, DV), lambda bh, c: (0, bh % H))],
        out_specs=pl.BlockSpec((L, DV), lambda bh, c: (tok(bh, c), bh % H)),
        out_shape=jax.ShapeDtypeStruct((T, HDV), BF16),
        scratch_shapes=[pltpu.VMEM((DK, DV), F32),
                        pltpu.VMEM((1, DK), F32),
                        pltpu.VMEM((V7X_SUBLANES, V7X_LANES), F32)],
        compiler_params=_params("parallel", "arbitrary"),
        name="mlstm_scan",
    )(proj, proj, proj, proj, g_col, g_row, b_col, b_row, head_norm.reshape(1, HDV))


def _swa_body(sink_ref, q_ref, kp_ref, kc_ref, o_ref, *, BLK, KVH, GROUP, HD):
    first = pl.program_id(1) == 0
    kv = jnp.concatenate([kp_ref[...], kc_ref[...]], axis=0).astype(F32)
    ri = lax.broadcasted_iota(jnp.int32, (BLK, 2 * BLK), 0)
    ci = lax.broadcasted_iota(jnp.int32, (BLK, 2 * BLK), 1)
    dist = ri + BLK - ci
    mask = (dist >= 0) & (dist < WINDOW) & ((ci >= BLK) | jnp.logical_not(first))
    lo = lax.broadcasted_iota(jnp.int32, (1, 2 * HD), 1) < HD
    scale = HD ** -0.5
    for pair in range(KVH // 2):
        xk = kv[:, pair * 2 * HD:(pair + 1) * 2 * HD]
        xv = kv[:, (KVH + pair * 2) * HD:(KVH + pair * 2 + 2) * HD]
        xk_r = pltpu.roll(xk, HD, 1)
        xv_r = pltpu.roll(xv, HD, 1)
        for sub in range(2):
            kh = 2 * pair + sub
            if sub == 0:
                k2 = jnp.where(lo, xk, xk_r)
                v_lo = jnp.where(lo, xv, 0.0)
                v_hi = jnp.where(lo, 0.0, xv_r)
            else:
                k2 = jnp.where(lo, xk_r, xk)
                v_lo = jnp.where(lo, xv_r, 0.0)
                v_hi = jnp.where(lo, 0.0, xv)
            k2 = k2.astype(BF16)
            v_halves = (v_lo.astype(BF16), v_hi.astype(BF16))
            for gp in range(GROUP // 2):
                h0 = kh * GROUP + 2 * gp
                q2 = q_ref[:, h0 * HD:(h0 + 2) * HD]
                acc = None
                for e in range(2):
                    qm = jnp.where(lo if e == 0 else jnp.logical_not(lo), q2, jnp.zeros_like(q2))
                    s = lax.dot_general(qm, k2, NT_DIMS, preferred_element_type=F32) * scale
                    s = jnp.where(mask, s, -jnp.inf)
                    sink = sink_ref[h0 + e]
                    mx = jnp.maximum(jnp.max(s, axis=-1, keepdims=True), sink)
                    pr = jnp.exp(s - mx)
                    den = jnp.sum(pr, axis=-1, keepdims=True) + jnp.exp(sink - mx)
                    o = jnp.dot(pr.astype(BF16), v_halves[e], preferred_element_type=F32) / den
                    acc = o if acc is None else acc + o
                o_ref[:, h0 * HD:(h0 + 2) * HD] = acc.astype(o_ref.dtype)


def swa_attention(q, kv, sinks, B, S):
    T, QD = q.shape
    HD, GROUP, BLK = A_HEAD_DIM, A_GROUP, WINDOW
    KVH = kv.shape[1] // (2 * HD)
    assert QD == KVH * GROUP * HD and 2 * HD == V7X_LANES and KVH % 2 == 0 and GROUP % 2 == 0
    NB = S // BLK
    return pl.pallas_call(
        functools.partial(_swa_body, BLK=BLK, KVH=KVH, GROUP=GROUP, HD=HD),
        grid=(B, NB),
        in_specs=[pl.BlockSpec(memory_space=pltpu.SMEM),
                  pl.BlockSpec((BLK, QD), lambda b, n: (b * NB + n, 0)),
                  pl.BlockSpec((BLK, 2 * KVH * HD), lambda b, n: (b * NB + jnp.maximum(n - 1, 0), 0)),
                  pl.BlockSpec((BLK, 2 * KVH * HD), lambda b, n: (b * NB + n, 0))],
        out_specs=pl.BlockSpec((BLK, QD), lambda b, n: (b * NB + n, 0)),
        out_shape=jax.ShapeDtypeStruct((T, QD), BF16),
        compiler_params=_params("parallel", "arbitrary"),
        name="swa_attention",
    )(sinks, q, kv, kv)


KEY_BIG = 1 << 30


def _topk_rows(problems, k):
    tm = problems[0][0].shape[1]
    slot = lax.broadcasted_iota(jnp.int32, (k, tm), 0)
    state = [[vals, keys, [], [], jnp.zeros((k, tm), vals.dtype), jnp.zeros((k, tm), keys.dtype)]
             for vals, keys in problems]
    for it in range(k):
        for st in state:
            vals, keys = st[0], st[1]
            m = jnp.max(vals, axis=0, keepdims=True)
            kmin = jnp.min(jnp.where(vals == m, keys, KEY_BIG), axis=0, keepdims=True)
            st[0] = jnp.where(keys == kmin, -jnp.inf, vals)
            st[2].append(m)
            st[3].append(kmin)
            st[4] = jnp.where(slot == it, m, st[4])
            st[5] = jnp.where(slot == it, kmin, st[5])
    return [tuple(st[2:]) for st in state]


def _oddeven_merge_sort(n):
    pairs = []
    p = 1
    while p < n:
        k = p
        while k >= 1:
            for j in range(k % p, n - k, 2 * k):
                for i in range(min(k, n - j - k)):
                    if (i + j) // (2 * p) == (i + j + k) // (2 * p):
                        pairs.append((i + j, i + j + k))
            k //= 2
        p *= 2
    return pairs


def _topk_rows_distinct(problems, k):
    sub = V7X_SUBLANES
    tl = problems[0][0].shape[1]
    slot = lax.broadcasted_iota(jnp.int32, (k, tl), 0)
    state = []
    for vals, keys in problems:
        n = vals.shape[0] // sub
        v = [vals[j * sub:(j + 1) * sub] for j in range(n)]
        q = [keys[j * sub:(j + 1) * sub] for j in range(n)]
        size = 1
        while size < n:
            size *= 2
        for i, j in _oddeven_merge_sort(size):
            if j < n:
                first = v[i] >= v[j]
                v[i], v[j] = jnp.maximum(v[i], v[j]), jnp.minimum(v[i], v[j])
                q[i], q[j] = jnp.where(first, q[i], q[j]), jnp.where(first, q[j], q[i])
        state.append(dict(v=v, q=q, n=n, rows_v=[], rows_k=[], arr_v=jnp.zeros((k, tl), vals.dtype),
                          arr_k=jnp.zeros((k, tl), keys.dtype), tie=jnp.zeros((1, tl), jnp.bool_)))
    for it in range(k):
        for st in state:
            v, q, n = st["v"], st["q"], st["n"]
            m = jnp.max(v[0], axis=0, keepdims=True)
            pop = v[0] == m
            key = jnp.max(jnp.where(pop, q[0], -1), axis=0, keepdims=True)
            hits = jnp.sum(pop.astype(F32), axis=0, keepdims=True)
            st["tie"] = st["tie"] | (hits != 1.0)
            if st["rows_v"]:
                st["tie"] = st["tie"] | (m == st["rows_v"][-1])
            depth = min(n, k - it)
            for j in range(depth):
                v[j] = jnp.where(pop, v[j + 1] if j + 1 < n else -jnp.inf, v[j])
                if j + 1 < n:
                    q[j] = jnp.where(pop, q[j + 1], q[j])
            st["rows_v"].append(m)
            st["rows_k"].append(key)
            st["arr_v"] = jnp.where(slot == it, m, st["arr_v"])
            st["arr_k"] = jnp.where(slot == it, key, st["arr_k"])
    out = []
    for st in state:
        tie = st["tie"] | (jnp.max(st["v"][0], axis=0, keepdims=True) == st["rows_v"][-1])
        out.append(((st["rows_v"], st["rows_k"], st["arr_v"], st["arr_k"]), tie))
    return out


def _candidates(top1, top2, k):
    r1_v, r1_i, v1, i1 = top1
    r2_v, r2_i, v2, i2 = top2
    tm = v1.shape[1]
    sub = V7X_SUBLANES
    vals, keys = [], []

    def pack(code, e1, e2):
        return (code << 14) | (e1 << 7) | e2

    b_iota = lax.broadcasted_iota(jnp.int32, (k, tm), 0)
    vals.append(r1_v[0] + v2)
    keys.append(pack(b_iota, r1_i[0], i2))
    a = 1
    while a < k and k // (a + 1) >= 2:
        nb = k // (a + 1)
        rows = -(-nb // sub) * sub
        bi = lax.broadcasted_iota(jnp.int32, (rows, tm), 0)
        vals.append(jnp.where(bi < nb, r1_v[a] + v2[0:rows], -jnp.inf))
        keys.append(pack(a * k + bi, r1_i[a], i2[0:rows]))
        a += 1
    if a < k:
        a0 = (a // sub) * sub
        ai = lax.broadcasted_iota(jnp.int32, (k - a0, tm), 0) + a0
        vals.append(jnp.where(ai >= a, v1[a0:k] + r2_v[0], -jnp.inf))
        keys.append(pack(ai * k, i1[a0:k], r2_i[0]))
    return jnp.concatenate(vals, axis=0), jnp.concatenate(keys, axis=0)


def _peer_qproj_body(x_ref, g_ref, w_ref, xn_ref, q_ref):
    @pl.when(pl.program_id(1) == 0)
    def _():
        xn_ref[...] = _rms(x_ref[...], g_ref[...]).astype(BF16)

    q_ref[...] = jnp.dot(xn_ref[...], w_ref[...], preferred_element_type=F32).astype(q_ref.dtype)


def peer_qproj(h, gain, wq, tm=1024, tn=1024):
    T, D = h.shape
    N = wq.shape[1]
    tm, tn = _tile(T, tm), _tile(N, tn)
    return pl.pallas_call(
        _peer_qproj_body,
        grid=(T // tm, N // tn),
        in_specs=[pl.BlockSpec((tm, D), lambda i, j: (i, 0)),
                  pl.BlockSpec((1, D), lambda i, j: (0, 0)),
                  pl.BlockSpec((D, tn), lambda i, j: (0, j))],
        out_specs=[pl.BlockSpec((tm, D), lambda i, j: (i, 0)),
                   pl.BlockSpec((tm, tn), lambda i, j: (i, j))],
        out_shape=[jax.ShapeDtypeStruct((T, D), BF16), jax.ShapeDtypeStruct((T, N), BF16)],
        compiler_params=_params("parallel", "arbitrary"),
        name="peer_qproj",
    )(h, gain.reshape(1, D), wq)


TOPK_TOKENS = 256


def _key_scores(q_ref, k1, k2, s1_ref, s2_ref):
    half = k1.shape[1]
    s1_ref[...] = lax.dot_general(k1, q_ref[:, 0:half], NT_DIMS, preferred_element_type=F32)
    s2_ref[...] = lax.dot_general(k2, q_ref[:, half:2 * half], NT_DIMS, preferred_element_type=F32)


def _select_from_scores(s1_ref, s2_ref, gate_ref, e1_ref, e2_ref, *, K):
    NK = s1_ref.shape[0]
    tl = V7X_LANES
    key_iota = lax.broadcasted_iota(jnp.int32, (NK, tl), 0)
    groups = [slice(c * tl, (c + 1) * tl) for c in range(s1_ref.shape[1] // tl)]

    def emit(lanes, top):
        rows_s, _, top_s, top_key = top
        ex = jnp.exp(top_s - rows_s[0])
        gate_ref[:, lanes] = ex / jnp.sum(ex, axis=0, keepdims=True)
        e1_ref[:, lanes] = ((top_key >> 7) & 127).astype(F32)
        e2_ref[:, lanes] = (top_key & 127).astype(F32)

    stage1 = _topk_rows_distinct(
        [(ref[:, lanes], key_iota) for lanes in groups for ref in (s1_ref, s2_ref)], K)
    stage2 = _topk_rows_distinct(
        [_candidates(stage1[2 * c][0], stage1[2 * c + 1][0], K) for c in range(len(groups))], K)
    for c, lanes in enumerate(groups):
        emit(lanes, stage2[c][0])

    for c, lanes in enumerate(groups):
        any_tie = jnp.max((stage1[2 * c][1] | stage1[2 * c + 1][1] | stage2[c][1]).astype(jnp.int32))

        @pl.when(any_tie > 0)
        def _():
            top1, top2 = _topk_rows([(s1_ref[:, lanes], key_iota), (s2_ref[:, lanes], key_iota)], K)
            emit(lanes, _topk_rows([_candidates(top1, top2, K)], K)[0])


def _peer_topk_body(q_ref, k1_ref, k2_ref, gate_ref, e1_ref, e2_ref, s1_ref, s2_ref, *, K):
    hd = pl.program_id(1)
    _key_scores(q_ref, k1_ref[hd], k2_ref[hd], s1_ref, s2_ref)
    _select_from_scores(s1_ref, s2_ref, gate_ref, e1_ref, e2_ref, K=K)


def peer_topk(q, k1, k2):
    T = q.shape[0]
    PH, NK, HALF = k1.shape
    K = P_TOPK
    assert NK == V7X_LANES and HALF == V7X_LANES and q.shape[1] == PH * 2 * HALF
    tq = _tile(T, TOPK_TOKENS)
    k_spec = pl.BlockSpec((PH, NK, HALF), lambda c, hd: (0, 0, 0))
    sel_spec = pl.BlockSpec((K, tq), lambda c, hd: (hd, c))
    sel = jax.ShapeDtypeStruct((PH * K, T), F32)
    return pl.pallas_call(
        functools.partial(_peer_topk_body, K=K),
        grid=(T // tq, PH),
        in_specs=[pl.BlockSpec((tq, 2 * HALF), lambda c, hd: (c, hd)), k_spec, k_spec],
        out_specs=[sel_spec] * 3, out_shape=[sel] * 3,
        scratch_shapes=[pltpu.VMEM((NK, tq), F32)] * 2,
        compiler_params=_params("parallel", "arbitrary"),
        name="peer_topk",
    )(q, k1, k2)


SCATTER_UNROLL = 64
SCATTER_PITCH_PAD = V7X_SUBLANES


def _peer_scatter_body(gate_ref, e1_ref, e2_ref, w_ref, scr_ref, gt_ref, e1t_ref, e2t_ref, *, NK):
    tmb = gt_ref.shape[0]
    pitch = NK + SCATTER_PITCH_PAD
    gt_ref[...] = gate_ref[...].T
    e1t_ref[...] = e1_ref[...].T
    e2t_ref[...] = e2_ref[...].T
    key_iota = lax.broadcasted_iota(jnp.int32, (NK, gt_ref.shape[1]), 0).astype(F32)

    def token_group(grp, carry):
        for u in range(SCATTER_UNROLL):
            t = grp * SCATTER_UNROLL + u
            g_row = gt_ref[pl.ds(t, 1), :]
            sel1 = jnp.where(e1t_ref[pl.ds(t, 1), :] == key_iota, g_row, 0.0).astype(BF16)
            sel2 = jnp.where(e2t_ref[pl.ds(t, 1), :] == key_iota, 1.0, 0.0).astype(BF16)
            w_t = lax.dot_general(sel1, sel2, NT_DIMS, preferred_element_type=F32)
            scr_ref[pl.ds(pl.multiple_of(t * pitch, V7X_SUBLANES), NK), :] = w_t
        return carry

    lax.fori_loop(0, tmb // SCATTER_UNROLL, token_group, 0)
    for j in range(NK):
        w_ref[j] = scr_ref[pl.ds(j, tmb, stride=pitch), :].astype(w_ref.dtype)


def peer_scatter(gate, e1, e2, NK, tmb=128):
    HK, T = gate.shape
    tmb = _tile(T, tmb)
    spec = pl.BlockSpec((HK, tmb), lambda i: (0, i))
    return pl.pallas_call(
        functools.partial(_peer_scatter_body, NK=NK),
        grid=(T // tmb,),
        in_specs=[spec, spec, spec],
        out_specs=pl.BlockSpec((NK, tmb, NK), lambda i: (0, i, 0)),
        out_shape=jax.ShapeDtypeStruct((NK, T, NK), BF16),
        scratch_shapes=[pltpu.VMEM((tmb * (NK + SCATTER_PITCH_PAD), NK), F32),
                        pltpu.VMEM((tmb, HK), F32),
                        pltpu.VMEM((tmb, HK), F32),
                        pltpu.VMEM((tmb, HK), F32)],
        compiler_params=_params("parallel"),
        name="peer_scatter",
    )(gate, e1, e2)


def _gelu(x):
    return 0.5 * x * (1.0 + lax.erf(x * (2.0 ** -0.5)))


def _peer_dense_body(xn_ref, u_ref, v_ref, w_ref, o_ref, *, NK):
    @pl.when(pl.program_id(1) == 0)
    def _():
        o_ref[...] = jnp.zeros_like(o_ref)

    act = jnp.dot(xn_ref[...], u_ref[...], preferred_element_type=F32)
    coef = jnp.concatenate(
        [(w_ref[c].astype(F32) * _gelu(act[:, c * NK:(c + 1) * NK])).astype(BF16)
         for c in range(w_ref.shape[0])], axis=1)
    o_ref[...] += jnp.dot(coef, v_ref[...], preferred_element_type=F32)


def peer_dense(xn, ut_all, v_all, layer, w, tm=1024, te=512):
    T, D = xn.shape
    E = v_all.shape[1]
    NK = w.shape[0]
    tm, te = _tile(T, tm), _tile(E, te)
    assert te % NK == 0 and E == NK * NK
    return pl.pallas_call(
        functools.partial(_peer_dense_body, NK=NK),
        grid=(T // tm, E // te),
        in_specs=[pl.BlockSpec((tm, D), lambda i, j: (i, 0)),
                  pl.BlockSpec((None, D, te), lambda i, j: (layer, 0, j)),
                  pl.BlockSpec((None, te, D), lambda i, j: (layer, j, 0)),
                  pl.BlockSpec((te // NK, tm, NK), lambda i, j: (j, i, 0))],
        out_specs=pl.BlockSpec((tm, D), lambda i, j: (i, 0)),
        out_shape=jax.ShapeDtypeStruct((T, D), F32),
        compiler_params=_params("parallel", "arbitrary"),
        name="peer_dense",
    )(xn, ut_all, v_all, w)


def peer_layer(h, gain, wq, k1, k2, ut_all, v_all, layer):
    xn, q = peer_qproj(h, gain, wq)
    gate, e1, e2 = peer_topk(q, k1, k2)
    w = peer_scatter(gate, e1, e2, k1.shape[1])
    return peer_dense(xn, ut_all, v_all, layer, w)


def kernel(x, p, a_norm, a_w_in, a_gate_bias, a_head_norm, a_w_out, kv_norm, w_kv, b_norm, b_w_q, b_sinks,
           b_w_out, c_norm, peer_w_q, peer_k1, peer_k2, peer_u, peer_v, ple_norm, ple_w_gate, ple_w_proj,
           final_norm):
    B, S, D = x.shape
    T = B * S
    depth = p.shape[0]
    n_a = a_norm.shape[0]
    bf = lambda t: t.astype(BF16)
    h = x.reshape(T, D)
    ut_all, v_all = bf(peer_u).transpose(0, 2, 1), bf(peer_v)
    for i in range(depth):
        if i < n_a:
            w_in = a_w_in[i]
            n_main = w_in.shape[1] - 2 * M_HEADS
            w_gates = jnp.pad(w_in[:, n_main:], ((0, 0), (0, V7X_LANES - 2 * M_HEADS)))
            proj = rms_matmul(h, a_norm[i], bf(w_in[:, :n_main]), BF16, tm=1024)
            gates = rms_matmul(h, a_norm[i], bf(w_gates), F32)[:, :2 * M_HEADS]
            y = mlstm_scan(proj, gates, a_gate_bias[i], a_head_norm[i], B, S)
            h = matmul_residual(y, bf(a_w_out[i]), h)
        else:
            j = i - n_a
            if j == 0:
                kv = rms_matmul(h, kv_norm, bf(w_kv), BF16)
            q = rms_matmul(h, b_norm[j], bf(b_w_q[j]), BF16, tm=1024, tn=1024)
            o = swa_attention(q, kv, b_sinks[j], B, S)
            h = matmul_residual(o, bf(b_w_out[j]), h)
        peer_out = peer_layer(h, c_norm[i], bf(peer_w_q[i]), bf(peer_k1[i]), bf(peer_k2[i]), ut_all, v_all, i)
        h = ple_gate(h, peer_out, ple_norm[i], bf(ple_w_gate[i]), p[i].reshape(T, -1), bf(ple_w_proj[i]),
                     final_norm, final=(i == depth - 1))
    return h.reshape(B, S, D)
```

```python
import functools

import jax
import jax.numpy as jnp
from jax import lax
from jax.experimental import pallas as pl
from jax.experimental.pallas import tpu as pltpu

F32 = jnp.float32
BF16 = jnp.bfloat16

EPS = 1e-6
GATE_CAP = 15.0

M_HEADS = 4
A_HEAD_DIM = 64
A_GROUP = 8
WINDOW = 128
P_HEADS = 8
P_NKEYS = 128
P_TOPK = 16

V7X_LANES = 128
V7X_SUBLANES = 8
V7X_VMEM_BYTES = 64 * 1024 * 1024
VMEM_LIMIT = (V7X_VMEM_BYTES * 3) // 4

NT_DIMS = (((1,), (1,)), ((), ()))
TN_DIMS = (((0,), (0,)), ((), ()))


def _params(*sem):
    return pltpu.CompilerParams(dimension_semantics=sem, vmem_limit_bytes=VMEM_LIMIT)


def _rms(x, gain):
    ms = jnp.mean(x * x, axis=-1, keepdims=True)
    return x * lax.rsqrt(ms + EPS) * gain


def _tile(n, pref):
    t = min(n, pref)
    assert n % t == 0, (n, pref)
    return t


def _rms_matmul_body(x_ref, g_ref, w_ref, o_ref, xn_ref):
    @pl.when(pl.program_id(1) == 0)
    def _():
        xn_ref[...] = _rms(x_ref[...], g_ref[...]).astype(BF16)

    o_ref[...] = jnp.dot(xn_ref[...], w_ref[...], preferred_element_type=F32).astype(o_ref.dtype)


def rms_matmul(x, gain, w, out_dtype, tm=512, tn=512):
    T, D = x.shape
    N = w.shape[1]
    tm, tn = _tile(T, tm), _tile(N, tn)
    return pl.pallas_call(
        _rms_matmul_body,
        grid=(T // tm, N // tn),
        in_specs=[pl.BlockSpec((tm, D), lambda i, j: (i, 0)),
                  pl.BlockSpec((1, D), lambda i, j: (0, 0)),
                  pl.BlockSpec((D, tn), lambda i, j: (0, j))],
        out_specs=pl.BlockSpec((tm, tn), lambda i, j: (i, j)),
        out_shape=jax.ShapeDtypeStruct((T, N), out_dtype),
        scratch_shapes=[pltpu.VMEM((tm, D), BF16)],
        compiler_params=_params("parallel", "arbitrary"),
        name="rms_matmul",
    )(x, gain.reshape(1, D), w)


def _matmul_residual_body(y_ref, w_ref, h_ref, o_ref):
    o_ref[...] = h_ref[...] + jnp.dot(y_ref[...], w_ref[...], preferred_element_type=F32)


def matmul_residual(y, w, h, tm=1024, tn=1024):
    T, K = y.shape
    N = w.shape[1]
    tm, tn = _tile(T, tm), _tile(N, tn)
    return pl.pallas_call(
        _matmul_residual_body,
        grid=(T // tm, N // tn),
        in_specs=[pl.BlockSpec((tm, K), lambda i, j: (i, 0)),
                  pl.BlockSpec((K, tn), lambda i, j: (0, j)),
                  pl.BlockSpec((tm, tn), lambda i, j: (i, j))],
        out_specs=pl.BlockSpec((tm, tn), lambda i, j: (i, j)),
        out_shape=jax.ShapeDtypeStruct((T, N), F32),
        compiler_params=_params("parallel", "arbitrary"),
        name="matmul_residual",
    )(y, w, h)


def _ple_body(h_ref, d_ref, g_ref, wg_ref, p_ref, wp_ref, fn_ref, o_ref, *, final):
    h = h_ref[...] + d_ref[...]
    xn = _rms(h, g_ref[...]).astype(BF16)
    gate = jax.nn.sigmoid(jnp.dot(xn, wg_ref[...], preferred_element_type=F32))
    emb = jnp.dot(p_ref[...].astype(BF16), wp_ref[...], preferred_element_type=F32)
    out = h + gate * emb
    if final:
        out = _rms(out, fn_ref[...])
    o_ref[...] = out


def ple_gate(h, peer_out, gain, wg, p, wp, final_gain, final, tm=256):
    T, D = h.shape
    PD = p.shape[1]
    tm = _tile(T, tm)
    return pl.pallas_call(
        functools.partial(_ple_body, final=final),
        grid=(T // tm,),
        in_specs=[pl.BlockSpec((tm, D), lambda i: (i, 0)),
                  pl.BlockSpec((tm, D), lambda i: (i, 0)),
                  pl.BlockSpec((1, D), lambda i: (0, 0)),
                  pl.BlockSpec((D, D), lambda i: (0, 0)),
                  pl.BlockSpec((tm, PD), lambda i: (i, 0)),
                  pl.BlockSpec((PD, D), lambda i: (0, 0)),
                  pl.BlockSpec((1, D), lambda i: (0, 0))],
        out_specs=pl.BlockSpec((tm, D), lambda i: (i, 0)),
        out_shape=jax.ShapeDtypeStruct((T, D), F32),
        compiler_params=_params("parallel"),
        name="ple_gate",
    )(h, peer_out, gain.reshape(1, D), wg, p, wp, final_gain.reshape(1, D))


def _soft_cap(t):
    return GATE_CAP * jnp.tanh(t / GATE_CAP)


def _log_sigmoid(t):
    return jnp.minimum(t, 0.0) - jnp.log(1.0 + jnp.exp(-jnp.abs(t)))


def _split3(x):
    hi = x.astype(BF16)
    r = x - hi.astype(F32)
    mid = r.astype(BF16)
    lo = (r - mid.astype(F32)).astype(BF16)
    return hi, mid, lo


def _mlstm_body(q_ref, k_ref, v_ref, og_ref, gc_ref, gr_ref, bc_ref, br_ref, hn_ref, o_ref,
                c_ref, n_ref, m_ref, *, L, DK):
    @pl.when(pl.program_id(1) == 0)
    def _():
        c_ref[...] = jnp.zeros_like(c_ref)
        n_ref[...] = jnp.zeros_like(n_ref)
        m_ref[...] = jnp.zeros_like(m_ref)

    q = (q_ref[...].astype(F32) * (DK ** -0.5)).astype(BF16)
    k = k_ref[...]
    v = v_ref[...]

    gc = gc_ref[...] + bc_ref[...]
    gr = gr_ref[...] + br_ref[...]
    li_c = _soft_cap(gc[:, 0:1])
    lf_c = _log_sigmoid(_soft_cap(gc[:, 1:2]))
    li_r = _soft_cap(gr[0:1, :])
    lf_r = _log_sigmoid(_soft_cap(gr[1:2, :]))

    row = lax.broadcasted_iota(jnp.int32, (L, L), 0)
    col = lax.broadcasted_iota(jnp.int32, (L, L), 1)
    causal = row >= col
    tri_l = causal.astype(BF16)
    tri_u = (row <= col).astype(BF16)
    b_c = sum(jnp.dot(tri_l, part, preferred_element_type=F32)
              for part in _split3(jnp.broadcast_to(lf_c, (L, V7X_LANES))))[:, 0:1]
    b_r = sum(jnp.dot(part, tri_u, preferred_element_type=F32)
              for part in _split3(jnp.broadcast_to(lf_r, (V7X_SUBLANES, L))))[0:1, :]

    m_prev = m_ref[0:1, 0:1]
    dmat = jnp.where(causal, b_c - b_r + li_r, -jnp.inf)
    inter = b_c + m_prev
    m_t = jnp.maximum(inter, jnp.max(dmat, axis=-1, keepdims=True))
    w_intra = jnp.exp(dmat - m_t)
    w_inter = jnp.exp(inter - m_t)

    s = lax.dot_general(q, k, NT_DIMS, preferred_element_type=F32) * w_intra
    c_old = c_ref[...]
    n_old = n_ref[...]
    num = (w_inter * jnp.dot(q, c_old.astype(BF16), preferred_element_type=F32)
           + jnp.dot(s.astype(BF16), v, preferred_element_type=F32))
    qn = lax.dot_general(q, jnp.broadcast_to(n_old, (V7X_LANES, DK)).astype(BF16), NT_DIMS,
                         preferred_element_type=F32)[:, 0:1]
    den = w_inter * qn + jnp.sum(s, axis=-1, keepdims=True)
    hh = num / jnp.maximum(jnp.abs(den), jnp.exp(-m_t))

    m_new = m_t[L - 1:L, :]
    b_last = b_c[L - 1:L, :]
    w_state = jnp.exp(b_last - b_c + li_c - m_new)
    decay = jnp.exp(b_last + m_prev - m_new)
    kw = k.astype(F32) * w_state
    c_ref[...] = decay * c_old + lax.dot_general(kw.astype(BF16), v, TN_DIMS, preferred_element_type=F32)
    n_ref[...] = decay * n_old + jnp.sum(kw, axis=0, keepdims=True)
    m_ref[...] = jnp.broadcast_to(m_new, m_ref.shape)

    ms = jnp.mean(hh * hh, axis=-1, keepdims=True)
    y = hh * lax.rsqrt(ms + EPS) * hn_ref[...] * jax.nn.sigmoid(og_ref[...].astype(F32))
    o_ref[...] = y.astype(o_ref.dtype)


def mlstm_scan(proj, gates, gate_bias, head_norm, B, S, chunk=256):
    T = proj.shape[0]
    H = M_HEADS
    HDV = head_norm.shape[0]
    DV = HDV // H
    DK = (proj.shape[1] - 2 * HDV) // (2 * H)
    L = _tile(S, chunk)
    NC = S // L
    assert DV % DK == 0 or DK % DV == 0
    k_off = (H * DK) // DK
    v_off = (2 * H * DK) // DV
    og_off = (2 * H * DK + H * DV) // DV
    g2 = gates.reshape(T, 2, H)
    g_col = g2.transpose(2, 0, 1)
    g_row = g2.transpose(2, 1, 0)
    b_col = gate_bias.T.reshape(H, 1, 2)
    b_row = gate_bias.T.reshape(H, 2, 1)

    def tok(bh, c):
        return (bh // H) * NC + c

    return pl.pallas_call(
        functools.partial(_mlstm_body, L=L, DK=DK),
        grid=(B * H, NC),
        in_specs=[pl.BlockSpec((L, DK), lambda bh, c: (tok(bh, c), bh % H)),
                  pl.BlockSpec((L, DK), lambda bh, c: (tok(bh, c), k_off + bh % H)),
                  pl.BlockSpec((L, DV), lambda bh, c: (tok(bh, c), v_off + bh % H)),
                  pl.BlockSpec((L, DV), lambda bh, c: (tok(bh, c), og_off + bh % H)),
                  pl.BlockSpec((None, L, 2), lambda bh, c: (bh % H, tok(bh, c), 0)),
                  pl.BlockSpec((None, 2, L), lambda bh, c: (bh % H, 0, tok(bh, c))),
                  pl.BlockSpec((None, 1, 2), lambda bh, c: (bh % H, 0, 0)),
                  pl.BlockSpec((None, 2, 1), lambda bh, c: (bh % H, 0, 0)),
                  pl.BlockSpec((1, DV), lambda bh, c: (0, bh % H))],
        out_specs=pl.BlockSpec((L, DV), lambda bh, c: (tok(bh, c), bh % H)),
        out_shape=jax.ShapeDtypeStruct((T, HDV), BF16),
        scratch_shapes=[pltpu.VMEM((DK, DV), F32),
                        pltpu.VMEM((1, DK), F32),
                        pltpu.VMEM((V7X_SUBLANES, V7X_LANES), F32)],
        compiler_params=_params("parallel", "arbitrary"),
        name="mlstm_scan",
    )(proj, proj, proj, proj, g_col, g_row, b_col, b_row, head_norm.reshape(1, HDV))


def _swa_body(sink_ref, q_ref, kp_ref, kc_ref, o_ref, *, BLK, KVH, GROUP, HD):
    first = pl.program_id(1) == 0
    ri = lax.broadcasted_iota(jnp.int32, (BLK, BLK), 0)
    ci = lax.broadcasted_iota(jnp.int32, (BLK, BLK), 1)
    from_cur = ci <= ri
    prev_bias = jnp.where(jnp.logical_and(first, ci > ri), -jnp.inf, 0.0)
    lo = lax.broadcasted_iota(jnp.int32, (1, 2 * HD), 1) < HD
    scale = HD ** -0.5

    def head_pair_operands(kv, pair):
        xk = kv[:, pair * 2 * HD:(pair + 1) * 2 * HD]
        xv = kv[:, (KVH + pair * 2) * HD:(KVH + pair * 2 + 2) * HD]
        xk_r = pltpu.roll(xk, HD, 1)
        xv_r = pltpu.roll(xv, HD, 1)
        head0 = (jnp.where(lo, xk, xk_r), jnp.where(lo, xv, 0.0), jnp.where(lo, 0.0, xv_r))
        head1 = (jnp.where(lo, xk_r, xk), jnp.where(lo, xv_r, 0.0), jnp.where(lo, 0.0, xv))
        return [tuple(t.astype(BF16) for t in head) for head in (head0, head1)]

    kv = jnp.concatenate([kp_ref[...], kc_ref[...]], axis=0).astype(F32)
    for pair in range(KVH // 2):
        operands = head_pair_operands(kv, pair)
        for sub in range(2):
            kh = 2 * pair + sub
            k2, *v_halves = operands[sub]
            for gp in range(GROUP // 2):
                h0 = kh * GROUP + 2 * gp
                q2 = (q_ref[:, h0 * HD:(h0 + 2) * HD].astype(F32) * scale).astype(BF16)
                acc = None
                for e in range(2):
                    qm = jnp.where(lo if e == 0 else jnp.logical_not(lo), q2, jnp.zeros_like(q2))
                    s2 = lax.dot_general(qm, k2, NT_DIMS, preferred_element_type=F32)
                    s = jnp.where(from_cur, s2[:, BLK:], s2[:, :BLK] + prev_bias)
                    sink = sink_ref[h0 + e]
                    mx = jnp.maximum(jnp.max(s, axis=-1, keepdims=True), sink)
                    pr = jnp.exp(s - mx)
                    den = jnp.sum(pr, axis=-1, keepdims=True) + jnp.exp(sink - mx)
                    p2 = jnp.concatenate([jnp.where(from_cur, 0.0, pr), jnp.where(from_cur, pr, 0.0)], axis=1)
                    o = jnp.dot(p2.astype(BF16), v_halves[e], preferred_element_type=F32) / den
                    acc = o if acc is None else acc + o
                o_ref[:, h0 * HD:(h0 + 2) * HD] = acc.astype(o_ref.dtype)


def swa_attention(q, kv, sinks, B, S):
    T, QD = q.shape
    HD, GROUP, BLK = A_HEAD_DIM, A_GROUP, WINDOW
    KVH = kv.shape[1] // (2 * HD)
    assert QD == KVH * GROUP * HD and 2 * HD == V7X_LANES and KVH % 2 == 0 and GROUP % 2 == 0
    assert HD in (4, 16, 64, 256)
    NB = S // BLK
    return pl.pallas_call(
        functools.partial(_swa_body, BLK=BLK, KVH=KVH, GROUP=GROUP, HD=HD),
        grid=(B, NB),
        in_specs=[pl.BlockSpec(memory_space=pltpu.SMEM),
                  pl.BlockSpec((BLK, QD), lambda b, n: (b * NB + n, 0)),
                  pl.BlockSpec((BLK, 2 * KVH * HD), lambda b, n: (b * NB + jnp.maximum(n - 1, 0), 0)),
                  pl.BlockSpec((BLK, 2 * KVH * HD), lambda b, n: (b * NB + n, 0))],
        out_specs=pl.BlockSpec((BLK, QD), lambda b, n: (b * NB + n, 0)),
        out_shape=jax.ShapeDtypeStruct((T, QD), BF16),
        compiler_params=_params("parallel", "arbitrary"),
        name="swa_attention",
    )(sinks, q, kv, kv)


KEY_BIG = 1 << 30


def _topk_rows(problems, k):
    tm = problems[0][0].shape[1]
    slot = lax.broadcasted_iota(jnp.int32, (k, tm), 0)
    state = [[vals, keys, [], [], jnp.zeros((k, tm), vals.dtype), jnp.zeros((k, tm), keys.dtype)]
             for vals, keys in problems]
    for it in range(k):
        for st in state:
            vals, keys = st[0], st[1]
            m = jnp.max(vals, axis=0, keepdims=True)
            kmin = jnp.min(jnp.where(vals == m, keys, KEY_BIG), axis=0, keepdims=True)
            st[0] = jnp.where(keys == kmin, -jnp.inf, vals)
            st[2].append(m)
            st[3].append(kmin)
            st[4] = jnp.where(slot == it, m, st[4])
            st[5] = jnp.where(slot == it, kmin, st[5])
    return [tuple(st[2:]) for st in state]


def _oddeven_merge_sort(n):
    pairs = []
    p = 1
    while p < n:
        k = p
        while k >= 1:
            for j in range(k % p, n - k, 2 * k):
                for i in range(min(k, n - j - k)):
                    if (i + j) // (2 * p) == (i + j + k) // (2 * p):
                        pairs.append((i + j, i + j + k))
            k //= 2
        p *= 2
    return pairs


def _topk_rows_distinct(problems, k):
    sub = V7X_SUBLANES
    tl = problems[0][0].shape[1]
    slot = lax.broadcasted_iota(jnp.int32, (k, tl), 0)
    state = []
    for vals, keys in problems:
        n = vals.shape[0] // sub
        v = [vals[j * sub:(j + 1) * sub] for j in range(n)]
        q = [keys[j * sub:(j + 1) * sub] for j in range(n)]
        size = 1
        while size < n:
            size *= 2
        for i, j in _oddeven_merge_sort(size):
            if j < n:
                first = v[i] >= v[j]
                v[i], v[j] = jnp.maximum(v[i], v[j]), jnp.minimum(v[i], v[j])
                q[i], q[j] = jnp.where(first, q[i], q[j]), jnp.where(first, q[j], q[i])
        state.append(dict(v=v, q=q, n=n, rows_v=[], rows_k=[], arr_v=jnp.zeros((k, tl), vals.dtype),
                          arr_k=jnp.zeros((k, tl), keys.dtype), tie=jnp.zeros((1, tl), jnp.bool_),
                          pops=jnp.zeros((sub, tl), F32)))
    for it in range(k):
        for st in state:
            v, q, n = st["v"], st["q"], st["n"]
            m = jnp.max(v[0], axis=0, keepdims=True)
            pop = v[0] == m
            key = jnp.max(jnp.where(pop, q[0], -1), axis=0, keepdims=True)
            st["pops"] = st["pops"] + pop.astype(F32)
            if st["rows_v"]:
                st["tie"] = st["tie"] | (m == st["rows_v"][-1])
            depth = min(n, k - it)
            for j in range(depth):
                v[j] = jnp.where(pop, v[j + 1] if j + 1 < n else -jnp.inf, v[j])
                if j + 1 < n:
                    q[j] = jnp.where(pop, q[j + 1], q[j])
            st["rows_v"].append(m)
            st["rows_k"].append(key)
            st["arr_v"] = jnp.where(slot == it, m, st["arr_v"])
            st["arr_k"] = jnp.where(slot == it, key, st["arr_k"])
    out = []
    for st in state:
        tie = (st["tie"] | (jnp.max(st["v"][0], axis=0, keepdims=True) == st["rows_v"][-1])
               | (jnp.sum(st["pops"], axis=0, keepdims=True) != float(k)))
        out.append(((st["rows_v"], st["rows_k"], st["arr_v"], st["arr_k"]), tie))
    return out


def _candidates(top1, top2, k):
    r1_v, r1_i, v1, i1 = top1
    r2_v, r2_i, v2, i2 = top2
    tm = v1.shape[1]
    sub = V7X_SUBLANES
    vals, keys = [], []

    def pack(code, e1, e2):
        return (code << 14) | (e1 << 7) | e2

    b_iota = lax.broadcasted_iota(jnp.int32, (k, tm), 0)
    vals.append(r1_v[0] + v2)
    keys.append(pack(b_iota, r1_i[0], i2))
    a = 1
    while a < k and k // (a + 1) >= 2:
        nb = k // (a + 1)
        rows = -(-nb // sub) * sub
        bi = lax.broadcasted_iota(jnp.int32, (rows, tm), 0)
        vals.append(jnp.where(bi < nb, r1_v[a] + v2[0:rows], -jnp.inf))
        keys.append(pack(a * k + bi, r1_i[a], i2[0:rows]))
        a += 1
    if a < k:
        a0 = (a // sub) * sub
        ai = lax.broadcasted_iota(jnp.int32, (k - a0, tm), 0) + a0
        vals.append(jnp.where(ai >= a, v1[a0:k] + r2_v[0], -jnp.inf))
        keys.append(pack(ai * k, i1[a0:k], r2_i[0]))
    return jnp.concatenate(vals, axis=0), jnp.concatenate(keys, axis=0)


def _peer_qproj_body(x_ref, g_ref, w_ref, xn_ref, q_ref):
    @pl.when(pl.program_id(1) == 0)
    def _():
        xn_ref[...] = _rms(x_ref[...], g_ref[...]).astype(BF16)

    q_ref[...] = jnp.dot(xn_ref[...], w_ref[...], preferred_element_type=F32).astype(q_ref.dtype)


def peer_qproj(h, gain, wq, tm=1024, tn=1024):
    T, D = h.shape
    N = wq.shape[1]
    tm, tn = _tile(T, tm), _tile(N, tn)
    return pl.pallas_call(
        _peer_qproj_body,
        grid=(T // tm, N // tn),
        in_specs=[pl.BlockSpec((tm, D), lambda i, j: (i, 0)),
                  pl.BlockSpec((1, D), lambda i, j: (0, 0)),
                  pl.BlockSpec((D, tn), lambda i, j: (0, j))],
        out_specs=[pl.BlockSpec((tm, D), lambda i, j: (i, 0)),
                   pl.BlockSpec((tm, tn), lambda i, j: (i, j))],
        out_shape=[jax.ShapeDtypeStruct((T, D), BF16), jax.ShapeDtypeStruct((T, N), BF16)],
        compiler_params=_params("parallel", "arbitrary"),
        name="peer_qproj",
    )(h, gain.reshape(1, D), wq)


TOPK_TOKENS = 256


def _key_scores(q_ref, k1, k2, s1_ref, s2_ref):
    half = k1.shape[1]
    s1_ref[...] = lax.dot_general(k1, q_ref[:, 0:half], NT_DIMS, preferred_element_type=F32)
    s2_ref[...] = lax.dot_general(k2, q_ref[:, half:2 * half], NT_DIMS, preferred_element_type=F32)


def _select_from_scores(s1_ref, s2_ref, gate_ref, e1_ref, e2_ref, *, K):
    NK = s1_ref.shape[0]
    tl = V7X_LANES
    key_iota = lax.broadcasted_iota(jnp.int32, (NK, tl), 0)
    groups = [slice(c * tl, (c + 1) * tl) for c in range(s1_ref.shape[1] // tl)]

    def emit(lanes, top):
        rows_s, _, top_s, top_key = top
        ex = jnp.exp(top_s - rows_s[0])
        gate_ref[:, lanes] = ex / jnp.sum(ex, axis=0, keepdims=True)
        e1_ref[:, lanes] = ((top_key >> 7) & 127).astype(F32)
        e2_ref[:, lanes] = (top_key & 127).astype(F32)

    stage1 = _topk_rows_distinct(
        [(ref[:, lanes], key_iota) for lanes in groups for ref in (s1_ref, s2_ref)], K)
    stage2 = _topk_rows_distinct(
        [_candidates(stage1[2 * c][0], stage1[2 * c + 1][0], K) for c in range(len(groups))], K)
    for c, lanes in enumerate(groups):
        emit(lanes, stage2[c][0])

    for c, lanes in enumerate(groups):
        any_tie = jnp.max((stage1[2 * c][1] | stage1[2 * c + 1][1] | stage2[c][1]).astype(jnp.int32))

        @pl.when(any_tie > 0)
        def _():
            top1, top2 = _topk_rows([(s1_ref[:, lanes], key_iota), (s2_ref[:, lanes], key_iota)], K)
            emit(lanes, _topk_rows([_candidates(top1, top2, K)], K)[0])


def _peer_topk_body(q_ref, k1_ref, k2_ref, gate_ref, e1_ref, e2_ref, s1_ref, s2_ref, *, K):
    hd = pl.program_id(1)
    _key_scores(q_ref, k1_ref[hd], k2_ref[hd], s1_ref, s2_ref)
    _select_from_scores(s1_ref, s2_ref, gate_ref, e1_ref, e2_ref, K=K)


def peer_topk(q, k1, k2):
    T = q.shape[0]
    PH, NK, HALF = k1.shape
    K = P_TOPK
    assert NK == V7X_LANES and HALF == V7X_LANES and q.shape[1] == PH * 2 * HALF
    tq = _tile(T, TOPK_TOKENS)
    k_spec = pl.BlockSpec((PH, NK, HALF), lambda c, hd: (0, 0, 0))
    sel_spec = pl.BlockSpec((K, tq), lambda c, hd: (hd, c))
    sel = jax.ShapeDtypeStruct((PH * K, T), F32)
    return pl.pallas_call(
        functools.partial(_peer_topk_body, K=K),
        grid=(T // tq, PH),
        in_specs=[pl.BlockSpec((tq, 2 * HALF), lambda c, hd: (c, hd)), k_spec, k_spec],
        out_specs=[sel_spec] * 3, out_shape=[sel] * 3,
        scratch_shapes=[pltpu.VMEM((NK, tq), F32)] * 2,
        compiler_params=_params("parallel", "arbitrary"),
        name="peer_topk",
    )(q, k1, k2)


SCATTER_UNROLL = 64
SCATTER_PITCH_PAD = V7X_SUBLANES // 2
BF16_SUBLANES = 2 * V7X_SUBLANES


def _peer_scatter_body(gate_ref, e1_ref, e2_ref, w_ref, scr_ref, gt_ref, e1t_ref, e2t_ref, *, NK):
    tmb, HK = gt_ref.shape
    pad = SCATTER_PITCH_PAD
    pitch = NK + pad
    gt_ref[...] = gate_ref[...].T
    e1t_ref[...] = e1_ref[...].T
    e2t_ref[...] = e2_ref[...].T
    key_iota = lax.broadcasted_iota(jnp.int32, (NK, HK), 0).astype(F32)
    tall = NK + BF16_SUBLANES
    key_iota_odd = (lax.broadcasted_iota(jnp.int32, (tall, HK), 0) - pad).astype(F32)

    def token_group(grp, carry):
        base = pl.multiple_of(grp * (SCATTER_UNROLL * pitch), V7X_SUBLANES)
        for u in range(SCATTER_UNROLL):
            t = grp * SCATTER_UNROLL + u
            odd = u % 2
            g_row = gt_ref[pl.ds(t, 1), :]
            sel1 = jnp.where(e1t_ref[pl.ds(t, 1), :] == (key_iota_odd if odd else key_iota), g_row, 0.0)
            sel2 = jnp.where(e2t_ref[pl.ds(t, 1), :] == key_iota, 1.0, 0.0).astype(BF16)
            w_t = lax.dot_general(sel1.astype(BF16), sel2, NT_DIMS, preferred_element_type=F32)
            rows = NK + 2 * pad * odd
            start = pl.multiple_of(base + (u * pitch - pad * odd), V7X_SUBLANES)
            scr_ref[pl.ds(start, rows), :] = w_t[0:rows]
        return carry

    assert SCATTER_UNROLL % 2 == 0 and (SCATTER_UNROLL * pitch) % V7X_SUBLANES == 0
    lax.fori_loop(0, tmb // SCATTER_UNROLL, token_group, 0)
    for j in range(NK):
        w_ref[j] = scr_ref[pl.ds(j, tmb, stride=pitch), :].astype(w_ref.dtype)


def peer_scatter(gate, e1, e2, NK, tmb=128):
    HK, T = gate.shape
    tmb = _tile(T, tmb)
    spec = pl.BlockSpec((HK, tmb), lambda i: (0, i))
    return pl.pallas_call(
        functools.partial(_peer_scatter_body, NK=NK),
        grid=(T // tmb,),
        in_specs=[spec, spec, spec],
        out_specs=pl.BlockSpec((NK, tmb, NK), lambda i: (0, i, 0)),
        out_shape=jax.ShapeDtypeStruct((NK, T, NK), BF16),
        scratch_shapes=[pltpu.VMEM((tmb * (NK + SCATTER_PITCH_PAD), NK), F32),
                        pltpu.VMEM((tmb, HK), F32),
                        pltpu.VMEM((tmb, HK), F32),
                        pltpu.VMEM((tmb, HK), F32)],
        compiler_params=_params("parallel"),
        name="peer_scatter",
    )(gate, e1, e2)


def _gelu(x):
    return 0.5 * x * (1.0 + lax.erf(x * (2.0 ** -0.5)))


def _peer_dense_body(xn_ref, u_ref, v_ref, w_ref, o_ref, *, NK):
    @pl.when(pl.program_id(1) == 0)
    def _():
        o_ref[...] = jnp.zeros_like(o_ref)

    act = lax.dot_general(xn_ref[...], u_ref[...], NT_DIMS, preferred_element_type=F32)
    coef = jnp.concatenate(
        [(w_ref[c].astype(F32) * _gelu(act[:, c * NK:(c + 1) * NK])).astype(BF16)
         for c in range(w_ref.shape[0])], axis=1)
    o_ref[...] += jnp.dot(coef, v_ref[...], preferred_element_type=F32)


def peer_dense(xn, u_all, v_all, layer, w, tm=1024, te=512):
    T, D = xn.shape
    E = u_all.shape[1]
    NK = w.shape[0]
    tm, te = _tile(T, tm), _tile(E, te)
    assert te % NK == 0 and E == NK * NK
    return pl.pallas_call(
        functools.partial(_peer_dense_body, NK=NK),
        grid=(T // tm, E // te),
        in_specs=[pl.BlockSpec((tm, D), lambda i, j: (i, 0)),
                  pl.BlockSpec((None, te, D), lambda i, j: (layer, j, 0)),
                  pl.BlockSpec((None, te, D), lambda i, j: (layer, j, 0)),
                  pl.BlockSpec((te // NK, tm, NK), lambda i, j: (j, i, 0))],
        out_specs=pl.BlockSpec((tm, D), lambda i, j: (i, 0)),
        out_shape=jax.ShapeDtypeStruct((T, D), F32),
        compiler_params=_params("parallel", "arbitrary"),
        name="peer_dense",
    )(xn, u_all, v_all, w)


def peer_layer(h, gain, wq, k1, k2, u_all, v_all, layer):
    xn, q = peer_qproj(h, gain, wq)
    gate, e1, e2 = peer_topk(q, k1, k2)
    w = peer_scatter(gate, e1, e2, k1.shape[1])
    return peer_dense(xn, u_all, v_all, layer, w)


def kernel(x, p, a_norm, a_w_in, a_gate_bias, a_head_norm, a_w_out, kv_norm, w_kv, b_norm, b_w_q, b_sinks,
           b_w_out, c_norm, peer_w_q, peer_k1, peer_k2, peer_u, peer_v, ple_norm, ple_w_gate, ple_w_proj,
           final_norm):
    B, S, D = x.shape
    T = B * S
    depth = p.shape[0]
    n_a = a_norm.shape[0]
    bf = lambda t: t.astype(BF16)
    h = x.reshape(T, D)
    u_all, v_all = bf(peer_u), bf(peer_v)
    for i in range(depth):
        if i < n_a:
            w_in = a_w_in[i]
            n_main = w_in.shape[1] - 2 * M_HEADS
            w_gates = jnp.pad(w_in[:, n_main:], ((0, 0), (0, V7X_LANES - 2 * M_HEADS)))
            proj = rms_matmul(h, a_norm[i], bf(w_in[:, :n_main]), BF16, tm=1024)
            gates = rms_matmul(h, a_norm[i], bf(w_gates), F32)[:, :2 * M_HEADS]
            y = mlstm_scan(proj, gates, a_gate_bias[i], a_head_norm[i], B, S)
            h = matmul_residual(y, bf(a_w_out[i]), h)
        else:
            j = i - n_a
            if j == 0:
                kv = rms_matmul(h, kv_norm, bf(w_kv), BF16)
            q = rms_matmul(h, b_norm[j], bf(b_w_q[j]), BF16, tm=1024, tn=1024)
            o = swa_attention(q, kv, b_sinks[j], B, S)
            h = matmul_residual(o, bf(b_w_out[j]), h)
        peer_out = peer_layer(h, c_norm[i], bf(peer_w_q[i]), bf(peer_k1[i]), bf(peer_k2[i]), u_all, v_all, i)
        h = ple_gate(h, peer_out, ple_norm[i], bf(ple_w_gate[i]), p[i].reshape(T, -1), bf(ple_w_proj[i]),
                     final_norm, final=(i == depth - 1))
    return h.reshape(B, S, D)
```

```python
import functools

import jax
import jax.numpy as jnp
from jax import lax
from jax.experimental import pallas as pl
from jax.experimental.pallas import tpu as pltpu

F32 = jnp.float32
BF16 = jnp.bfloat16

EPS = 1e-6
GATE_CAP = 15.0

M_HEADS = 4
A_HEAD_DIM = 64
A_GROUP = 8
WINDOW = 128
P_HEADS = 8
P_NKEYS = 128
P_TOPK = 16

V7X_LANES = 128
V7X_SUBLANES = 8
V7X_VMEM_BYTES = 64 * 1024 * 1024
VMEM_LIMIT = (V7X_VMEM_BYTES * 3) // 4
VMEM_LIMIT_LARGE = (V7X_VMEM_BYTES * 7) // 8

NT_DIMS = (((1,), (1,)), ((), ()))
TN_DIMS = (((0,), (0,)), ((), ()))


def _params(*sem):
    return pltpu.CompilerParams(dimension_semantics=sem, vmem_limit_bytes=VMEM_LIMIT)


def _rms(x, gain):
    ms = jnp.mean(x * x, axis=-1, keepdims=True)
    return x * lax.rsqrt(ms + EPS) * gain


def _tile(n, pref):
    t = min(n, pref)
    assert n % t == 0, (n, pref)
    return t


def _rms_matmul_body(x_ref, g_ref, w_ref, o_ref, xn_ref):
    @pl.when(pl.program_id(1) == 0)
    def _():
        xn_ref[...] = _rms(x_ref[...], g_ref[...]).astype(BF16)

    o_ref[...] = jnp.dot(xn_ref[...], w_ref[...], preferred_element_type=F32).astype(o_ref.dtype)


def rms_matmul(x, gain, w, out_dtype, tm=512, tn=512):
    T, D = x.shape
    N = w.shape[1]
    tm, tn = _tile(T, tm), _tile(N, tn)
    return pl.pallas_call(
        _rms_matmul_body,
        grid=(T // tm, N // tn),
        in_specs=[pl.BlockSpec((tm, D), lambda i, j: (i, 0)),
                  pl.BlockSpec((1, D), lambda i, j: (0, 0)),
                  pl.BlockSpec((D, tn), lambda i, j: (0, j))],
        out_specs=pl.BlockSpec((tm, tn), lambda i, j: (i, j)),
        out_shape=jax.ShapeDtypeStruct((T, N), out_dtype),
        scratch_shapes=[pltpu.VMEM((tm, D), BF16)],
        compiler_params=_params("parallel", "arbitrary"),
        name="rms_matmul",
    )(x, gain.reshape(1, D), w)


def _matmul_residual_body(y_ref, w_ref, h_ref, o_ref):
    o_ref[...] = h_ref[...] + jnp.dot(y_ref[...], w_ref[...], preferred_element_type=F32)


def matmul_residual(y, w, h, tm=1024, tn=1024):
    T, K = y.shape
    N = w.shape[1]
    tm, tn = _tile(T, tm), _tile(N, tn)
    return pl.pallas_call(
        _matmul_residual_body,
        grid=(T // tm, N // tn),
        in_specs=[pl.BlockSpec((tm, K), lambda i, j: (i, 0)),
                  pl.BlockSpec((K, tn), lambda i, j: (0, j)),
                  pl.BlockSpec((tm, tn), lambda i, j: (i, j))],
        out_specs=pl.BlockSpec((tm, tn), lambda i, j: (i, j)),
        out_shape=jax.ShapeDtypeStruct((T, N), F32),
        compiler_params=_params("parallel", "arbitrary"),
        name="matmul_residual",
    )(y, w, h)


def _ple_body(h_ref, d_ref, g_ref, wg_ref, p_ref, wp_ref, fn_ref, o_ref, *, final):
    h = h_ref[...] + d_ref[...]
    xn = _rms(h, g_ref[...]).astype(BF16)
    gate = jax.nn.sigmoid(jnp.dot(xn, wg_ref[...], preferred_element_type=F32))
    emb = jnp.dot(p_ref[...].astype(BF16), wp_ref[...], preferred_element_type=F32)
    out = h + gate * emb
    if final:
        out = _rms(out, fn_ref[...])
    o_ref[...] = out


def ple_gate(h, peer_out, gain, wg, p, wp, final_gain, final, tm=256):
    T, D = h.shape
    PD = p.shape[1]
    tm = _tile(T, tm)
    return pl.pallas_call(
        functools.partial(_ple_body, final=final),
        grid=(T // tm,),
        in_specs=[pl.BlockSpec((tm, D), lambda i: (i, 0)),
                  pl.BlockSpec((tm, D), lambda i: (i, 0)),
                  pl.BlockSpec((1, D), lambda i: (0, 0)),
                  pl.BlockSpec((D, D), lambda i: (0, 0)),
                  pl.BlockSpec((tm, PD), lambda i: (i, 0)),
                  pl.BlockSpec((PD, D), lambda i: (0, 0)),
                  pl.BlockSpec((1, D), lambda i: (0, 0))],
        out_specs=pl.BlockSpec((tm, D), lambda i: (i, 0)),
        out_shape=jax.ShapeDtypeStruct((T, D), F32),
        compiler_params=_params("parallel"),
        name="ple_gate",
    )(h, peer_out, gain.reshape(1, D), wg, p, wp, final_gain.reshape(1, D))


def _soft_cap(t):
    return GATE_CAP * jnp.tanh(t / GATE_CAP)


def _log_sigmoid(t):
    return jnp.minimum(t, 0.0) - jnp.log(1.0 + jnp.exp(-jnp.abs(t)))


def _split3(x):
    hi = x.astype(BF16)
    r = x - hi.astype(F32)
    mid = r.astype(BF16)
    lo = (r - mid.astype(F32)).astype(BF16)
    return hi, mid, lo


def _mlstm_body(q_ref, k_ref, v_ref, og_ref, gc_ref, gr_ref, bc_ref, br_ref, hn_ref, o_ref,
                c_ref, n_ref, m_ref, *, L, DK):
    @pl.when(pl.program_id(1) == 0)
    def _():
        c_ref[...] = jnp.zeros_like(c_ref)
        n_ref[...] = jnp.zeros_like(n_ref)
        m_ref[...] = jnp.zeros_like(m_ref)

    q = (q_ref[...].astype(F32) * (DK ** -0.5)).astype(BF16)
    k = k_ref[...]
    v = v_ref[...]

    gc = gc_ref[...] + bc_ref[...]
    gr = gr_ref[...] + br_ref[...]
    li_c = _soft_cap(gc[:, 0:1])
    lf_c = _log_sigmoid(_soft_cap(gc[:, 1:2]))
    li_r = _soft_cap(gr[0:1, :])
    lf_r = _log_sigmoid(_soft_cap(gr[1:2, :]))

    row = lax.broadcasted_iota(jnp.int32, (L, L), 0)
    col = lax.broadcasted_iota(jnp.int32, (L, L), 1)
    causal = row >= col
    tri_l = causal.astype(BF16)
    tri_u = (row <= col).astype(BF16)
    b_c = sum(jnp.dot(tri_l, part, preferred_element_type=F32)
              for part in _split3(jnp.broadcast_to(lf_c, (L, V7X_LANES))))[:, 0:1]
    b_r = sum(jnp.dot(part, tri_u, preferred_element_type=F32)
              for part in _split3(jnp.broadcast_to(lf_r, (V7X_SUBLANES, L))))[0:1, :]

    m_prev = m_ref[0:1, 0:1]
    dmat = jnp.where(causal, b_c - b_r + li_r, -jnp.inf)
    inter = b_c + m_prev
    m_t = jnp.maximum(inter, jnp.max(dmat, axis=-1, keepdims=True))
    w_intra = jnp.exp(dmat - m_t)
    w_inter = jnp.exp(inter - m_t)

    s = lax.dot_general(q, k, NT_DIMS, preferred_element_type=F32) * w_intra
    c_old = c_ref[...]
    n_old = n_ref[...]
    num = (w_inter * jnp.dot(q, c_old.astype(BF16), preferred_element_type=F32)
           + jnp.dot(s.astype(BF16), v, preferred_element_type=F32))
    qn = lax.dot_general(q, jnp.broadcast_to(n_old, (V7X_LANES, DK)).astype(BF16), NT_DIMS,
                         preferred_element_type=F32)[:, 0:1]
    den = w_inter * qn + jnp.sum(s, axis=-1, keepdims=True)
    hh = num / jnp.maximum(jnp.abs(den), jnp.exp(-m_t))

    m_new = m_t[L - 1:L, :]
    b_last = b_c[L - 1:L, :]
    w_state = jnp.exp(b_last - b_c + li_c - m_new)
    decay = jnp.exp(b_last + m_prev - m_new)
    kw = k.astype(F32) * w_state
    c_ref[...] = decay * c_old + lax.dot_general(kw.astype(BF16), v, TN_DIMS, preferred_element_type=F32)
    n_ref[...] = decay * n_old + jnp.sum(kw, axis=0, keepdims=True)
    m_ref[...] = jnp.broadcast_to(m_new, m_ref.shape)

    ms = jnp.mean(hh * hh, axis=-1, keepdims=True)
    y = hh * lax.rsqrt(ms + EPS) * hn_ref[...] * jax.nn.sigmoid(og_ref[...].astype(F32))
    o_ref[...] = y.astype(o_ref.dtype)


def mlstm_scan(proj, gates, gate_bias, head_norm, B, S, chunk=256):
    T = proj.shape[0]
    H = M_HEADS
    HDV = head_norm.shape[0]
    DV = HDV // H
    DK = (proj.shape[1] - 2 * HDV) // (2 * H)
    L = _tile(S, chunk)
    NC = S // L
    assert DV % DK == 0 or DK % DV == 0
    k_off = (H * DK) // DK
    v_off = (2 * H * DK) // DV
    og_off = (2 * H * DK + H * DV) // DV
    g2 = gates.reshape(T, 2, H)
    g_col = g2.transpose(2, 0, 1)
    g_row = g2.transpose(2, 1, 0)
    b_col = gate_bias.T.reshape(H, 1, 2)
    b_row = gate_bias.T.reshape(H, 2, 1)

    def tok(bh, c):
        return (bh // H) * NC + c

    return pl.pallas_call(
        functools.partial(_mlstm_body, L=L, DK=DK),
        grid=(B * H, NC),
        in_specs=[pl.BlockSpec((L, DK), lambda bh, c: (tok(bh, c), bh % H)),
                  pl.BlockSpec((L, DK), lambda bh, c: (tok(bh, c), k_off + bh % H)),
                  pl.BlockSpec((L, DV), lambda bh, c: (tok(bh, c), v_off + bh % H)),
                  pl.BlockSpec((L, DV), lambda bh, c: (tok(bh, c), og_off + bh % H)),
                  pl.BlockSpec((None, L, 2), lambda bh, c: (bh % H, tok(bh, c), 0)),
                  pl.BlockSpec((None, 2, L), lambda bh, c: (bh % H, 0, tok(bh, c))),
                  pl.BlockSpec((None, 1, 2), lambda bh, c: (bh % H, 0, 0)),
                  pl.BlockSpec((None, 2, 1), lambda bh, c: (bh % H, 0, 0)),
                  pl.BlockSpec((1, DV), lambda bh, c: (0, bh % H))],
        out_specs=pl.BlockSpec((L, DV), lambda bh, c: (tok(bh, c), bh % H)),
        out_shape=jax.ShapeDtypeStruct((T, HDV), BF16),
        scratch_shapes=[pltpu.VMEM((DK, DV), F32),
                        pltpu.VMEM((1, DK), F32),
                        pltpu.VMEM((V7X_SUBLANES, V7X_LANES), F32)],
        compiler_params=_params("parallel", "arbitrary"),
        name="mlstm_scan",
    )(proj, proj, proj, proj, g_col, g_row, b_col, b_row, head_norm.reshape(1, HDV))


def _swa_body(sink_ref, q_ref, kp_ref, kc_ref, o_ref, *, BLK, KVH, GROUP, HD):
    first = pl.program_id(1) == 0
    ri = lax.broadcasted_iota(jnp.int32, (BLK, BLK), 0)
    ci = lax.broadcasted_iota(jnp.int32, (BLK, BLK), 1)
    from_cur = ci <= ri
    prev_bias = jnp.where(jnp.logical_and(first, ci > ri), -jnp.inf, 0.0)
    lo = lax.broadcasted_iota(jnp.int32, (1, 2 * HD), 1) < HD
    scale = HD ** -0.5

    def head_pair_operands(kv, pair):
        xk = kv[:, pair * 2 * HD:(pair + 1) * 2 * HD]
        xv = kv[:, (KVH + pair * 2) * HD:(KVH + pair * 2 + 2) * HD]
        xk_r = pltpu.roll(xk, HD, 1)
        xv_r = pltpu.roll(xv, HD, 1)
        head0 = (jnp.where(lo, xk, xk_r), jnp.where(lo, xv, 0.0), jnp.where(lo, 0.0, xv_r))
        head1 = (jnp.where(lo, xk_r, xk), jnp.where(lo, xv_r, 0.0), jnp.where(lo, 0.0, xv))
        return [tuple(t.astype(BF16) for t in head) for head in (head0, head1)]

    kv = jnp.concatenate([kp_ref[...], kc_ref[...]], axis=0).astype(F32)
    for pair in range(KVH // 2):
        operands = head_pair_operands(kv, pair)
        for sub in range(2):
            kh = 2 * pair + sub
            k2, *v_halves = operands[sub]
            for gp in range(GROUP // 2):
                h0 = kh * GROUP + 2 * gp
                q2 = (q_ref[:, h0 * HD:(h0 + 2) * HD].astype(F32) * scale).astype(BF16)
                acc = None
                for e in range(2):
                    qm = jnp.where(lo if e == 0 else jnp.logical_not(lo), q2, jnp.zeros_like(q2))
                    s2 = lax.dot_general(qm, k2, NT_DIMS, preferred_element_type=F32)
                    s = jnp.where(from_cur, s2[:, BLK:], s2[:, :BLK] + prev_bias)
                    sink = sink_ref[h0 + e]
                    mx = jnp.maximum(jnp.max(s, axis=-1, keepdims=True), sink)
                    pr = jnp.exp(s - mx)
                    den = jnp.sum(pr, axis=-1, keepdims=True) + jnp.exp(sink - mx)
                    p2 = jnp.concatenate([jnp.where(from_cur, 0.0, pr), jnp.where(from_cur, pr, 0.0)], axis=1)
                    o = jnp.dot(p2.astype(BF16), v_halves[e], preferred_element_type=F32) / den
                    acc = o if acc is None else acc + o
                o_ref[:, h0 * HD:(h0 + 2) * HD] = acc.astype(o_ref.dtype)


def swa_attention(q, kv, sinks, B, S):
    T, QD = q.shape
    HD, GROUP, BLK = A_HEAD_DIM, A_GROUP, WINDOW
    KVH = kv.shape[1] // (2 * HD)
    assert QD == KVH * GROUP * HD and 2 * HD == V7X_LANES and KVH % 2 == 0 and GROUP % 2 == 0
    assert HD in (4, 16, 64, 256)
    NB = S // BLK
    return pl.pallas_call(
        functools.partial(_swa_body, BLK=BLK, KVH=KVH, GROUP=GROUP, HD=HD),
        grid=(B, NB),
        in_specs=[pl.BlockSpec(memory_space=pltpu.SMEM),
                  pl.BlockSpec((BLK, QD), lambda b, n: (b * NB + n, 0)),
                  pl.BlockSpec((BLK, 2 * KVH * HD), lambda b, n: (b * NB + jnp.maximum(n - 1, 0), 0)),
                  pl.BlockSpec((BLK, 2 * KVH * HD), lambda b, n: (b * NB + n, 0))],
        out_specs=pl.BlockSpec((BLK, QD), lambda b, n: (b * NB + n, 0)),
        out_shape=jax.ShapeDtypeStruct((T, QD), BF16),
        compiler_params=_params("parallel", "arbitrary"),
        name="swa_attention",
    )(sinks, q, kv, kv)


KEY_BIG = 1 << 30


def _topk_rows(problems, k):
    tm = problems[0][0].shape[1]
    slot = lax.broadcasted_iota(jnp.int32, (k, tm), 0)
    state = [[vals, keys, [], [], jnp.zeros((k, tm), vals.dtype), jnp.zeros((k, tm), keys.dtype)]
             for vals, keys in problems]
    for it in range(k):
        for st in state:
            vals, keys = st[0], st[1]
            m = jnp.max(vals, axis=0, keepdims=True)
            kmin = jnp.min(jnp.where(vals == m, keys, KEY_BIG), axis=0, keepdims=True)
            st[0] = jnp.where(keys == kmin, -jnp.inf, vals)
            st[2].append(m)
            st[3].append(kmin)
            st[4] = jnp.where(slot == it, m, st[4])
            st[5] = jnp.where(slot == it, kmin, st[5])
    return [tuple(st[2:]) for st in state]


def _oddeven_merge_sort(n):
    pairs = []
    p = 1
    while p < n:
        k = p
        while k >= 1:
            for j in range(k % p, n - k, 2 * k):
                for i in range(min(k, n - j - k)):
                    if (i + j) // (2 * p) == (i + j + k) // (2 * p):
                        pairs.append((i + j, i + j + k))
            k //= 2
        p *= 2
    return pairs


def _topk_rows_distinct(problems, k):
    sub = V7X_SUBLANES
    tl = problems[0][0].shape[1]
    slot = lax.broadcasted_iota(jnp.int32, (k, tl), 0)
    state = []
    for vals, keys in problems:
        n = vals.shape[0] // sub
        v = [vals[j * sub:(j + 1) * sub] for j in range(n)]
        q = [keys[j * sub:(j + 1) * sub] for j in range(n)]
        size = 1
        while size < n:
            size *= 2
        for i, j in _oddeven_merge_sort(size):
            if j < n:
                first = v[i] >= v[j]
                v[i], v[j] = jnp.maximum(v[i], v[j]), jnp.minimum(v[i], v[j])
                q[i], q[j] = jnp.where(first, q[i], q[j]), jnp.where(first, q[j], q[i])
        state.append(dict(v=v, q=q, n=n, rows_v=[], rows_k=[], arr_v=jnp.zeros((k, tl), vals.dtype),
                          arr_k=jnp.zeros((k, tl), keys.dtype), tie=jnp.zeros((1, tl), jnp.bool_),
                          pops=jnp.zeros((sub, tl), F32)))
    for it in range(k):
        for st in state:
            v, q, n = st["v"], st["q"], st["n"]
            m = jnp.max(v[0], axis=0, keepdims=True)
            pop = v[0] == m
            key = jnp.max(jnp.where(pop, q[0], -1), axis=0, keepdims=True)
            st["pops"] = st["pops"] + pop.astype(F32)
            if st["rows_v"]:
                st["tie"] = st["tie"] | (m == st["rows_v"][-1])
            depth = min(n, k - it)
            for j in range(depth):
                v[j] = jnp.where(pop, v[j + 1] if j + 1 < n else -jnp.inf, v[j])
                if j + 1 < n:
                    q[j] = jnp.where(pop, q[j + 1], q[j])
            st["rows_v"].append(m)
            st["rows_k"].append(key)
            st["arr_v"] = jnp.where(slot == it, m, st["arr_v"])
            st["arr_k"] = jnp.where(slot == it, key, st["arr_k"])
    out = []
    for st in state:
        tie = (st["tie"] | (jnp.max(st["v"][0], axis=0, keepdims=True) == st["rows_v"][-1])
               | (jnp.sum(st["pops"], axis=0, keepdims=True) != float(k)))
        out.append(((st["rows_v"], st["rows_k"], st["arr_v"], st["arr_k"]), tie))
    return out


def _candidates(top1, top2, k):
    r1_v, r1_i, v1, i1 = top1
    r2_v, r2_i, v2, i2 = top2
    tm = v1.shape[1]
    sub = V7X_SUBLANES
    vals, keys = [], []

    def pack(code, e1, e2):
        return (code << 14) | (e1 << 7) | e2

    b_iota = lax.broadcasted_iota(jnp.int32, (k, tm), 0)
    vals.append(r1_v[0] + v2)
    keys.append(pack(b_iota, r1_i[0], i2))
    a = 1
    while a < k and k // (a + 1) >= 2:
        nb = k // (a + 1)
        rows = -(-nb // sub) * sub
        bi = lax.broadcasted_iota(jnp.int32, (rows, tm), 0)
        vals.append(jnp.where(bi < nb, r1_v[a] + v2[0:rows], -jnp.inf))
        keys.append(pack(a * k + bi, r1_i[a], i2[0:rows]))
        a += 1
    if a < k:
        a0 = (a // sub) * sub
        ai = lax.broadcasted_iota(jnp.int32, (k - a0, tm), 0) + a0
        vals.append(jnp.where(ai >= a, v1[a0:k] + r2_v[0], -jnp.inf))
        keys.append(pack(ai * k, i1[a0:k], r2_i[0]))
    return jnp.concatenate(vals, axis=0), jnp.concatenate(keys, axis=0)


def _peer_qproj_body(x_ref, g_ref, w_ref, xn_ref, q_ref):
    @pl.when(pl.program_id(1) == 0)
    def _():
        xn_ref[...] = _rms(x_ref[...], g_ref[...]).astype(BF16)

    q_ref[...] = jnp.dot(xn_ref[...], w_ref[...], preferred_element_type=F32).astype(q_ref.dtype)


def peer_qproj(h, gain, wq, tm=1024, tn=1024):
    T, D = h.shape
    N = wq.shape[1]
    tm, tn = _tile(T, tm), _tile(N, tn)
    return pl.pallas_call(
        _peer_qproj_body,
        grid=(T // tm, N // tn),
        in_specs=[pl.BlockSpec((tm, D), lambda i, j: (i, 0)),
                  pl.BlockSpec((1, D), lambda i, j: (0, 0)),
                  pl.BlockSpec((D, tn), lambda i, j: (0, j))],
        out_specs=[pl.BlockSpec((tm, D), lambda i, j: (i, 0)),
                   pl.BlockSpec((tm, tn), lambda i, j: (i, j))],
        out_shape=[jax.ShapeDtypeStruct((T, D), BF16), jax.ShapeDtypeStruct((T, N), BF16)],
        compiler_params=_params("parallel", "arbitrary"),
        name="peer_qproj",
    )(h, gain.reshape(1, D), wq)


TOPK_TOKENS = 256


def _key_scores(q_ref, k1, k2, s1_ref, s2_ref):
    half = k1.shape[1]
    s1_ref[...] = lax.dot_general(k1, q_ref[:, 0:half], NT_DIMS, preferred_element_type=F32)
    s2_ref[...] = lax.dot_general(k2, q_ref[:, half:2 * half], NT_DIMS, preferred_element_type=F32)


def _select_from_scores(s1_ref, s2_ref, gate_ref, e1_ref, e2_ref, *, K):
    NK = s1_ref.shape[0]
    tl = V7X_LANES
    key_iota = lax.broadcasted_iota(jnp.int32, (NK, tl), 0)
    groups = [slice(c * tl, (c + 1) * tl) for c in range(s1_ref.shape[1] // tl)]

    def emit(lanes, top):
        rows_s, _, top_s, top_key = top
        ex = jnp.exp(top_s - rows_s[0])
        gate_ref[:, lanes] = ex / jnp.sum(ex, axis=0, keepdims=True)
        e1_ref[:, lanes] = ((top_key >> 7) & 127).astype(F32)
        e2_ref[:, lanes] = (top_key & 127).astype(F32)

    stage1 = _topk_rows_distinct(
        [(ref[:, lanes], key_iota) for lanes in groups for ref in (s1_ref, s2_ref)], K)
    stage2 = _topk_rows_distinct(
        [_candidates(stage1[2 * c][0], stage1[2 * c + 1][0], K) for c in range(len(groups))], K)
    for c, lanes in enumerate(groups):
        emit(lanes, stage2[c][0])

    for c, lanes in enumerate(groups):
        any_tie = jnp.max((stage1[2 * c][1] | stage1[2 * c + 1][1] | stage2[c][1]).astype(jnp.int32))

        @pl.when(any_tie > 0)
        def _():
            top1, top2 = _topk_rows([(s1_ref[:, lanes], key_iota), (s2_ref[:, lanes], key_iota)], K)
            emit(lanes, _topk_rows([_candidates(top1, top2, K)], K)[0])


def _peer_topk_body(q_ref, k1_ref, k2_ref, gate_ref, e1_ref, e2_ref, s1_ref, s2_ref, *, K):
    hd = pl.program_id(1)
    _key_scores(q_ref, k1_ref[hd], k2_ref[hd], s1_ref, s2_ref)
    _select_from_scores(s1_ref, s2_ref, gate_ref, e1_ref, e2_ref, K=K)


def peer_topk(q, k1, k2):
    T = q.shape[0]
    PH, NK, HALF = k1.shape
    K = P_TOPK
    assert NK == V7X_LANES and HALF == V7X_LANES and q.shape[1] == PH * 2 * HALF
    tq = _tile(T, TOPK_TOKENS)
    k_spec = pl.BlockSpec((PH, NK, HALF), lambda c, hd: (0, 0, 0))
    sel_spec = pl.BlockSpec((K, tq), lambda c, hd: (hd, c))
    sel = jax.ShapeDtypeStruct((PH * K, T), F32)
    return pl.pallas_call(
        functools.partial(_peer_topk_body, K=K),
        grid=(T // tq, PH),
        in_specs=[pl.BlockSpec((tq, 2 * HALF), lambda c, hd: (c, hd)), k_spec, k_spec],
        out_specs=[sel_spec] * 3, out_shape=[sel] * 3,
        scratch_shapes=[pltpu.VMEM((NK, tq), F32)] * 2,
        compiler_params=_params("parallel", "arbitrary"),
        name="peer_topk",
    )(q, k1, k2)


SCATTER_UNROLL = 64
SCATTER_PITCH_PAD = V7X_SUBLANES // 2
BF16_SUBLANES = 2 * V7X_SUBLANES


def _peer_scatter_body(gate_ref, e1_ref, e2_ref, w_ref, scr_ref, gt_ref, e1t_ref, e2t_ref, *, NK):
    tmb, HK = gt_ref.shape
    pad = SCATTER_PITCH_PAD
    pitch = NK + pad
    gt_ref[...] = gate_ref[...].T
    e1t_ref[...] = e1_ref[...].T
    e2t_ref[...] = e2_ref[...].T
    key_iota = lax.broadcasted_iota(jnp.int32, (NK, HK), 0).astype(F32)
    tall = NK + BF16_SUBLANES
    key_iota_odd = (lax.broadcasted_iota(jnp.int32, (tall, HK), 0) - pad).astype(F32)

    def token_group(grp, carry):
        base = pl.multiple_of(grp * (SCATTER_UNROLL * pitch), V7X_SUBLANES)
        for u in range(SCATTER_UNROLL):
            t = grp * SCATTER_UNROLL + u
            odd = u % 2
            g_row = gt_ref[pl.ds(t, 1), :]
            sel1 = jnp.where(e1t_ref[pl.ds(t, 1), :] == (key_iota_odd if odd else key_iota), g_row, 0.0)
            sel2 = jnp.where(e2t_ref[pl.ds(t, 1), :] == key_iota, 1.0, 0.0).astype(BF16)
            w_t = lax.dot_general(sel1.astype(BF16), sel2, NT_DIMS, preferred_element_type=F32)
            rows = NK + 2 * pad * odd
            start = pl.multiple_of(base + (u * pitch - pad * odd), V7X_SUBLANES)
            scr_ref[pl.ds(start, rows), :] = w_t[0:rows]
        return carry

    assert SCATTER_UNROLL % 2 == 0 and (SCATTER_UNROLL * pitch) % V7X_SUBLANES == 0
    lax.fori_loop(0, tmb // SCATTER_UNROLL, token_group, 0)
    for j in range(NK):
        w_ref[j] = scr_ref[pl.ds(j, tmb, stride=pitch), :].astype(w_ref.dtype)


def peer_scatter(gate, e1, e2, NK, tmb=128):
    HK, T = gate.shape
    tmb = _tile(T, tmb)
    spec = pl.BlockSpec((HK, tmb), lambda i: (0, i))
    return pl.pallas_call(
        functools.partial(_peer_scatter_body, NK=NK),
        grid=(T // tmb,),
        in_specs=[spec, spec, spec],
        out_specs=pl.BlockSpec((NK, tmb, NK), lambda i: (0, i, 0)),
        out_shape=jax.ShapeDtypeStruct((NK, T, NK), BF16),
        scratch_shapes=[pltpu.VMEM((tmb * (NK + SCATTER_PITCH_PAD), NK), F32),
                        pltpu.VMEM((tmb, HK), F32),
                        pltpu.VMEM((tmb, HK), F32),
                        pltpu.VMEM((tmb, HK), F32)],
        compiler_params=_params("parallel"),
        name="peer_scatter",
    )(gate, e1, e2)


def _gelu(x):
    return 0.5 * x * (1.0 + lax.erf(x * (2.0 ** -0.5)))


def _peer_dense_body(xn_ref, u_ref, v_ref, w_ref, o_ref, *, NK):
    @pl.when(pl.program_id(1) == 0)
    def _():
        o_ref[...] = jnp.zeros_like(o_ref)

    act = lax.dot_general(xn_ref[...], u_ref[...], NT_DIMS, preferred_element_type=F32)
    coef = jnp.concatenate(
        [(w_ref[c].astype(F32) * _gelu(act[:, c * NK:(c + 1) * NK])).astype(BF16)
         for c in range(w_ref.shape[0])], axis=1)
    o_ref[...] += jnp.dot(coef, v_ref[...], preferred_element_type=F32)


def peer_dense(xn, u_all, v_all, layer, w, tm=1024, te=1024):
    T, D = xn.shape
    E = u_all.shape[1]
    NK = w.shape[0]
    tm, te = _tile(T, tm), _tile(E, te)
    assert te % NK == 0 and E == NK * NK
    return pl.pallas_call(
        functools.partial(_peer_dense_body, NK=NK),
        grid=(T // tm, E // te),
        in_specs=[pl.BlockSpec((tm, D), lambda i, j: (i, 0)),
                  pl.BlockSpec((None, te, D), lambda i, j: (layer, j, 0)),
                  pl.BlockSpec((None, te, D), lambda i, j: (layer, j, 0)),
                  pl.BlockSpec((te // NK, tm, NK), lambda i, j: (j, i, 0))],
        out_specs=pl.BlockSpec((tm, D), lambda i, j: (i, 0)),
        out_shape=jax.ShapeDtypeStruct((T, D), F32),
        compiler_params=pltpu.CompilerParams(dimension_semantics=("parallel", "arbitrary"),
                                             vmem_limit_bytes=VMEM_LIMIT_LARGE),
        name="peer_dense",
    )(xn, u_all, v_all, w)


def peer_layer(h, gain, wq, k1, k2, u_all, v_all, layer):
    xn, q = peer_qproj(h, gain, wq)
    gate, e1, e2 = peer_topk(q, k1, k2)
    w = peer_scatter(gate, e1, e2, k1.shape[1])
    return peer_dense(xn, u_all, v_all, layer, w)


def kernel(x, p, a_norm, a_w_in, a_gate_bias, a_head_norm, a_w_out, kv_norm, w_kv, b_norm, b_w_q, b_sinks,
           b_w_out, c_norm, peer_w_q, peer_k1, peer_k2, peer_u, peer_v, ple_norm, ple_w_gate, ple_w_proj,
           final_norm):
    B, S, D = x.shape
    T = B * S
    depth = p.shape[0]
    n_a = a_norm.shape[0]
    bf = lambda t: t.astype(BF16)
    h = x.reshape(T, D)
    u_all, v_all = bf(peer_u), bf(peer_v)
    for i in range(depth):
        if i < n_a:
            w_in = a_w_in[i]
            n_main = w_in.shape[1] - 2 * M_HEADS
            w_gates = jnp.pad(w_in[:, n_main:], ((0, 0), (0, V7X_LANES - 2 * M_HEADS)))
            proj = rms_matmul(h, a_norm[i], bf(w_in[:, :n_main]), BF16, tm=1024, tn=1024)
            gates = rms_matmul(h, a_norm[i], bf(w_gates), F32)[:, :2 * M_HEADS]
            y = mlstm_scan(proj, gates, a_gate_bias[i], a_head_norm[i], B, S)
            h = matmul_residual(y, bf(a_w_out[i]), h)
        else:
            j = i - n_a
            if j == 0:
                kv = rms_matmul(h, kv_norm, bf(w_kv), BF16)
            q = rms_matmul(h, b_norm[j], bf(b_w_q[j]), BF16, tm=1024, tn=1024)
            o = swa_attention(q, kv, b_sinks[j], B, S)
            h = matmul_residual(o, bf(b_w_out[j]), h)
        peer_out = peer_layer(h, c_norm[i], bf(peer_w_q[i]), bf(peer_k1[i]), bf(peer_k2[i]), u_all, v_all, i)
        h = ple_gate(h, peer_out, ple_norm[i], bf(ple_w_gate[i]), p[i].reshape(T, -1), bf(ple_w_proj[i]),
                     final_norm, final=(i == depth - 1))
    return h.reshape(B, S, D)
```

```python
import functools

import jax
import jax.numpy as jnp
from jax import lax
from jax.experimental import pallas as pl
from jax.experimental.pallas import tpu as pltpu

F32 = jnp.float32
BF16 = jnp.bfloat16

EPS = 1e-6
GATE_CAP = 15.0

M_HEADS = 4
A_HEAD_DIM = 64
A_GROUP = 8
WINDOW = 128
P_HEADS = 8
P_NKEYS = 128
P_TOPK = 16

V7X_LANES = 128
V7X_SUBLANES = 8
V7X_VMEM_BYTES = 64 * 1024 * 1024
VMEM_LIMIT = (V7X_VMEM_BYTES * 3) // 4
VMEM_LIMIT_LARGE = (V7X_VMEM_BYTES * 7) // 8

NT_DIMS = (((1,), (1,)), ((), ()))
TN_DIMS = (((0,), (0,)), ((), ()))


def _params(*sem):
    return pltpu.CompilerParams(dimension_semantics=sem, vmem_limit_bytes=VMEM_LIMIT)


def _rms(x, gain):
    ms = jnp.mean(x * x, axis=-1, keepdims=True)
    return x * lax.rsqrt(ms + EPS) * gain


def _tile(n, pref):
    t = min(n, pref)
    assert n % t == 0, (n, pref)
    return t


def _rms_matmul_body(x_ref, g_ref, w_ref, o_ref, xn_ref):
    @pl.when(pl.program_id(1) == 0)
    def _():
        xn_ref[...] = _rms(x_ref[...], g_ref[...]).astype(BF16)

    o_ref[...] = jnp.dot(xn_ref[...], w_ref[...], preferred_element_type=F32).astype(o_ref.dtype)


def rms_matmul(x, gain, w, out_dtype, tm=512, tn=512):
    T, D = x.shape
    N = w.shape[1]
    tm, tn = _tile(T, tm), _tile(N, tn)
    return pl.pallas_call(
        _rms_matmul_body,
        grid=(T // tm, N // tn),
        in_specs=[pl.BlockSpec((tm, D), lambda i, j: (i, 0)),
                  pl.BlockSpec((1, D), lambda i, j: (0, 0)),
                  pl.BlockSpec((D, tn), lambda i, j: (0, j))],
        out_specs=pl.BlockSpec((tm, tn), lambda i, j: (i, j)),
        out_shape=jax.ShapeDtypeStruct((T, N), out_dtype),
        scratch_shapes=[pltpu.VMEM((tm, D), BF16)],
        compiler_params=_params("parallel", "arbitrary"),
        name="rms_matmul",
    )(x, gain.reshape(1, D), w)


def _matmul_residual_body(y_ref, w_ref, h_ref, o_ref):
    o_ref[...] = h_ref[...] + jnp.dot(y_ref[...], w_ref[...], preferred_element_type=F32)


def matmul_residual(y, w, h, tm=1024, tn=1024):
    T, K = y.shape
    N = w.shape[1]
    tm, tn = _tile(T, tm), _tile(N, tn)
    return pl.pallas_call(
        _matmul_residual_body,
        grid=(T // tm, N // tn),
        in_specs=[pl.BlockSpec((tm, K), lambda i, j: (i, 0)),
                  pl.BlockSpec((K, tn), lambda i, j: (0, j)),
                  pl.BlockSpec((tm, tn), lambda i, j: (i, j))],
        out_specs=pl.BlockSpec((tm, tn), lambda i, j: (i, j)),
        out_shape=jax.ShapeDtypeStruct((T, N), F32),
        compiler_params=_params("parallel", "arbitrary"),
        name="matmul_residual",
    )(y, w, h)


def _ple_body(h_ref, d_ref, g_ref, wg_ref, p_ref, wp_ref, fn_ref, o_ref, *, final):
    h = h_ref[...] + d_ref[...]
    xn = _rms(h, g_ref[...]).astype(BF16)
    gate = jax.nn.sigmoid(jnp.dot(xn, wg_ref[...], preferred_element_type=F32))
    emb = jnp.dot(p_ref[...].astype(BF16), wp_ref[...], preferred_element_type=F32)
    out = h + gate * emb
    if final:
        out = _rms(out, fn_ref[...])
    o_ref[...] = out


def ple_gate(h, peer_out, gain, wg, p, wp, final_gain, final, tm=256):
    T, D = h.shape
    PD = p.shape[1]
    tm = _tile(T, tm)
    return pl.pallas_call(
        functools.partial(_ple_body, final=final),
        grid=(T // tm,),
        in_specs=[pl.BlockSpec((tm, D), lambda i: (i, 0)),
                  pl.BlockSpec((tm, D), lambda i: (i, 0)),
                  pl.BlockSpec((1, D), lambda i: (0, 0)),
                  pl.BlockSpec((D, D), lambda i: (0, 0)),
                  pl.BlockSpec((tm, PD), lambda i: (i, 0)),
                  pl.BlockSpec((PD, D), lambda i: (0, 0)),
                  pl.BlockSpec((1, D), lambda i: (0, 0))],
        out_specs=pl.BlockSpec((tm, D), lambda i: (i, 0)),
        out_shape=jax.ShapeDtypeStruct((T, D), F32),
        compiler_params=_params("parallel"),
        name="ple_gate",
    )(h, peer_out, gain.reshape(1, D), wg, p, wp, final_gain.reshape(1, D))


def _soft_cap(t):
    return GATE_CAP * jnp.tanh(t / GATE_CAP)


def _log_sigmoid(t):
    return jnp.minimum(t, 0.0) - jnp.log(1.0 + jnp.exp(-jnp.abs(t)))


def _split3(x):
    hi = x.astype(BF16)
    r = x - hi.astype(F32)
    mid = r.astype(BF16)
    lo = (r - mid.astype(F32)).astype(BF16)
    return hi, mid, lo


def _mlstm_body(q_ref, k_ref, v_ref, og_ref, gc_ref, gr_ref, bc_ref, br_ref, hn_ref, o_ref,
                c_ref, n_ref, m_ref, *, L, DK):
    @pl.when(pl.program_id(1) == 0)
    def _():
        c_ref[...] = jnp.zeros_like(c_ref)
        n_ref[...] = jnp.zeros_like(n_ref)
        m_ref[...] = jnp.zeros_like(m_ref)

    q = (q_ref[...].astype(F32) * (DK ** -0.5)).astype(BF16)
    k = k_ref[...]
    v = v_ref[...]

    gc = gc_ref[...] + bc_ref[...]
    gr = gr_ref[...] + br_ref[...]
    li_c = _soft_cap(gc[:, 0:1])
    lf_c = _log_sigmoid(_soft_cap(gc[:, 1:2]))
    li_r = _soft_cap(gr[0:1, :])
    lf_r = _log_sigmoid(_soft_cap(gr[1:2, :]))

    row = lax.broadcasted_iota(jnp.int32, (L, L), 0)
    col = lax.broadcasted_iota(jnp.int32, (L, L), 1)
    causal = row >= col
    tri_l = causal.astype(BF16)
    tri_u = (row <= col).astype(BF16)
    b_c = sum(jnp.dot(tri_l, part, preferred_element_type=F32)
              for part in _split3(jnp.broadcast_to(lf_c, (L, V7X_LANES))))[:, 0:1]
    b_r = sum(jnp.dot(part, tri_u, preferred_element_type=F32)
              for part in _split3(jnp.broadcast_to(lf_r, (V7X_SUBLANES, L))))[0:1, :]

    m_prev = m_ref[0:1, 0:1]
    dmat = jnp.where(causal, b_c - b_r + li_r, -jnp.inf)
    inter = b_c + m_prev
    m_t = jnp.maximum(inter, jnp.max(dmat, axis=-1, keepdims=True))
    w_intra = jnp.exp(dmat - m_t)
    w_inter = jnp.exp(inter - m_t)

    s = lax.dot_general(q, k, NT_DIMS, preferred_element_type=F32) * w_intra
    c_old = c_ref[...]
    n_old = n_ref[...]
    num = (w_inter * jnp.dot(q, c_old.astype(BF16), preferred_element_type=F32)
           + jnp.dot(s.astype(BF16), v, preferred_element_type=F32))
    qn = lax.dot_general(q, jnp.broadcast_to(n_old, (V7X_LANES, DK)).astype(BF16), NT_DIMS,
                         preferred_element_type=F32)[:, 0:1]
    den = w_inter * qn + jnp.sum(s, axis=-1, keepdims=True)
    hh = num / jnp.maximum(jnp.abs(den), jnp.exp(-m_t))

    m_new = m_t[L - 1:L, :]
    b_last = b_c[L - 1:L, :]
    w_state = jnp.exp(b_last - b_c + li_c - m_new)
    decay = jnp.exp(b_last + m_prev - m_new)
    kw = k.astype(F32) * w_state
    c_ref[...] = decay * c_old + lax.dot_general(kw.astype(BF16), v, TN_DIMS, preferred_element_type=F32)
    n_ref[...] = decay * n_old + jnp.sum(kw, axis=0, keepdims=True)
    m_ref[...] = jnp.broadcast_to(m_new, m_ref.shape)

    ms = jnp.mean(hh * hh, axis=-1, keepdims=True)
    y = hh * lax.rsqrt(ms + EPS) * hn_ref[...] * jax.nn.sigmoid(og_ref[...].astype(F32))
    o_ref[...] = y.astype(o_ref.dtype)


def mlstm_scan(proj, gates, gate_bias, head_norm, B, S, chunk=256):
    T = proj.shape[0]
    H = M_HEADS
    HDV = head_norm.shape[0]
    DV = HDV // H
    DK = (proj.shape[1] - 2 * HDV) // (2 * H)
    L = _tile(S, chunk)
    NC = S // L
    assert DV % DK == 0 or DK % DV == 0
    k_off = (H * DK) // DK
    v_off = (2 * H * DK) // DV
    og_off = (2 * H * DK + H * DV) // DV
    g2 = gates.reshape(T, 2, H)
    g_col = g2.transpose(2, 0, 1)
    g_row = g2.transpose(2, 1, 0)
    b_col = gate_bias.T.reshape(H, 1, 2)
    b_row = gate_bias.T.reshape(H, 2, 1)

    def tok(bh, c):
        return (bh // H) * NC + c

    return pl.pallas_call(
        functools.partial(_mlstm_body, L=L, DK=DK),
        grid=(B * H, NC),
        in_specs=[pl.BlockSpec((L, DK), lambda bh, c: (tok(bh, c), bh % H)),
                  pl.BlockSpec((L, DK), lambda bh, c: (tok(bh, c), k_off + bh % H)),
                  pl.BlockSpec((L, DV), lambda bh, c: (tok(bh, c), v_off + bh % H)),
                  pl.BlockSpec((L, DV), lambda bh, c: (tok(bh, c), og_off + bh % H)),
                  pl.BlockSpec((None, L, 2), lambda bh, c: (bh % H, tok(bh, c), 0)),
                  pl.BlockSpec((None, 2, L), lambda bh, c: (bh % H, 0, tok(bh, c))),
                  pl.BlockSpec((None, 1, 2), lambda bh, c: (bh % H, 0, 0)),
                  pl.BlockSpec((None, 2, 1), lambda bh, c: (bh % H, 0, 0)),
                  pl.BlockSpec((1, DV), lambda bh, c: (0, bh % H))],
        out_specs=pl.BlockSpec((L, DV), lambda bh, c: (tok(bh, c), bh % H)),
        out_shape=jax.ShapeDtypeStruct((T, HDV), BF16),
        scratch_shapes=[pltpu.VMEM((DK, DV), F32),
                        pltpu.VMEM((1, DK), F32),
                        pltpu.VMEM((V7X_SUBLANES, V7X_LANES), F32)],
        compiler_params=_params("parallel", "arbitrary"),
        name="mlstm_scan",
    )(proj, proj, proj, proj, g_col, g_row, b_col, b_row, head_norm.reshape(1, HDV))


def _swa_body(sink_ref, q_ref, kp_ref, kc_ref, o_ref, *, BLK, KVH, GROUP, HD):
    first = pl.program_id(1) == 0
    ri = lax.broadcasted_iota(jnp.int32, (BLK, BLK), 0)
    ci = lax.broadcasted_iota(jnp.int32, (BLK, BLK), 1)
    from_cur = ci <= ri
    prev_bias = jnp.where(jnp.logical_and(first, ci > ri), -jnp.inf, 0.0)
    lo = lax.broadcasted_iota(jnp.int32, (1, 2 * HD), 1) < HD
    scale = HD ** -0.5

    def head_pair_operands(kv, pair):
        xk = kv[:, pair * 2 * HD:(pair + 1) * 2 * HD]
        xv = kv[:, (KVH + pair * 2) * HD:(KVH + pair * 2 + 2) * HD]
        xk_r = pltpu.roll(xk, HD, 1)
        xv_r = pltpu.roll(xv, HD, 1)
        head0 = (jnp.where(lo, xk, xk_r), jnp.where(lo, xv, 0.0), jnp.where(lo, 0.0, xv_r))
        head1 = (jnp.where(lo, xk_r, xk), jnp.where(lo, xv_r, 0.0), jnp.where(lo, 0.0, xv))
        return [tuple(t.astype(BF16) for t in head) for head in (head0, head1)]

    kv = jnp.concatenate([kp_ref[...], kc_ref[...]], axis=0).astype(F32)
    for pair in range(KVH // 2):
        operands = head_pair_operands(kv, pair)
        for sub in range(2):
            kh = 2 * pair + sub
            k2, *v_halves = operands[sub]
            for gp in range(GROUP // 2):
                h0 = kh * GROUP + 2 * gp
                q2 = (q_ref[:, h0 * HD:(h0 + 2) * HD].astype(F32) * scale).astype(BF16)
                acc = None
                for e in range(2):
                    qm = jnp.where(lo if e == 0 else jnp.logical_not(lo), q2, jnp.zeros_like(q2))
                    s2 = lax.dot_general(qm, k2, NT_DIMS, preferred_element_type=F32)
                    s = jnp.where(from_cur, s2[:, BLK:], s2[:, :BLK] + prev_bias)
                    sink = sink_ref[h0 + e]
                    mx = jnp.maximum(jnp.max(s, axis=-1, keepdims=True), sink)
                    pr = jnp.exp(s - mx)
                    den = jnp.sum(pr, axis=-1, keepdims=True) + jnp.exp(sink - mx)
                    p2 = jnp.concatenate([jnp.where(from_cur, 0.0, pr), jnp.where(from_cur, pr, 0.0)], axis=1)
                    o = jnp.dot(p2.astype(BF16), v_halves[e], preferred_element_type=F32) / den
                    acc = o if acc is None else acc + o
                o_ref[:, h0 * HD:(h0 + 2) * HD] = acc.astype(o_ref.dtype)


def swa_attention(q, kv, sinks, B, S):
    T, QD = q.shape
    HD, GROUP, BLK = A_HEAD_DIM, A_GROUP, WINDOW
    KVH = kv.shape[1] // (2 * HD)
    assert QD == KVH * GROUP * HD and 2 * HD == V7X_LANES and KVH % 2 == 0 and GROUP % 2 == 0
    assert HD in (4, 16, 64, 256)
    NB = S // BLK
    return pl.pallas_call(
        functools.partial(_swa_body, BLK=BLK, KVH=KVH, GROUP=GROUP, HD=HD),
        grid=(B, NB),
        in_specs=[pl.BlockSpec(memory_space=pltpu.SMEM),
                  pl.BlockSpec((BLK, QD), lambda b, n: (b * NB + n, 0)),
                  pl.BlockSpec((BLK, 2 * KVH * HD), lambda b, n: (b * NB + jnp.maximum(n - 1, 0), 0)),
                  pl.BlockSpec((BLK, 2 * KVH * HD), lambda b, n: (b * NB + n, 0))],
        out_specs=pl.BlockSpec((BLK, QD), lambda b, n: (b * NB + n, 0)),
        out_shape=jax.ShapeDtypeStruct((T, QD), BF16),
        compiler_params=_params("parallel", "arbitrary"),
        name="swa_attention",
    )(sinks, q, kv, kv)


KEY_BIG = 1 << 30


def _topk_rows(problems, k):
    tm = problems[0][0].shape[1]
    slot = lax.broadcasted_iota(jnp.int32, (k, tm), 0)
    state = [[vals, keys, [], [], jnp.zeros((k, tm), vals.dtype), jnp.zeros((k, tm), keys.dtype)]
             for vals, keys in problems]
    for it in range(k):
        for st in state:
            vals, keys = st[0], st[1]
            m = jnp.max(vals, axis=0, keepdims=True)
            kmin = jnp.min(jnp.where(vals == m, keys, KEY_BIG), axis=0, keepdims=True)
            st[0] = jnp.where(keys == kmin, -jnp.inf, vals)
            st[2].append(m)
            st[3].append(kmin)
            st[4] = jnp.where(slot == it, m, st[4])
            st[5] = jnp.where(slot == it, kmin, st[5])
    return [tuple(st[2:]) for st in state]


def _oddeven_merge_sort(n):
    pairs = []
    p = 1
    while p < n:
        k = p
        while k >= 1:
            for j in range(k % p, n - k, 2 * k):
                for i in range(min(k, n - j - k)):
                    if (i + j) // (2 * p) == (i + j + k) // (2 * p):
                        pairs.append((i + j, i + j + k))
            k //= 2
        p *= 2
    return pairs


def _topk_rows_distinct(problems, k):
    sub = V7X_SUBLANES
    tl = problems[0][0].shape[1]
    slot = lax.broadcasted_iota(jnp.int32, (k, tl), 0)
    state = []
    for vals, keys in problems:
        n = vals.shape[0] // sub
        v = [vals[j * sub:(j + 1) * sub] for j in range(n)]
        q = [keys[j * sub:(j + 1) * sub] for j in range(n)]
        size = 1
        while size < n:
            size *= 2
        for i, j in _oddeven_merge_sort(size):
            if j < n:
                first = v[i] >= v[j]
                v[i], v[j] = jnp.maximum(v[i], v[j]), jnp.minimum(v[i], v[j])
                q[i], q[j] = jnp.where(first, q[i], q[j]), jnp.where(first, q[j], q[i])
        state.append(dict(v=v, q=q, n=n, rows_v=[], rows_k=[], arr_v=jnp.zeros((k, tl), vals.dtype),
                          arr_k=jnp.zeros((k, tl), keys.dtype), tie=jnp.zeros((1, tl), jnp.bool_),
                          pops=jnp.zeros((sub, tl), F32)))
    for it in range(k):
        for st in state:
            v, q, n = st["v"], st["q"], st["n"]
            m = jnp.max(v[0], axis=0, keepdims=True)
            pop = v[0] == m
            key = jnp.max(jnp.where(pop, q[0], -1), axis=0, keepdims=True)
            st["pops"] = st["pops"] + pop.astype(F32)
            if st["rows_v"]:
                st["tie"] = st["tie"] | (m == st["rows_v"][-1])
            depth = min(n, k - it)
            for j in range(depth):
                v[j] = jnp.where(pop, v[j + 1] if j + 1 < n else -jnp.inf, v[j])
                if j + 1 < n:
                    q[j] = jnp.where(pop, q[j + 1], q[j])
            st["rows_v"].append(m)
            st["rows_k"].append(key)
            st["arr_v"] = jnp.where(slot == it, m, st["arr_v"])
            st["arr_k"] = jnp.where(slot == it, key, st["arr_k"])
    out = []
    for st in state:
        tie = (st["tie"] | (jnp.max(st["v"][0], axis=0, keepdims=True) == st["rows_v"][-1])
               | (jnp.sum(st["pops"], axis=0, keepdims=True) != float(k)))
        out.append(((st["rows_v"], st["rows_k"], st["arr_v"], st["arr_k"]), tie))
    return out


def _candidates(top1, top2, k):
    r1_v, r1_i, v1, i1 = top1
    r2_v, r2_i, v2, i2 = top2
    tm = v1.shape[1]
    sub = V7X_SUBLANES
    vals, keys = [], []

    def pack(code, e1, e2):
        return (code << 14) | (e1 << 7) | e2

    b_iota = lax.broadcasted_iota(jnp.int32, (k, tm), 0)
    vals.append(r1_v[0] + v2)
    keys.append(pack(b_iota, r1_i[0], i2))
    a = 1
    while a < k and k // (a + 1) >= 2:
        nb = k // (a + 1)
        rows = -(-nb // sub) * sub
        bi = lax.broadcasted_iota(jnp.int32, (rows, tm), 0)
        vals.append(jnp.where(bi < nb, r1_v[a] + v2[0:rows], -jnp.inf))
        keys.append(pack(a * k + bi, r1_i[a], i2[0:rows]))
        a += 1
    if a < k:
        a0 = (a // sub) * sub
        ai = lax.broadcasted_iota(jnp.int32, (k - a0, tm), 0) + a0
        vals.append(jnp.where(ai >= a, v1[a0:k] + r2_v[0], -jnp.inf))
        keys.append(pack(ai * k, i1[a0:k], r2_i[0]))
    return jnp.concatenate(vals, axis=0), jnp.concatenate(keys, axis=0)


def _peer_qproj_body(x_ref, g_ref, w_ref, xn_ref, q_ref):
    @pl.when(pl.program_id(1) == 0)
    def _():
        xn_ref[...] = _rms(x_ref[...], g_ref[...]).astype(BF16)

    q_ref[...] = jnp.dot(xn_ref[...], w_ref[...], preferred_element_type=F32).astype(q_ref.dtype)


def peer_qproj(h, gain, wq, tm=1024, tn=1024):
    T, D = h.shape
    N = wq.shape[1]
    tm, tn = _tile(T, tm), _tile(N, tn)
    return pl.pallas_call(
        _peer_qproj_body,
        grid=(T // tm, N // tn),
        in_specs=[pl.BlockSpec((tm, D), lambda i, j: (i, 0)),
                  pl.BlockSpec((1, D), lambda i, j: (0, 0)),
                  pl.BlockSpec((D, tn), lambda i, j: (0, j))],
        out_specs=[pl.BlockSpec((tm, D), lambda i, j: (i, 0)),
                   pl.BlockSpec((tm, tn), lambda i, j: (i, j))],
        out_shape=[jax.ShapeDtypeStruct((T, D), BF16), jax.ShapeDtypeStruct((T, N), BF16)],
        compiler_params=_params("parallel", "arbitrary"),
        name="peer_qproj",
    )(h, gain.reshape(1, D), wq)


TOPK_TOKENS = 256


def _key_scores(q_ref, k1, k2, s1_ref, s2_ref):
    half = k1.shape[1]
    s1_ref[...] = lax.dot_general(k1, q_ref[:, 0:half], NT_DIMS, preferred_element_type=F32)
    s2_ref[...] = lax.dot_general(k2, q_ref[:, half:2 * half], NT_DIMS, preferred_element_type=F32)


def _select_from_scores(s1_ref, s2_ref, gate_ref, e1_ref, e2_ref, *, K):
    NK = s1_ref.shape[0]
    tl = V7X_LANES
    key_iota = lax.broadcasted_iota(jnp.int32, (NK, tl), 0)
    groups = [slice(c * tl, (c + 1) * tl) for c in range(s1_ref.shape[1] // tl)]

    def emit(lanes, top):
        rows_s, _, top_s, top_key = top
        ex = jnp.exp(top_s - rows_s[0])
        gate_ref[:, lanes] = ex / jnp.sum(ex, axis=0, keepdims=True)
        e1_ref[:, lanes] = ((top_key >> 7) & 127).astype(F32)
        e2_ref[:, lanes] = (top_key & 127).astype(F32)

    stage1 = _topk_rows_distinct(
        [(ref[:, lanes], key_iota) for lanes in groups for ref in (s1_ref, s2_ref)], K)
    stage2 = _topk_rows_distinct(
        [_candidates(stage1[2 * c][0], stage1[2 * c + 1][0], K) for c in range(len(groups))], K)
    for c, lanes in enumerate(groups):
        emit(lanes, stage2[c][0])

    for c, lanes in enumerate(groups):
        any_tie = jnp.max((stage1[2 * c][1] | stage1[2 * c + 1][1] | stage2[c][1]).astype(jnp.int32))

        @pl.when(any_tie > 0)
        def _():
            top1, top2 = _topk_rows([(s1_ref[:, lanes], key_iota), (s2_ref[:, lanes], key_iota)], K)
            emit(lanes, _topk_rows([_candidates(top1, top2, K)], K)[0])


def _peer_topk_body(q_ref, k1_ref, k2_ref, gate_ref, e1_ref, e2_ref, s1_ref, s2_ref, *, K):
    hd = pl.program_id(1)
    _key_scores(q_ref, k1_ref[hd], k2_ref[hd], s1_ref, s2_ref)
    _select_from_scores(s1_ref, s2_ref, gate_ref, e1_ref, e2_ref, K=K)


def peer_topk(q, k1, k2):
    T = q.shape[0]
    PH, NK, HALF = k1.shape
    K = P_TOPK
    assert NK == V7X_LANES and HALF == V7X_LANES and q.shape[1] == PH * 2 * HALF
    tq = _tile(T, TOPK_TOKENS)
    k_spec = pl.BlockSpec((PH, NK, HALF), lambda c, hd: (0, 0, 0))
    sel_spec = pl.BlockSpec((K, tq), lambda c, hd: (hd, c))
    sel = jax.ShapeDtypeStruct((PH * K, T), F32)
    return pl.pallas_call(
        functools.partial(_peer_topk_body, K=K),
        grid=(T // tq, PH),
        in_specs=[pl.BlockSpec((tq, 2 * HALF), lambda c, hd: (c, hd)), k_spec, k_spec],
        out_specs=[sel_spec] * 3, out_shape=[sel] * 3,
        scratch_shapes=[pltpu.VMEM((NK, tq), F32)] * 2,
        compiler_params=_params("parallel", "arbitrary"),
        name="peer_topk",
    )(q, k1, k2)


SCATTER_PITCH_PAD = V7X_SUBLANES // 2
BF16_SUBLANES = 2 * V7X_SUBLANES


def _peer_scatter_body(gate_ref, e1_ref, e2_ref, w_ref, scr_ref, gt_ref, e1t_ref, *, NK):
    tmb, HK = gt_ref.shape
    pad = SCATTER_PITCH_PAD
    pitch = NK + pad
    gt_ref[...] = gate_ref[...].T
    e1t_ref[...] = e1_ref[...].T
    key_iota = lax.broadcasted_iota(jnp.int32, (NK, HK), 0).astype(F32)
    tall = NK + BF16_SUBLANES
    key_iota_odd = (lax.broadcasted_iota(jnp.int32, (tall, HK), 0) - pad).astype(F32)
    lane_iota = lax.broadcasted_iota(jnp.int32, (HK, NK), 1).astype(F32).astype(BF16)
    e2_cols = e2_ref[...].astype(BF16)
    one, zero = jnp.ones((), BF16), jnp.zeros((), BF16)
    for t in range(tmb):
        odd = t % 2
        g_row = gt_ref[t:t + 1, :]
        sel1 = jnp.where(e1t_ref[t:t + 1, :] == (key_iota_odd if odd else key_iota), g_row, 0.0)
        sel2t = jnp.where(e2_cols[:, t:t + 1] == lane_iota, one, zero)
        w_t = jnp.dot(sel1.astype(BF16), sel2t, preferred_element_type=F32)
        rows = NK + 2 * pad * odd
        start = t * pitch - pad * odd
        scr_ref[start:start + rows, :] = w_t[0:rows]
    for j in range(NK):
        w_ref[j] = scr_ref[pl.ds(j, tmb, stride=pitch), :].astype(w_ref.dtype)


def peer_scatter(gate, e1, e2, NK, tmb=128):
    HK, T = gate.shape
    tmb = _tile(T, tmb)
    assert tmb % 2 == 0
    spec = pl.BlockSpec((HK, tmb), lambda i: (0, i))
    return pl.pallas_call(
        functools.partial(_peer_scatter_body, NK=NK),
        grid=(T // tmb,),
        in_specs=[spec, spec, spec],
        out_specs=pl.BlockSpec((NK, tmb, NK), lambda i: (0, i, 0)),
        out_shape=jax.ShapeDtypeStruct((NK, T, NK), BF16),
        scratch_shapes=[pltpu.VMEM((tmb * (NK + SCATTER_PITCH_PAD), NK), F32),
                        pltpu.VMEM((tmb, HK), F32),
                        pltpu.VMEM((tmb, HK), F32)],
        compiler_params=_params("parallel"),
        name="peer_scatter",
    )(gate, e1, e2)


def _gelu(x):
    return 0.5 * x * (1.0 + lax.erf(x * (2.0 ** -0.5)))


def _peer_dense_body(xn_ref, u_ref, v_ref, w_ref, o_ref, *, NK):
    @pl.when(pl.program_id(1) == 0)
    def _():
        o_ref[...] = jnp.zeros_like(o_ref)

    act = lax.dot_general(xn_ref[...], u_ref[...], NT_DIMS, preferred_element_type=F32)
    coef = jnp.concatenate(
        [(w_ref[c].astype(F32) * _gelu(act[:, c * NK:(c + 1) * NK])).astype(BF16)
         for c in range(w_ref.shape[0])], axis=1)
    o_ref[...] += jnp.dot(coef, v_ref[...], preferred_element_type=F32)


def peer_dense(xn, u_all, v_all, layer, w, tm=1024, te=1024):
    T, D = xn.shape
    E = u_all.shape[1]
    NK = w.shape[0]
    tm, te = _tile(T, tm), _tile(E, te)
    assert te % NK == 0 and E == NK * NK
    return pl.pallas_call(
        functools.partial(_peer_dense_body, NK=NK),
        grid=(T // tm, E // te),
        in_specs=[pl.BlockSpec((tm, D), lambda i, j: (i, 0)),
                  pl.BlockSpec((None, te, D), lambda i, j: (layer, j, 0)),
                  pl.BlockSpec((None, te, D), lambda i, j: (layer, j, 0)),
                  pl.BlockSpec((te // NK, tm, NK), lambda i, j: (j, i, 0))],
        out_specs=pl.BlockSpec((tm, D), lambda i, j: (i, 0)),
        out_shape=jax.ShapeDtypeStruct((T, D), F32),
        compiler_params=pltpu.CompilerParams(dimension_semantics=("parallel", "arbitrary"),
                                             vmem_limit_bytes=VMEM_LIMIT_LARGE),
        name="peer_dense",
    )(xn, u_all, v_all, w)


def peer_layer(h, gain, wq, k1, k2, u_all, v_all, layer):
    xn, q = peer_qproj(h, gain, wq)
    gate, e1, e2 = peer_topk(q, k1, k2)
    w = peer_scatter(gate, e1, e2, k1.shape[1])
    return peer_dense(xn, u_all, v_all, layer, w)


def kernel(x, p, a_norm, a_w_in, a_gate_bias, a_head_norm, a_w_out, kv_norm, w_kv, b_norm, b_w_q, b_sinks,
           b_w_out, c_norm, peer_w_q, peer_k1, peer_k2, peer_u, peer_v, ple_norm, ple_w_gate, ple_w_proj,
           final_norm):
    B, S, D = x.shape
    T = B * S
    depth = p.shape[0]
    n_a = a_norm.shape[0]
    bf = lambda t: t.astype(BF16)
    h = x.reshape(T, D)
    u_all, v_all = bf(peer_u), bf(peer_v)
    for i in range(depth):
        if i < n_a:
            w_in = a_w_in[i]
            n_main = w_in.shape[1] - 2 * M_HEADS
            w_gates = jnp.pad(w_in[:, n_main:], ((0, 0), (0, V7X_LANES - 2 * M_HEADS)))
            proj = rms_matmul(h, a_norm[i], bf(w_in[:, :n_main]), BF16, tm=1024, tn=1024)
            gates = rms_matmul(h, a_norm[i], bf(w_gates), F32)[:, :2 * M_HEADS]
            y = mlstm_scan(proj, gates, a_gate_bias[i], a_head_norm[i], B, S)
            h = matmul_residual(y, bf(a_w_out[i]), h)
        else:
            j = i - n_a
            if j == 0:
                kv = rms_matmul(h, kv_norm, bf(w_kv), BF16)
            q = rms_matmul(h, b_norm[j], bf(b_w_q[j]), BF16, tm=1024, tn=1024)
            o = swa_attention(q, kv, b_sinks[j], B, S)
            h = matmul_residual(o, bf(b_w_out[j]), h)
        peer_out = peer_layer(h, c_norm[i], bf(peer_w_q[i]), bf(peer_k1[i]), bf(peer_k2[i]), u_all, v_all, i)
        h = ple_gate(h, peer_out, ple_norm[i], bf(ple_w_gate[i]), p[i].reshape(T, -1), bf(ple_w_proj[i]),
                     final_norm, final=(i == depth - 1))
    return h.reshape(B, S, D)
```

```python
import functools

import jax
import jax.numpy as jnp
from jax import lax
from jax.experimental import pallas as pl
from jax.experimental.pallas import tpu as pltpu

F32 = jnp.float32
BF16 = jnp.bfloat16

EPS = 1e-6
GATE_CAP = 15.0

M_HEADS = 4
A_HEAD_DIM = 64
A_GROUP = 8
WINDOW = 128
P_HEADS = 8
P_NKEYS = 128
P_TOPK = 16

V7X_LANES = 128
V7X_SUBLANES = 8
V7X_VMEM_BYTES = 64 * 1024 * 1024
VMEM_LIMIT = (V7X_VMEM_BYTES * 3) // 4
VMEM_LIMIT_LARGE = (V7X_VMEM_BYTES * 7) // 8

NT_DIMS = (((1,), (1,)), ((), ()))
TN_DIMS = (((0,), (0,)), ((), ()))


def _params(*sem):
    return pltpu.CompilerParams(dimension_semantics=sem, vmem_limit_bytes=VMEM_LIMIT)


def _rms(x, gain):
    ms = jnp.mean(x * x, axis=-1, keepdims=True)
    return x * lax.rsqrt(ms + EPS) * gain


def _tile(n, pref):
    t = min(n, pref)
    assert n % t == 0, (n, pref)
    return t


def _rms_matmul_body(x_ref, g_ref, w_ref, o_ref, xn_ref):
    @pl.when(pl.program_id(1) == 0)
    def _():
        xn_ref[...] = _rms(x_ref[...], g_ref[...]).astype(BF16)

    o_ref[...] = jnp.dot(xn_ref[...], w_ref[...], preferred_element_type=F32).astype(o_ref.dtype)


def rms_matmul(x, gain, w, out_dtype, tm=512, tn=512):
    T, D = x.shape
    N = w.shape[1]
    tm, tn = _tile(T, tm), _tile(N, tn)
    return pl.pallas_call(
        _rms_matmul_body,
        grid=(T // tm, N // tn),
        in_specs=[pl.BlockSpec((tm, D), lambda i, j: (i, 0)),
                  pl.BlockSpec((1, D), lambda i, j: (0, 0)),
                  pl.BlockSpec((D, tn), lambda i, j: (0, j))],
        out_specs=pl.BlockSpec((tm, tn), lambda i, j: (i, j)),
        out_shape=jax.ShapeDtypeStruct((T, N), out_dtype),
        scratch_shapes=[pltpu.VMEM((tm, D), BF16)],
        compiler_params=_params("parallel", "arbitrary"),
        name="rms_matmul",
    )(x, gain.reshape(1, D), w)


def _matmul_residual_body(y_ref, w_ref, h_ref, o_ref):
    o_ref[...] = h_ref[...] + jnp.dot(y_ref[...], w_ref[...], preferred_element_type=F32)


def matmul_residual(y, w, h, tm=512, tn=2048):
    T, K = y.shape
    N = w.shape[1]
    tm, tn = _tile(T, tm), _tile(N, tn)
    return pl.pallas_call(
        _matmul_residual_body,
        grid=(T // tm, N // tn),
        in_specs=[pl.BlockSpec((tm, K), lambda i, j: (i, 0)),
                  pl.BlockSpec((K, tn), lambda i, j: (0, j)),
                  pl.BlockSpec((tm, tn), lambda i, j: (i, j))],
        out_specs=pl.BlockSpec((tm, tn), lambda i, j: (i, j)),
        out_shape=jax.ShapeDtypeStruct((T, N), F32),
        compiler_params=_params("parallel", "arbitrary"),
        name="matmul_residual",
    )(y, w, h)


def _ple_body(h_ref, d_ref, g_ref, wg_ref, p_ref, wp_ref, fn_ref, o_ref, *, final):
    h = h_ref[...] + d_ref[...]
    xn = _rms(h, g_ref[...]).astype(BF16)
    gate = jax.nn.sigmoid(jnp.dot(xn, wg_ref[...], preferred_element_type=F32))
    emb = jnp.dot(p_ref[...].astype(BF16), wp_ref[...], preferred_element_type=F32)
    out = h + gate * emb
    if final:
        out = _rms(out, fn_ref[...])
    o_ref[...] = out


def ple_gate(h, peer_out, gain, wg, p, wp, final_gain, final, tm=256):
    T, D = h.shape
    PD = p.shape[1]
    tm = _tile(T, tm)
    return pl.pallas_call(
        functools.partial(_ple_body, final=final),
        grid=(T // tm,),
        in_specs=[pl.BlockSpec((tm, D), lambda i: (i, 0)),
                  pl.BlockSpec((tm, D), lambda i: (i, 0)),
                  pl.BlockSpec((1, D), lambda i: (0, 0)),
                  pl.BlockSpec((D, D), lambda i: (0, 0)),
                  pl.BlockSpec((tm, PD), lambda i: (i, 0)),
                  pl.BlockSpec((PD, D), lambda i: (0, 0)),
                  pl.BlockSpec((1, D), lambda i: (0, 0))],
        out_specs=pl.BlockSpec((tm, D), lambda i: (i, 0)),
        out_shape=jax.ShapeDtypeStruct((T, D), F32),
        compiler_params=_params("parallel"),
        name="ple_gate",
    )(h, peer_out, gain.reshape(1, D), wg, p, wp, final_gain.reshape(1, D))


def _soft_cap(t):
    return GATE_CAP * jnp.tanh(t / GATE_CAP)


def _log_sigmoid(t):
    return jnp.minimum(t, 0.0) - jnp.log(1.0 + jnp.exp(-jnp.abs(t)))


def _split3(x):
    hi = x.astype(BF16)
    r = x - hi.astype(F32)
    mid = r.astype(BF16)
    lo = (r - mid.astype(F32)).astype(BF16)
    return hi, mid, lo


MLSTM_HEADS_PER_STEP = 4


def _mlstm_body(q_ref, k_ref, v_ref, og_ref, gc_ref, gr_ref, bc_ref, br_ref, hn_ref, o_ref, *state, L, DK, HP):
    @pl.when(pl.program_id(1) == 0)
    def _():
        for ref in state:
            ref[...] = jnp.zeros_like(ref)

    DV = o_ref.shape[1] // HP
    row = lax.broadcasted_iota(jnp.int32, (L, L), 0)
    col = lax.broadcasted_iota(jnp.int32, (L, L), 1)
    causal = row >= col
    tri_l = causal.astype(BF16)
    tri_u = (row <= col).astype(BF16)
    for i in range(HP):
        dk, dv = slice(i * DK, (i + 1) * DK), slice(i * DV, (i + 1) * DV)
        _mlstm_head(q_ref.at[:, dk], k_ref.at[:, dk], v_ref.at[:, dv], og_ref.at[:, dv], gc_ref.at[i], gr_ref.at[i],
                    bc_ref.at[i], br_ref.at[i], hn_ref.at[:, dv], o_ref.at[:, dv], *state[3 * i:3 * i + 3],
                    causal, tri_l, tri_u, L=L, DK=DK)


def _mlstm_head(q_ref, k_ref, v_ref, og_ref, gc_ref, gr_ref, bc_ref, br_ref, hn_ref, o_ref,
                c_ref, n_ref, m_ref, causal, tri_l, tri_u, *, L, DK):
    q = (q_ref[...].astype(F32) * (DK ** -0.5)).astype(BF16)
    k = k_ref[...]
    v = v_ref[...]

    gc = gc_ref[...] + bc_ref[...]
    gr = gr_ref[...] + br_ref[...]
    li_c = _soft_cap(gc[:, 0:1])
    lf_c = _log_sigmoid(_soft_cap(gc[:, 1:2]))
    li_r = _soft_cap(gr[0:1, :])
    lf_r = _log_sigmoid(_soft_cap(gr[1:2, :]))

    b_c = sum(jnp.dot(tri_l, part, preferred_element_type=F32)
              for part in _split3(jnp.broadcast_to(lf_c, (L, V7X_LANES))))[:, 0:1]
    b_r = sum(jnp.dot(part, tri_u, preferred_element_type=F32)
              for part in _split3(jnp.broadcast_to(lf_r, (V7X_SUBLANES, L))))[0:1, :]

    m_prev = m_ref[0:1, 0:1]
    dmat = jnp.where(causal, b_c - b_r + li_r, -jnp.inf)
    inter = b_c + m_prev
    m_t = jnp.maximum(inter, jnp.max(dmat, axis=-1, keepdims=True))
    w_intra = jnp.exp(dmat - m_t)
    w_inter = jnp.exp(inter - m_t)

    s = lax.dot_general(q, k, NT_DIMS, preferred_element_type=F32) * w_intra
    c_old = c_ref[...]
    n_old = n_ref[...]
    num = (w_inter * jnp.dot(q, c_old.astype(BF16), preferred_element_type=F32)
           + jnp.dot(s.astype(BF16), v, preferred_element_type=F32))
    qn = lax.dot_general(q, jnp.broadcast_to(n_old, (V7X_LANES, DK)).astype(BF16), NT_DIMS,
                         preferred_element_type=F32)[:, 0:1]
    den = w_inter * qn + jnp.sum(s, axis=-1, keepdims=True)
    hh = num / jnp.maximum(jnp.abs(den), jnp.exp(-m_t))

    m_new = m_t[L - 1:L, :]
    b_last = b_c[L - 1:L, :]
    w_state = jnp.exp(b_last - b_c + li_c - m_new)
    decay = jnp.exp(b_last + m_prev - m_new)
    kw = k.astype(F32) * w_state
    c_ref[...] = decay * c_old + lax.dot_general(kw.astype(BF16), v, TN_DIMS, preferred_element_type=F32)
    n_ref[...] = decay * n_old + jnp.sum(kw, axis=0, keepdims=True)
    m_ref[...] = jnp.broadcast_to(m_new, m_ref.shape)

    ms = jnp.mean(hh * hh, axis=-1, keepdims=True)
    y = hh * lax.rsqrt(ms + EPS) * hn_ref[...] * jax.nn.sigmoid(og_ref[...].astype(F32))
    o_ref[...] = y.astype(o_ref.dtype)


def mlstm_scan(proj, gates, gate_bias, head_norm, B, S, chunk=256):
    T = proj.shape[0]
    H = M_HEADS
    HDV = head_norm.shape[0]
    DV = HDV // H
    DK = (proj.shape[1] - 2 * HDV) // (2 * H)
    L = _tile(S, chunk)
    NC = S // L
    assert DV % DK == 0 or DK % DV == 0
    k_off = (H * DK) // DK
    v_off = (2 * H * DK) // DV
    og_off = (2 * H * DK + H * DV) // DV
    g2 = gates.reshape(T, 2, H)
    g_col = g2.transpose(2, 0, 1)
    g_row = g2.transpose(2, 1, 0)
    b_col = gate_bias.T.reshape(H, 1, 2)
    b_row = gate_bias.T.reshape(H, 2, 1)

    HP = MLSTM_HEADS_PER_STEP
    assert H % HP == 0 and k_off % HP == 0 and v_off % HP == 0 and og_off % HP == 0
    groups = H // HP

    def tok(g, c):
        return (g // groups) * NC + c

    return pl.pallas_call(
        functools.partial(_mlstm_body, L=L, DK=DK, HP=HP),
        grid=(B * groups, NC),
        in_specs=[pl.BlockSpec((L, HP * DK), lambda g, c: (tok(g, c), g % groups)),
                  pl.BlockSpec((L, HP * DK), lambda g, c: (tok(g, c), k_off // HP + g % groups)),
                  pl.BlockSpec((L, HP * DV), lambda g, c: (tok(g, c), v_off // HP + g % groups)),
                  pl.BlockSpec((L, HP * DV), lambda g, c: (tok(g, c), og_off // HP + g % groups)),
                  pl.BlockSpec((HP, L, 2), lambda g, c: (g % groups, tok(g, c), 0)),
                  pl.BlockSpec((HP, 2, L), lambda g, c: (g % groups, 0, tok(g, c))),
                  pl.BlockSpec((HP, 1, 2), lambda g, c: (g % groups, 0, 0)),
                  pl.BlockSpec((HP, 2, 1), lambda g, c: (g % groups, 0, 0)),
                  pl.BlockSpec((1, HP * DV), lambda g, c: (0, g % groups))],
        out_specs=pl.BlockSpec((L, HP * DV), lambda g, c: (tok(g, c), g % groups)),
        out_shape=jax.ShapeDtypeStruct((T, HDV), BF16),
        scratch_shapes=[pltpu.VMEM((DK, DV), F32),
                        pltpu.VMEM((1, DK), F32),
                        pltpu.VMEM((V7X_SUBLANES, V7X_LANES), F32)] * HP,
        compiler_params=_params("parallel", "arbitrary"),
        name="mlstm_scan",
    )(proj, proj, proj, proj, g_col, g_row, b_col, b_row, head_norm.reshape(1, HDV))


def _swa_body(sink_ref, q_ref, kp_ref, kc_ref, o_ref, *, BLK, KVH, GROUP, HD):
    first = pl.program_id(1) == 0
    ri = lax.broadcasted_iota(jnp.int32, (BLK, BLK), 0)
    ci = lax.broadcasted_iota(jnp.int32, (BLK, BLK), 1)
    from_cur = ci <= ri
    prev_bias = jnp.where(jnp.logical_and(first, ci > ri), -jnp.inf, 0.0)
    lo = lax.broadcasted_iota(jnp.int32, (1, 2 * HD), 1) < HD
    scale = HD ** -0.5

    def head_pair_operands(kv, pair):
        xk = kv[:, pair * 2 * HD:(pair + 1) * 2 * HD]
        xv = kv[:, (KVH + pair * 2) * HD:(KVH + pair * 2 + 2) * HD]
        xk_r = pltpu.roll(xk, HD, 1)
        xv_r = pltpu.roll(xv, HD, 1)
        head0 = (jnp.where(lo, xk, xk_r), jnp.where(lo, xv, 0.0), jnp.where(lo, 0.0, xv_r))
        head1 = (jnp.where(lo, xk_r, xk), jnp.where(lo, xv_r, 0.0), jnp.where(lo, 0.0, xv))
        return [tuple(t.astype(BF16) for t in head) for head in (head0, head1)]

    kv = jnp.concatenate([kp_ref[...], kc_ref[...]], axis=0).astype(F32)
    for pair in range(KVH // 2):
        operands = head_pair_operands(kv, pair)
        for sub in range(2):
            kh = 2 * pair + sub
            k2, *v_halves = operands[sub]
            for gp in range(GROUP // 2):
                h0 = kh * GROUP + 2 * gp
                q2 = (q_ref[:, h0 * HD:(h0 + 2) * HD].astype(F32) * scale).astype(BF16)
                acc = None
                for e in range(2):
                    qm = jnp.where(lo if e == 0 else jnp.logical_not(lo), q2, jnp.zeros_like(q2))
                    s2 = lax.dot_general(qm, k2, NT_DIMS, preferred_element_type=F32)
                    s = jnp.where(from_cur, s2[:, BLK:], s2[:, :BLK] + prev_bias)
                    sink = sink_ref[h0 + e]
                    mx = jnp.maximum(jnp.max(s, axis=-1, keepdims=True), sink)
                    pr = jnp.exp(s - mx)
                    den = jnp.sum(pr, axis=-1, keepdims=True) + jnp.exp(sink - mx)
                    p2 = jnp.concatenate([jnp.where(from_cur, 0.0, pr), jnp.where(from_cur, pr, 0.0)], axis=1)
                    o = jnp.dot(p2.astype(BF16), v_halves[e], preferred_element_type=F32) / den
                    acc = o if acc is None else acc + o
                o_ref[:, h0 * HD:(h0 + 2) * HD] = acc.astype(o_ref.dtype)


def swa_attention(q, kv, sinks, B, S):
    T, QD = q.shape
    HD, GROUP, BLK = A_HEAD_DIM, A_GROUP, WINDOW
    KVH = kv.shape[1] // (2 * HD)
    assert QD == KVH * GROUP * HD and 2 * HD == V7X_LANES and KVH % 2 == 0 and GROUP % 2 == 0
    assert HD in (4, 16, 64, 256)
    NB = S // BLK
    return pl.pallas_call(
        functools.partial(_swa_body, BLK=BLK, KVH=KVH, GROUP=GROUP, HD=HD),
        grid=(B, NB),
        in_specs=[pl.BlockSpec(memory_space=pltpu.SMEM),
                  pl.BlockSpec((BLK, QD), lambda b, n: (b * NB + n, 0)),
                  pl.BlockSpec((BLK, 2 * KVH * HD), lambda b, n: (b * NB + jnp.maximum(n - 1, 0), 0)),
                  pl.BlockSpec((BLK, 2 * KVH * HD), lambda b, n: (b * NB + n, 0))],
        out_specs=pl.BlockSpec((BLK, QD), lambda b, n: (b * NB + n, 0)),
        out_shape=jax.ShapeDtypeStruct((T, QD), BF16),
        compiler_params=_params("parallel", "arbitrary"),
        name="swa_attention",
    )(sinks, q, kv, kv)


KEY_BIG = 1 << 30


def _topk_rows(problems, k):
    tm = problems[0][0].shape[1]
    slot = lax.broadcasted_iota(jnp.int32, (k, tm), 0)
    state = [[vals, keys, [], [], jnp.zeros((k, tm), vals.dtype), jnp.zeros((k, tm), keys.dtype)]
             for vals, keys in problems]
    for it in range(k):
        for st in state:
            vals, keys = st[0], st[1]
            m = jnp.max(vals, axis=0, keepdims=True)
            kmin = jnp.min(jnp.where(vals == m, keys, KEY_BIG), axis=0, keepdims=True)
            st[0] = jnp.where(keys == kmin, -jnp.inf, vals)
            st[2].append(m)
            st[3].append(kmin)
            st[4] = jnp.where(slot == it, m, st[4])
            st[5] = jnp.where(slot == it, kmin, st[5])
    return [tuple(st[2:]) for st in state]


def _oddeven_merge_sort(n):
    pairs = []
    p = 1
    while p < n:
        k = p
        while k >= 1:
            for j in range(k % p, n - k, 2 * k):
                for i in range(min(k, n - j - k)):
                    if (i + j) // (2 * p) == (i + j + k) // (2 * p):
                        pairs.append((i + j, i + j + k))
            k //= 2
        p *= 2
    return pairs


def _topk_rows_distinct(problems, k):
    sub = V7X_SUBLANES
    tl = problems[0][0].shape[1]
    slot = lax.broadcasted_iota(jnp.int32, (k, tl), 0)
    state = []
    for vals, keys in problems:
        n = vals.shape[0] // sub
        v = [vals[j * sub:(j + 1) * sub] for j in range(n)]
        q = [keys[j * sub:(j + 1) * sub] for j in range(n)]
        size = 1
        while size < n:
            size *= 2
        for i, j in _oddeven_merge_sort(size):
            if j < n:
                first = v[i] >= v[j]
                v[i], v[j] = jnp.maximum(v[i], v[j]), jnp.minimum(v[i], v[j])
                q[i], q[j] = jnp.where(first, q[i], q[j]), jnp.where(first, q[j], q[i])
        state.append(dict(v=v, q=q, n=n, rows_v=[], rows_k=[], arr_v=jnp.zeros((k, tl), vals.dtype),
                          arr_k=jnp.zeros((k, tl), keys.dtype), tie=jnp.zeros((1, tl), jnp.bool_),
                          pops=jnp.zeros((sub, tl), F32)))
    for it in range(k):
        for st in state:
            v, q, n = st["v"], st["q"], st["n"]
            m = jnp.max(v[0], axis=0, keepdims=True)
            pop = v[0] == m
            key = jnp.max(jnp.where(pop, q[0], -1), axis=0, keepdims=True)
            st["pops"] = st["pops"] + pop.astype(F32)
            if st["rows_v"]:
                st["tie"] = st["tie"] | (m == st["rows_v"][-1])
            depth = min(n, k - it)
            for j in range(depth):
                v[j] = jnp.where(pop, v[j + 1] if j + 1 < n else -jnp.inf, v[j])
                if j + 1 < n:
                    q[j] = jnp.where(pop, q[j + 1], q[j])
            st["rows_v"].append(m)
            st["rows_k"].append(key)
            st["arr_v"] = jnp.where(slot == it, m, st["arr_v"])
            st["arr_k"] = jnp.where(slot == it, key, st["arr_k"])
    out = []
    for st in state:
        tie = (st["tie"] | (jnp.max(st["v"][0], axis=0, keepdims=True) == st["rows_v"][-1])
               | (jnp.sum(st["pops"], axis=0, keepdims=True) != float(k)))
        out.append(((st["rows_v"], st["rows_k"], st["arr_v"], st["arr_k"]), tie))
    return out


def _candidates(top1, top2, k):
    r1_v, r1_i, v1, i1 = top1
    r2_v, r2_i, v2, i2 = top2
    tm = v1.shape[1]
    sub = V7X_SUBLANES
    vals, keys = [], []

    def pack(code, e1, e2):
        return (code << 14) | (e1 << 7) | e2

    b_iota = lax.broadcasted_iota(jnp.int32, (k, tm), 0)
    vals.append(r1_v[0] + v2)
    keys.append(pack(b_iota, r1_i[0], i2))
    a = 1
    while a < k and k // (a + 1) >= 2:
        nb = k // (a + 1)
        rows = -(-nb // sub) * sub
        bi = lax.broadcasted_iota(jnp.int32, (rows, tm), 0)
        vals.append(jnp.where(bi < nb, r1_v[a] + v2[0:rows], -jnp.inf))
        keys.append(pack(a * k + bi, r1_i[a], i2[0:rows]))
        a += 1
    if a < k:
        a0 = (a // sub) * sub
        ai = lax.broadcasted_iota(jnp.int32, (k - a0, tm), 0) + a0
        vals.append(jnp.where(ai >= a, v1[a0:k] + r2_v[0], -jnp.inf))
        keys.append(pack(ai * k, i1[a0:k], r2_i[0]))
    return jnp.concatenate(vals, axis=0), jnp.concatenate(keys, axis=0)


def _peer_qproj_body(x_ref, g_ref, w_ref, xn_ref, q_ref):
    @pl.when(pl.program_id(1) == 0)
    def _():
        xn_ref[...] = _rms(x_ref[...], g_ref[...]).astype(BF16)

    q_ref[...] = jnp.dot(xn_ref[...], w_ref[...], preferred_element_type=F32).astype(q_ref.dtype)


def peer_qproj(h, gain, wq, tm=1024, tn=1024):
    T, D = h.shape
    N = wq.shape[1]
    tm, tn = _tile(T, tm), _tile(N, tn)
    return pl.pallas_call(
        _peer_qproj_body,
        grid=(T // tm, N // tn),
        in_specs=[pl.BlockSpec((tm, D), lambda i, j: (i, 0)),
                  pl.BlockSpec((1, D), lambda i, j: (0, 0)),
                  pl.BlockSpec((D, tn), lambda i, j: (0, j))],
        out_specs=[pl.BlockSpec((tm, D), lambda i, j: (i, 0)),
                   pl.BlockSpec((tm, tn), lambda i, j: (i, j))],
        out_shape=[jax.ShapeDtypeStruct((T, D), BF16), jax.ShapeDtypeStruct((T, N), BF16)],
        compiler_params=_params("parallel", "arbitrary"),
        name="peer_qproj",
    )(h, gain.reshape(1, D), wq)


TOPK_TOKENS = 256


def _key_scores(q_ref, k1, k2, s1_ref, s2_ref):
    half = k1.shape[1]
    s1_ref[...] = lax.dot_general(k1, q_ref[:, 0:half], NT_DIMS, preferred_element_type=F32)
    s2_ref[...] = lax.dot_general(k2, q_ref[:, half:2 * half], NT_DIMS, preferred_element_type=F32)


def _select_from_scores(s1_ref, s2_ref, gate_ref, e1_ref, e2_ref, *, K):
    NK = s1_ref.shape[0]
    tl = V7X_LANES
    key_iota = lax.broadcasted_iota(jnp.int32, (NK, tl), 0)
    groups = [slice(c * tl, (c + 1) * tl) for c in range(s1_ref.shape[1] // tl)]

    def emit(lanes, top):
        rows_s, _, top_s, top_key = top
        ex = jnp.exp(top_s - rows_s[0])
        gate_ref[:, lanes] = ex / jnp.sum(ex, axis=0, keepdims=True)
        e1_ref[:, lanes] = ((top_key >> 7) & 127).astype(F32)
        e2_ref[:, lanes] = (top_key & 127).astype(F32)

    stage1 = _topk_rows_distinct(
        [(ref[:, lanes], key_iota) for lanes in groups for ref in (s1_ref, s2_ref)], K)
    stage2 = _topk_rows_distinct(
        [_candidates(stage1[2 * c][0], stage1[2 * c + 1][0], K) for c in range(len(groups))], K)
    for c, lanes in enumerate(groups):
        emit(lanes, stage2[c][0])

    for c, lanes in enumerate(groups):
        any_tie = jnp.max((stage1[2 * c][1] | stage1[2 * c + 1][1] | stage2[c][1]).astype(jnp.int32))

        @pl.when(any_tie > 0)
        def _():
            top1, top2 = _topk_rows([(s1_ref[:, lanes], key_iota), (s2_ref[:, lanes], key_iota)], K)
            emit(lanes, _topk_rows([_candidates(top1, top2, K)], K)[0])


def _peer_topk_body(q_ref, k1_ref, k2_ref, gate_ref, e1_ref, e2_ref, s1_ref, s2_ref, *, K):
    hd = pl.program_id(1)
    _key_scores(q_ref, k1_ref[hd], k2_ref[hd], s1_ref, s2_ref)
    _select_from_scores(s1_ref, s2_ref, gate_ref, e1_ref, e2_ref, K=K)


def peer_topk(q, k1, k2):
    T = q.shape[0]
    PH, NK, HALF = k1.shape
    K = P_TOPK
    assert NK == V7X_LANES and HALF == V7X_LANES and q.shape[1] == PH * 2 * HALF
    tq = _tile(T, TOPK_TOKENS)
    k_spec = pl.BlockSpec((PH, NK, HALF), lambda c, hd: (0, 0, 0))
    sel_spec = pl.BlockSpec((K, tq), lambda c, hd: (hd, c))
    sel = jax.ShapeDtypeStruct((PH * K, T), F32)
    return pl.pallas_call(
        functools.partial(_peer_topk_body, K=K),
        grid=(T // tq, PH),
        in_specs=[pl.BlockSpec((tq, 2 * HALF), lambda c, hd: (c, hd)), k_spec, k_spec],
        out_specs=[sel_spec] * 3, out_shape=[sel] * 3,
        scratch_shapes=[pltpu.VMEM((NK, tq), F32)] * 2,
        compiler_params=_params("parallel", "arbitrary"),
        name="peer_topk",
    )(q, k1, k2)


SCATTER_PITCH_PAD = V7X_SUBLANES // 2
BF16_SUBLANES = 2 * V7X_SUBLANES


def _peer_scatter_body(gate_ref, e1_ref, e2_ref, w_ref, scr_ref, gt_ref, e1t_ref, *, NK):
    tmb, HK = gt_ref.shape
    pad = SCATTER_PITCH_PAD
    pitch = NK + pad
    gt_ref[...] = gate_ref[...].T
    e1t_ref[...] = e1_ref[...].T
    key_iota = lax.broadcasted_iota(jnp.int32, (NK, HK), 0).astype(F32)
    tall = NK + BF16_SUBLANES
    key_iota_odd = (lax.broadcasted_iota(jnp.int32, (tall, HK), 0) - pad).astype(F32)
    lane_iota = lax.broadcasted_iota(jnp.int32, (HK, NK), 1).astype(F32).astype(BF16)
    e2_cols = e2_ref[...].astype(BF16)
    one, zero = jnp.ones((), BF16), jnp.zeros((), BF16)
    for t in range(tmb):
        odd = t % 2
        g_row = gt_ref[t:t + 1, :]
        sel1 = jnp.where(e1t_ref[t:t + 1, :] == (key_iota_odd if odd else key_iota), g_row, 0.0)
        sel2t = jnp.where(e2_cols[:, t:t + 1] == lane_iota, one, zero)
        w_t = jnp.dot(sel1.astype(BF16), sel2t, preferred_element_type=F32)
        rows = NK + 2 * pad * odd
        start = t * pitch - pad * odd
        scr_ref[start:start + rows, :] = w_t[0:rows]
    for j in range(NK):
        w_ref[j] = scr_ref[pl.ds(j, tmb, stride=pitch), :].astype(w_ref.dtype)


def peer_scatter(gate, e1, e2, NK, tmb=128):
    HK, T = gate.shape
    tmb = _tile(T, tmb)
    assert tmb % 2 == 0
    spec = pl.BlockSpec((HK, tmb), lambda i: (0, i))
    return pl.pallas_call(
        functools.partial(_peer_scatter_body, NK=NK),
        grid=(T // tmb,),
        in_specs=[spec, spec, spec],
        out_specs=pl.BlockSpec((NK, tmb, NK), lambda i: (0, i, 0)),
        out_shape=jax.ShapeDtypeStruct((NK, T, NK), BF16),
        scratch_shapes=[pltpu.VMEM((tmb * (NK + SCATTER_PITCH_PAD), NK), F32),
                        pltpu.VMEM((tmb, HK), F32),
                        pltpu.VMEM((tmb, HK), F32)],
        compiler_params=_params("parallel"),
        name="peer_scatter",
    )(gate, e1, e2)


def _gelu(x):
    return 0.5 * x * (1.0 + lax.erf(x * (2.0 ** -0.5)))


def _peer_dense_body(xn_ref, u_ref, v_ref, w_ref, o_ref, *, NK):
    @pl.when(pl.program_id(1) == 0)
    def _():
        o_ref[...] = jnp.zeros_like(o_ref)

    act = lax.dot_general(xn_ref[...], u_ref[...], NT_DIMS, preferred_element_type=F32)
    coef = jnp.concatenate(
        [(w_ref[c].astype(F32) * _gelu(act[:, c * NK:(c + 1) * NK])).astype(BF16)
         for c in range(w_ref.shape[0])], axis=1)
    o_ref[...] += jnp.dot(coef, v_ref[...], preferred_element_type=F32)


def peer_dense(xn, u_all, v_all, layer, w, tm=1024, te=1024):
    T, D = xn.shape
    E = u_all.shape[1]
    NK = w.shape[0]
    tm, te = _tile(T, tm), _tile(E, te)
    assert te % NK == 0 and E == NK * NK
    return pl.pallas_call(
        functools.partial(_peer_dense_body, NK=NK),
        grid=(T // tm, E // te),
        in_specs=[pl.BlockSpec((tm, D), lambda i, j: (i, 0)),
                  pl.BlockSpec((None, te, D), lambda i, j: (layer, j, 0)),
                  pl.BlockSpec((None, te, D), lambda i, j: (layer, j, 0)),
                  pl.BlockSpec((te // NK, tm, NK), lambda i, j: (j, i, 0))],
        out_specs=pl.BlockSpec((tm, D), lambda i, j: (i, 0)),
        out_shape=jax.ShapeDtypeStruct((T, D), F32),
        compiler_params=pltpu.CompilerParams(dimension_semantics=("parallel", "arbitrary"),
                                             vmem_limit_bytes=VMEM_LIMIT_LARGE),
        name="peer_dense",
    )(xn, u_all, v_all, w)


def peer_layer(h, gain, wq, k1, k2, u_all, v_all, layer):
    xn, q = peer_qproj(h, gain, wq)
    gate, e1, e2 = peer_topk(q, k1, k2)
    w = peer_scatter(gate, e1, e2, k1.shape[1])
    return peer_dense(xn, u_all, v_all, layer, w)


def kernel(x, p, a_norm, a_w_in, a_gate_bias, a_head_norm, a_w_out, kv_norm, w_kv, b_norm, b_w_q, b_sinks,
           b_w_out, c_norm, peer_w_q, peer_k1, peer_k2, peer_u, peer_v, ple_norm, ple_w_gate, ple_w_proj,
           final_norm):
    B, S, D = x.shape
    T = B * S
    depth = p.shape[0]
    n_a = a_norm.shape[0]
    bf = lambda t: t.astype(BF16)
    h = x.reshape(T, D)
    u_all, v_all = bf(peer_u), bf(peer_v)
    for i in range(depth):
        if i < n_a:
            w_in = a_w_in[i]
            n_main = w_in.shape[1] - 2 * M_HEADS
            w_gates = jnp.pad(w_in[:, n_main:], ((0, 0), (0, V7X_LANES - 2 * M_HEADS)))
            proj = rms_matmul(h, a_norm[i], bf(w_in[:, :n_main]), BF16, tm=1024, tn=1024)
            gates = rms_matmul(h, a_norm[i], bf(w_gates), F32)[:, :2 * M_HEADS]
            y = mlstm_scan(proj, gates, a_gate_bias[i], a_head_norm[i], B, S)
            h = matmul_residual(y, bf(a_w_out[i]), h)
        else:
            j = i - n_a
            if j == 0:
                kv = rms_matmul(h, kv_norm, bf(w_kv), BF16)
            q = rms_matmul(h, b_norm[j], bf(b_w_q[j]), BF16, tm=1024, tn=1024)
            o = swa_attention(q, kv, b_sinks[j], B, S)
            h = matmul_residual(o, bf(b_w_out[j]), h)
        peer_out = peer_layer(h, c_norm[i], bf(peer_w_q[i]), bf(peer_k1[i]), bf(peer_k2[i]), u_all, v_all, i)
        h = ple_gate(h, peer_out, ple_norm[i], bf(ple_w_gate[i]), p[i].reshape(T, -1), bf(ple_w_proj[i]),
                     final_norm, final=(i == depth - 1))
    return h.reshape(B, S, D)
```

```python
import functools

import jax
import jax.numpy as jnp
from jax import lax
from jax.experimental import pallas as pl
from jax.experimental.pallas import tpu as pltpu

F32 = jnp.float32
BF16 = jnp.bfloat16

EPS = 1e-6
GATE_CAP = 15.0

M_HEADS = 4
A_HEAD_DIM = 64
A_GROUP = 8
WINDOW = 128
P_HEADS = 8
P_NKEYS = 128
P_TOPK = 16

V7X_LANES = 128
V7X_SUBLANES = 8
V7X_VMEM_BYTES = 64 * 1024 * 1024
VMEM_LIMIT = (V7X_VMEM_BYTES * 3) // 4
VMEM_LIMIT_LARGE = (V7X_VMEM_BYTES * 7) // 8

NT_DIMS = (((1,), (1,)), ((), ()))
TN_DIMS = (((0,), (0,)), ((), ()))


def _params(*sem):
    return pltpu.CompilerParams(dimension_semantics=sem, vmem_limit_bytes=VMEM_LIMIT)


def _rms(x, gain):
    ms = jnp.mean(x * x, axis=-1, keepdims=True)
    return x * lax.rsqrt(ms + EPS) * gain


def _tile(n, pref):
    t = min(n, pref)
    assert n % t == 0, (n, pref)
    return t


def _rms_matmul_body(x_ref, g_ref, w_ref, *rest, side):
    if side:
        ws_ref, o_ref, os_ref, xn_ref = rest
    else:
        o_ref, xn_ref = rest

    @pl.when(pl.program_id(1) == 0)
    def _():
        xn_ref[...] = _rms(x_ref[...], g_ref[...]).astype(BF16)
        if side:
            os_ref[...] = jnp.dot(xn_ref[...], ws_ref[...], preferred_element_type=F32)

    o_ref[...] = jnp.dot(xn_ref[...], w_ref[...], preferred_element_type=F32).astype(o_ref.dtype)


def rms_matmul(x, gain, w, out_dtype, tm=512, tn=512, w_side=None):
    T, D = x.shape
    N = w.shape[1]
    tm, tn = _tile(T, tm), _tile(N, tn)
    side = w_side is not None
    in_specs = [pl.BlockSpec((tm, D), lambda i, j: (i, 0)),
                pl.BlockSpec((1, D), lambda i, j: (0, 0)),
                pl.BlockSpec((D, tn), lambda i, j: (0, j))]
    out_specs = [pl.BlockSpec((tm, tn), lambda i, j: (i, j))]
    out_shape = [jax.ShapeDtypeStruct((T, N), out_dtype)]
    args = [x, gain.reshape(1, D), w]
    if side:
        NS = w_side.shape[1]
        in_specs.append(pl.BlockSpec((D, NS), lambda i, j: (0, 0)))
        out_specs.append(pl.BlockSpec((tm, NS), lambda i, j: (i, 0)))
        out_shape.append(jax.ShapeDtypeStruct((T, NS), F32))
        args.append(w_side)
    res = pl.pallas_call(
        functools.partial(_rms_matmul_body, side=side),
        grid=(T // tm, N // tn),
        in_specs=in_specs, out_specs=out_specs, out_shape=out_shape,
        scratch_shapes=[pltpu.VMEM((tm, D), BF16)],
        compiler_params=_params("parallel", "arbitrary"),
        name="rms_matmul",
    )(*args)
    return res if side else res[0]


def _matmul_residual_body(y_ref, w_ref, h_ref, o_ref):
    o_ref[...] = h_ref[...] + jnp.dot(y_ref[...], w_ref[...], preferred_element_type=F32)


def matmul_residual(y, w, h, tm=512, tn=2048):
    T, K = y.shape
    N = w.shape[1]
    tm, tn = _tile(T, tm), _tile(N, tn)
    return pl.pallas_call(
        _matmul_residual_body,
        grid=(T // tm, N // tn),
        in_specs=[pl.BlockSpec((tm, K), lambda i, j: (i, 0)),
                  pl.BlockSpec((K, tn), lambda i, j: (0, j)),
                  pl.BlockSpec((tm, tn), lambda i, j: (i, j))],
        out_specs=pl.BlockSpec((tm, tn), lambda i, j: (i, j)),
        out_shape=jax.ShapeDtypeStruct((T, N), F32),
        compiler_params=_params("parallel", "arbitrary"),
        name="matmul_residual",
    )(y, w, h)


def _ple_body(h_ref, d_ref, g_ref, wg_ref, p_ref, wp_ref, fn_ref, o_ref, *, final):
    h = h_ref[...] + d_ref[...]
    xn = _rms(h, g_ref[...]).astype(BF16)
    gate = jax.nn.sigmoid(jnp.dot(xn, wg_ref[...], preferred_element_type=F32))
    emb = jnp.dot(p_ref[...].astype(BF16), wp_ref[...], preferred_element_type=F32)
    out = h + gate * emb
    if final:
        out = _rms(out, fn_ref[...])
    o_ref[...] = out


def ple_gate(h, peer_out, gain, wg, p, wp, final_gain, final, tm=256):
    T, D = h.shape
    PD = p.shape[1]
    tm = _tile(T, tm)
    return pl.pallas_call(
        functools.partial(_ple_body, final=final),
        grid=(T // tm,),
        in_specs=[pl.BlockSpec((tm, D), lambda i: (i, 0)),
                  pl.BlockSpec((tm, D), lambda i: (i, 0)),
                  pl.BlockSpec((1, D), lambda i: (0, 0)),
                  pl.BlockSpec((D, D), lambda i: (0, 0)),
                  pl.BlockSpec((tm, PD), lambda i: (i, 0)),
                  pl.BlockSpec((PD, D), lambda i: (0, 0)),
                  pl.BlockSpec((1, D), lambda i: (0, 0))],
        out_specs=pl.BlockSpec((tm, D), lambda i: (i, 0)),
        out_shape=jax.ShapeDtypeStruct((T, D), F32),
        compiler_params=_params("parallel"),
        name="ple_gate",
    )(h, peer_out, gain.reshape(1, D), wg, p, wp, final_gain.reshape(1, D))


def _soft_cap(t):
    return GATE_CAP * jnp.tanh(t / GATE_CAP)


def _log_sigmoid(t):
    return jnp.minimum(t, 0.0) - jnp.log(1.0 + jnp.exp(-jnp.abs(t)))


def _split3(x):
    hi = x.astype(BF16)
    r = x - hi.astype(F32)
    mid = r.astype(BF16)
    lo = (r - mid.astype(F32)).astype(BF16)
    return hi, mid, lo


MLSTM_HEADS_PER_STEP = 4


def _mlstm_body(q_ref, k_ref, v_ref, og_ref, gc_ref, gr_ref, bc_ref, br_ref, hn_ref, o_ref, *state, L, DK, HP):
    @pl.when(pl.program_id(1) == 0)
    def _():
        for ref in state:
            ref[...] = jnp.zeros_like(ref)

    DV = o_ref.shape[1] // HP
    row = lax.broadcasted_iota(jnp.int32, (L, L), 0)
    col = lax.broadcasted_iota(jnp.int32, (L, L), 1)
    causal = row >= col
    tri_l = causal.astype(BF16)
    tri_u = (row <= col).astype(BF16)
    for i in range(HP):
        dk, dv = slice(i * DK, (i + 1) * DK), slice(i * DV, (i + 1) * DV)
        _mlstm_head(q_ref.at[:, dk], k_ref.at[:, dk], v_ref.at[:, dv], og_ref.at[:, dv], gc_ref.at[i], gr_ref.at[i],
                    bc_ref.at[i], br_ref.at[i], hn_ref.at[:, dv], o_ref.at[:, dv], *state[3 * i:3 * i + 3],
                    causal, tri_l, tri_u, L=L, DK=DK)


def _mlstm_head(q_ref, k_ref, v_ref, og_ref, gc_ref, gr_ref, bc_ref, br_ref, hn_ref, o_ref,
                c_ref, n_ref, m_ref, causal, tri_l, tri_u, *, L, DK):
    q = (q_ref[...].astype(F32) * (DK ** -0.5)).astype(BF16)
    k = k_ref[...]
    v = v_ref[...]

    gc = gc_ref[...] + bc_ref[...]
    gr = gr_ref[...] + br_ref[...]
    li_c = _soft_cap(gc[:, 0:1])
    lf_c = _log_sigmoid(_soft_cap(gc[:, 1:2]))
    li_r = _soft_cap(gr[0:1, :])
    lf_r = _log_sigmoid(_soft_cap(gr[1:2, :]))

    b_c = sum(jnp.dot(tri_l, part, preferred_element_type=F32)
              for part in _split3(jnp.broadcast_to(lf_c, (L, V7X_LANES))))[:, 0:1]
    b_r = sum(jnp.dot(part, tri_u, preferred_element_type=F32)
              for part in _split3(jnp.broadcast_to(lf_r, (V7X_SUBLANES, L))))[0:1, :]

    m_prev = m_ref[0:1, 0:1]
    dmat = jnp.where(causal, b_c - b_r + li_r, -jnp.inf)
    inter = b_c + m_prev
    m_t = jnp.maximum(inter, jnp.max(dmat, axis=-1, keepdims=True))
    w_intra = jnp.exp(dmat - m_t)
    w_inter = jnp.exp(inter - m_t)

    s = lax.dot_general(q, k, NT_DIMS, preferred_element_type=F32) * w_intra
    c_old = c_ref[...]
    n_old = n_ref[...]
    num = (w_inter * jnp.dot(q, c_old.astype(BF16), preferred_element_type=F32)
           + jnp.dot(s.astype(BF16), v, preferred_element_type=F32))
    qn = lax.dot_general(q, jnp.broadcast_to(n_old, (V7X_LANES, DK)).astype(BF16), NT_DIMS,
                         preferred_element_type=F32)[:, 0:1]
    den = w_inter * qn + jnp.sum(s, axis=-1, keepdims=True)
    hh = num / jnp.maximum(jnp.abs(den), jnp.exp(-m_t))

    m_new = m_t[L - 1:L, :]
    b_last = b_c[L - 1:L, :]
    w_state = jnp.exp(b_last - b_c + li_c - m_new)
    decay = jnp.exp(b_last + m_prev - m_new)
    kw = k.astype(F32) * w_state
    c_ref[...] = decay * c_old + lax.dot_general(kw.astype(BF16), v, TN_DIMS, preferred_element_type=F32)
    n_ref[...] = decay * n_old + jnp.sum(kw, axis=0, keepdims=True)
    m_ref[...] = jnp.broadcast_to(m_new, m_ref.shape)

    ms = jnp.mean(hh * hh, axis=-1, keepdims=True)
    y = hh * lax.rsqrt(ms + EPS) * hn_ref[...] * jax.nn.sigmoid(og_ref[...].astype(F32))
    o_ref[...] = y.astype(o_ref.dtype)


def mlstm_scan(proj, gates, gate_bias, head_norm, B, S, chunk=256):
    T = proj.shape[0]
    H = M_HEADS
    HDV = head_norm.shape[0]
    DV = HDV // H
    DK = (proj.shape[1] - 2 * HDV) // (2 * H)
    L = _tile(S, chunk)
    NC = S // L
    assert DV % DK == 0 or DK % DV == 0
    k_off = (H * DK) // DK
    v_off = (2 * H * DK) // DV
    og_off = (2 * H * DK + H * DV) // DV
    g2 = gates.reshape(T, 2, H)
    g_col = g2.transpose(2, 0, 1)
    g_row = g2.transpose(2, 1, 0)
    b_col = gate_bias.T.reshape(H, 1, 2)
    b_row = gate_bias.T.reshape(H, 2, 1)

    HP = MLSTM_HEADS_PER_STEP
    assert H % HP == 0 and k_off % HP == 0 and v_off % HP == 0 and og_off % HP == 0
    groups = H // HP

    def tok(g, c):
        return (g // groups) * NC + c

    return pl.pallas_call(
        functools.partial(_mlstm_body, L=L, DK=DK, HP=HP),
        grid=(B * groups, NC),
        in_specs=[pl.BlockSpec((L, HP * DK), lambda g, c: (tok(g, c), g % groups)),
                  pl.BlockSpec((L, HP * DK), lambda g, c: (tok(g, c), k_off // HP + g % groups)),
                  pl.BlockSpec((L, HP * DV), lambda g, c: (tok(g, c), v_off // HP + g % groups)),
                  pl.BlockSpec((L, HP * DV), lambda g, c: (tok(g, c), og_off // HP + g % groups)),
                  pl.BlockSpec((HP, L, 2), lambda g, c: (g % groups, tok(g, c), 0)),
                  pl.BlockSpec((HP, 2, L), lambda g, c: (g % groups, 0, tok(g, c))),
                  pl.BlockSpec((HP, 1, 2), lambda g, c: (g % groups, 0, 0)),
                  pl.BlockSpec((HP, 2, 1), lambda g, c: (g % groups, 0, 0)),
                  pl.BlockSpec((1, HP * DV), lambda g, c: (0, g % groups))],
        out_specs=pl.BlockSpec((L, HP * DV), lambda g, c: (tok(g, c), g % groups)),
        out_shape=jax.ShapeDtypeStruct((T, HDV), BF16),
        scratch_shapes=[pltpu.VMEM((DK, DV), F32),
                        pltpu.VMEM((1, DK), F32),
                        pltpu.VMEM((V7X_SUBLANES, V7X_LANES), F32)] * HP,
        compiler_params=_params("parallel", "arbitrary"),
        name="mlstm_scan",
    )(proj, proj, proj, proj, g_col, g_row, b_col, b_row, head_norm.reshape(1, HDV))


def _swa_body(sink_ref, q_ref, kp_ref, kc_ref, o_ref, *, BLK, KVH, GROUP, HD):
    first = pl.program_id(1) == 0
    ri = lax.broadcasted_iota(jnp.int32, (BLK, BLK), 0)
    ci = lax.broadcasted_iota(jnp.int32, (BLK, BLK), 1)
    from_cur = ci <= ri
    prev_bias = jnp.where(jnp.logical_and(first, ci > ri), -jnp.inf, 0.0)
    lo = lax.broadcasted_iota(jnp.int32, (1, 2 * HD), 1) < HD
    scale = HD ** -0.5

    def head_pair_operands(kv, pair):
        xk = kv[:, pair * 2 * HD:(pair + 1) * 2 * HD]
        xv = kv[:, (KVH + pair * 2) * HD:(KVH + pair * 2 + 2) * HD]
        xk_r = pltpu.roll(xk, HD, 1)
        xv_r = pltpu.roll(xv, HD, 1)
        head0 = (jnp.where(lo, xk, xk_r), jnp.where(lo, xv, 0.0), jnp.where(lo, 0.0, xv_r))
        head1 = (jnp.where(lo, xk_r, xk), jnp.where(lo, xv_r, 0.0), jnp.where(lo, 0.0, xv))
        return [tuple(t.astype(BF16) for t in head) for head in (head0, head1)]

    kv = jnp.concatenate([kp_ref[...], kc_ref[...]], axis=0).astype(F32)
    for pair in range(KVH // 2):
        operands = head_pair_operands(kv, pair)
        for sub in range(2):
            kh = 2 * pair + sub
            k2, *v_halves = operands[sub]
            for gp in range(GROUP // 2):
                h0 = kh * GROUP + 2 * gp
                q2 = (q_ref[:, h0 * HD:(h0 + 2) * HD].astype(F32) * scale).astype(BF16)
                acc = None
                for e in range(2):
                    qm = jnp.where(lo if e == 0 else jnp.logical_not(lo), q2, jnp.zeros_like(q2))
                    s2 = lax.dot_general(qm, k2, NT_DIMS, preferred_element_type=F32)
                    s = jnp.where(from_cur, s2[:, BLK:], s2[:, :BLK] + prev_bias)
                    sink = sink_ref[h0 + e]
                    mx = jnp.maximum(jnp.max(s, axis=-1, keepdims=True), sink)
                    pr = jnp.exp(s - mx)
                    den = jnp.sum(pr, axis=-1, keepdims=True) + jnp.exp(sink - mx)
                    p2 = jnp.concatenate([jnp.where(from_cur, 0.0, pr), jnp.where(from_cur, pr, 0.0)], axis=1)
                    o = jnp.dot(p2.astype(BF16), v_halves[e], preferred_element_type=F32) / den
                    acc = o if acc is None else acc + o
                o_ref[:, h0 * HD:(h0 + 2) * HD] = acc.astype(o_ref.dtype)


def swa_attention(q, kv, sinks, B, S):
    T, QD = q.shape
    HD, GROUP, BLK = A_HEAD_DIM, A_GROUP, WINDOW
    KVH = kv.shape[1] // (2 * HD)
    assert QD == KVH * GROUP * HD and 2 * HD == V7X_LANES and KVH % 2 == 0 and GROUP % 2 == 0
    assert HD in (4, 16, 64, 256)
    NB = S // BLK
    return pl.pallas_call(
        functools.partial(_swa_body, BLK=BLK, KVH=KVH, GROUP=GROUP, HD=HD),
        grid=(B, NB),
        in_specs=[pl.BlockSpec(memory_space=pltpu.SMEM),
                  pl.BlockSpec((BLK, QD), lambda b, n: (b * NB + n, 0)),
                  pl.BlockSpec((BLK, 2 * KVH * HD), lambda b, n: (b * NB + jnp.maximum(n - 1, 0), 0)),
                  pl.BlockSpec((BLK, 2 * KVH * HD), lambda b, n: (b * NB + n, 0))],
        out_specs=pl.BlockSpec((BLK, QD), lambda b, n: (b * NB + n, 0)),
        out_shape=jax.ShapeDtypeStruct((T, QD), BF16),
        compiler_params=_params("parallel", "arbitrary"),
        name="swa_attention",
    )(sinks, q, kv, kv)


KEY_BIG = 1 << 30


def _topk_rows(problems, k):
    tm = problems[0][0].shape[1]
    slot = lax.broadcasted_iota(jnp.int32, (k, tm), 0)
    state = [[vals, keys, [], [], jnp.zeros((k, tm), vals.dtype), jnp.zeros((k, tm), keys.dtype)]
             for vals, keys in problems]
    for it in range(k):
        for st in state:
            vals, keys = st[0], st[1]
            m = jnp.max(vals, axis=0, keepdims=True)
            kmin = jnp.min(jnp.where(vals == m, keys, KEY_BIG), axis=0, keepdims=True)
            st[0] = jnp.where(keys == kmin, -jnp.inf, vals)
            st[2].append(m)
            st[3].append(kmin)
            st[4] = jnp.where(slot == it, m, st[4])
            st[5] = jnp.where(slot == it, kmin, st[5])
    return [tuple(st[2:]) for st in state]


def _oddeven_merge_sort(n):
    pairs = []
    p = 1
    while p < n:
        k = p
        while k >= 1:
            for j in range(k % p, n - k, 2 * k):
                for i in range(min(k, n - j - k)):
                    if (i + j) // (2 * p) == (i + j + k) // (2 * p):
                        pairs.append((i + j, i + j + k))
            k //= 2
        p *= 2
    return pairs


def _topk_rows_distinct(problems, k):
    sub = V7X_SUBLANES
    tl = problems[0][0].shape[1]
    slot = lax.broadcasted_iota(jnp.int32, (k, tl), 0)
    state = []
    for vals, keys in problems:
        n = vals.shape[0] // sub
        v = [vals[j * sub:(j + 1) * sub] for j in range(n)]
        q = [keys[j * sub:(j + 1) * sub] for j in range(n)]
        size = 1
        while size < n:
            size *= 2
        for i, j in _oddeven_merge_sort(size):
            if j < n:
                first = v[i] >= v[j]
                v[i], v[j] = jnp.maximum(v[i], v[j]), jnp.minimum(v[i], v[j])
                q[i], q[j] = jnp.where(first, q[i], q[j]), jnp.where(first, q[j], q[i])
        state.append(dict(v=v, q=q, n=n, rows_v=[], rows_k=[], arr_v=jnp.zeros((k, tl), vals.dtype),
                          arr_k=jnp.zeros((k, tl), keys.dtype), tie=jnp.zeros((1, tl), jnp.bool_),
                          pops=jnp.zeros((sub, tl), F32)))
    for it in range(k):
        for st in state:
            v, q, n = st["v"], st["q"], st["n"]
            m = jnp.max(v[0], axis=0, keepdims=True)
            pop = v[0] == m
            key = jnp.max(jnp.where(pop, q[0], -1), axis=0, keepdims=True)
            st["pops"] = st["pops"] + pop.astype(F32)
            if st["rows_v"]:
                st["tie"] = st["tie"] | (m == st["rows_v"][-1])
            depth = min(n, k - it)
            for j in range(depth):
                v[j] = jnp.where(pop, v[j + 1] if j + 1 < n else -jnp.inf, v[j])
                if j + 1 < n:
                    q[j] = jnp.where(pop, q[j + 1], q[j])
            st["rows_v"].append(m)
            st["rows_k"].append(key)
            st["arr_v"] = jnp.where(slot == it, m, st["arr_v"])
            st["arr_k"] = jnp.where(slot == it, key, st["arr_k"])
    out = []
    for st in state:
        tie = (st["tie"] | (jnp.max(st["v"][0], axis=0, keepdims=True) == st["rows_v"][-1])
               | (jnp.sum(st["pops"], axis=0, keepdims=True) != float(k)))
        out.append(((st["rows_v"], st["rows_k"], st["arr_v"], st["arr_k"]), tie))
    return out


def _candidates(top1, top2, k):
    r1_v, r1_i, v1, i1 = top1
    r2_v, r2_i, v2, i2 = top2
    tm = v1.shape[1]
    sub = V7X_SUBLANES
    vals, keys = [], []

    def pack(code, e1, e2):
        return (code << 14) | (e1 << 7) | e2

    b_iota = lax.broadcasted_iota(jnp.int32, (k, tm), 0)
    vals.append(r1_v[0] + v2)
    keys.append(pack(b_iota, r1_i[0], i2))
    a = 1
    while a < k and k // (a + 1) >= 2:
        nb = k // (a + 1)
        rows = -(-nb // sub) * sub
        bi = lax.broadcasted_iota(jnp.int32, (rows, tm), 0)
        vals.append(jnp.where(bi < nb, r1_v[a] + v2[0:rows], -jnp.inf))
        keys.append(pack(a * k + bi, r1_i[a], i2[0:rows]))
        a += 1
    if a < k:
        a0 = (a // sub) * sub
        ai = lax.broadcasted_iota(jnp.int32, (k - a0, tm), 0) + a0
        vals.append(jnp.where(ai >= a, v1[a0:k] + r2_v[0], -jnp.inf))
        keys.append(pack(ai * k, i1[a0:k], r2_i[0]))
    return jnp.concatenate(vals, axis=0), jnp.concatenate(keys, axis=0)


def _peer_qproj_body(x_ref, g_ref, w_ref, xn_ref, q_ref):
    @pl.when(pl.program_id(1) == 0)
    def _():
        xn_ref[...] = _rms(x_ref[...], g_ref[...]).astype(BF16)

    q_ref[...] = jnp.dot(xn_ref[...], w_ref[...], preferred_element_type=F32).astype(q_ref.dtype)


def peer_qproj(h, gain, wq, tm=512, tn=2048):
    T, D = h.shape
    N = wq.shape[1]
    tm, tn = _tile(T, tm), _tile(N, tn)
    return pl.pallas_call(
        _peer_qproj_body,
        grid=(T // tm, N // tn),
        in_specs=[pl.BlockSpec((tm, D), lambda i, j: (i, 0)),
                  pl.BlockSpec((1, D), lambda i, j: (0, 0)),
                  pl.BlockSpec((D, tn), lambda i, j: (0, j))],
        out_specs=[pl.BlockSpec((tm, D), lambda i, j: (i, 0)),
                   pl.BlockSpec((tm, tn), lambda i, j: (i, j))],
        out_shape=[jax.ShapeDtypeStruct((T, D), BF16), jax.ShapeDtypeStruct((T, N), BF16)],
        compiler_params=_params("parallel", "arbitrary"),
        name="peer_qproj",
    )(h, gain.reshape(1, D), wq)


TOPK_TOKENS = 256


def _key_scores(q_ref, k1, k2, s1_ref, s2_ref):
    half = k1.shape[1]
    s1_ref[...] = lax.dot_general(k1, q_ref[:, 0:half], NT_DIMS, preferred_element_type=F32)
    s2_ref[...] = lax.dot_general(k2, q_ref[:, half:2 * half], NT_DIMS, preferred_element_type=F32)


def _select_from_scores(s1_ref, s2_ref, gate_ref, e1_ref, e2_ref, *, K):
    NK = s1_ref.shape[0]
    tl = V7X_LANES
    key_iota = lax.broadcasted_iota(jnp.int32, (NK, tl), 0)
    groups = [slice(c * tl, (c + 1) * tl) for c in range(s1_ref.shape[1] // tl)]

    def emit(lanes, top):
        rows_s, _, top_s, top_key = top
        ex = jnp.exp(top_s - rows_s[0])
        gate_ref[:, lanes] = ex / jnp.sum(ex, axis=0, keepdims=True)
        e1_ref[:, lanes] = ((top_key >> 7) & 127).astype(F32)
        e2_ref[:, lanes] = (top_key & 127).astype(F32)

    stage1 = _topk_rows_distinct(
        [(ref[:, lanes], key_iota) for lanes in groups for ref in (s1_ref, s2_ref)], K)
    stage2 = _topk_rows_distinct(
        [_candidates(stage1[2 * c][0], stage1[2 * c + 1][0], K) for c in range(len(groups))], K)
    for c, lanes in enumerate(groups):
        emit(lanes, stage2[c][0])

    for c, lanes in enumerate(groups):
        any_tie = jnp.max((stage1[2 * c][1] | stage1[2 * c + 1][1] | stage2[c][1]).astype(jnp.int32))

        @pl.when(any_tie > 0)
        def _():
            top1, top2 = _topk_rows([(s1_ref[:, lanes], key_iota), (s2_ref[:, lanes], key_iota)], K)
            emit(lanes, _topk_rows([_candidates(top1, top2, K)], K)[0])


def _peer_topk_body(q_ref, k1_ref, k2_ref, gate_ref, e1_ref, e2_ref, s1_ref, s2_ref, *, K):
    hd = pl.program_id(1)
    _key_scores(q_ref, k1_ref[hd], k2_ref[hd], s1_ref, s2_ref)
    _select_from_scores(s1_ref, s2_ref, gate_ref, e1_ref, e2_ref, K=K)


def peer_topk(q, k1, k2):
    T = q.shape[0]
    PH, NK, HALF = k1.shape
    K = P_TOPK
    assert NK == V7X_LANES and HALF == V7X_LANES and q.shape[1] == PH * 2 * HALF
    tq = _tile(T, TOPK_TOKENS)
    k_spec = pl.BlockSpec((PH, NK, HALF), lambda c, hd: (0, 0, 0))
    sel_spec = pl.BlockSpec((K, tq), lambda c, hd: (hd, c))
    sel = jax.ShapeDtypeStruct((PH * K, T), F32)
    return pl.pallas_call(
        functools.partial(_peer_topk_body, K=K),
        grid=(T // tq, PH),
        in_specs=[pl.BlockSpec((tq, 2 * HALF), lambda c, hd: (c, hd)), k_spec, k_spec],
        out_specs=[sel_spec] * 3, out_shape=[sel] * 3,
        scratch_shapes=[pltpu.VMEM((NK, tq), F32)] * 2,
        compiler_params=_params("parallel", "arbitrary"),
        name="peer_topk",
    )(q, k1, k2)


SCATTER_PITCH_PAD = V7X_SUBLANES // 2
BF16_SUBLANES = 2 * V7X_SUBLANES


def _peer_scatter_body(gate_ref, e1_ref, e2_ref, w_ref, scr_ref, gt_ref, e1t_ref, *, NK):
    tmb, HK = gt_ref.shape
    pad = SCATTER_PITCH_PAD
    pitch = NK + pad
    gt_ref[...] = gate_ref[...].T
    e1t_ref[...] = e1_ref[...].T
    key_iota = lax.broadcasted_iota(jnp.int32, (NK, HK), 0).astype(F32)
    tall = NK + BF16_SUBLANES
    key_iota_odd = (lax.broadcasted_iota(jnp.int32, (tall, HK), 0) - pad).astype(F32)
    lane_iota = lax.broadcasted_iota(jnp.int32, (HK, NK), 1).astype(F32).astype(BF16)
    e2_cols = e2_ref[...].astype(BF16)
    one, zero = jnp.ones((), BF16), jnp.zeros((), BF16)
    for t in range(tmb):
        odd = t % 2
        g_row = gt_ref[t:t + 1, :]
        sel1 = jnp.where(e1t_ref[t:t + 1, :] == (key_iota_odd if odd else key_iota), g_row, 0.0)
        sel2t = jnp.where(e2_cols[:, t:t + 1] == lane_iota, one, zero)
        w_t = jnp.dot(sel1.astype(BF16), sel2t, preferred_element_type=F32)
        rows = NK + 2 * pad * odd
        start = t * pitch - pad * odd
        scr_ref[start:start + rows, :] = w_t[0:rows]
    for j in range(NK):
        w_ref[j] = scr_ref[pl.ds(j, tmb, stride=pitch), :].astype(w_ref.dtype)


def peer_scatter(gate, e1, e2, NK, tmb=128):
    HK, T = gate.shape
    tmb = _tile(T, tmb)
    assert tmb % 2 == 0
    spec = pl.BlockSpec((HK, tmb), lambda i: (0, i))
    return pl.pallas_call(
        functools.partial(_peer_scatter_body, NK=NK),
        grid=(T // tmb,),
        in_specs=[spec, spec, spec],
        out_specs=pl.BlockSpec((NK, tmb, NK), lambda i: (0, i, 0)),
        out_shape=jax.ShapeDtypeStruct((NK, T, NK), BF16),
        scratch_shapes=[pltpu.VMEM((tmb * (NK + SCATTER_PITCH_PAD), NK), F32),
                        pltpu.VMEM((tmb, HK), F32),
                        pltpu.VMEM((tmb, HK), F32)],
        compiler_params=_params("parallel"),
        name="peer_scatter",
    )(gate, e1, e2)


def _gelu(x):
    return 0.5 * x * (1.0 + lax.erf(x * (2.0 ** -0.5)))


def _peer_dense_body(xn_ref, u_ref, v_ref, w_ref, o_ref, *, NK):
    @pl.when(pl.program_id(1) == 0)
    def _():
        o_ref[...] = jnp.zeros_like(o_ref)

    act = lax.dot_general(xn_ref[...], u_ref[...], NT_DIMS, preferred_element_type=F32)
    coef = jnp.concatenate(
        [(w_ref[c].astype(F32) * _gelu(act[:, c * NK:(c + 1) * NK])).astype(BF16)
         for c in range(w_ref.shape[0])], axis=1)
    o_ref[...] += jnp.dot(coef, v_ref[...], preferred_element_type=F32)


def peer_dense(xn, u_all, v_all, layer, w, tm=1024, te=1024):
    T, D = xn.shape
    E = u_all.shape[1]
    NK = w.shape[0]
    tm, te = _tile(T, tm), _tile(E, te)
    assert te % NK == 0 and E == NK * NK
    return pl.pallas_call(
        functools.partial(_peer_dense_body, NK=NK),
        grid=(T // tm, E // te),
        in_specs=[pl.BlockSpec((tm, D), lambda i, j: (i, 0)),
                  pl.BlockSpec((None, te, D), lambda i, j: (layer, j, 0)),
                  pl.BlockSpec((None, te, D), lambda i, j: (layer, j, 0)),
                  pl.BlockSpec((te // NK, tm, NK), lambda i, j: (j, i, 0))],
        out_specs=pl.BlockSpec((tm, D), lambda i, j: (i, 0)),
        out_shape=jax.ShapeDtypeStruct((T, D), F32),
        compiler_params=pltpu.CompilerParams(dimension_semantics=("parallel", "arbitrary"),
                                             vmem_limit_bytes=VMEM_LIMIT_LARGE),
        name="peer_dense",
    )(xn, u_all, v_all, w)


def peer_layer(h, gain, wq, k1, k2, u_all, v_all, layer):
    xn, q = peer_qproj(h, gain, wq)
    gate, e1, e2 = peer_topk(q, k1, k2)
    w = peer_scatter(gate, e1, e2, k1.shape[1])
    return peer_dense(xn, u_all, v_all, layer, w)


def kernel(x, p, a_norm, a_w_in, a_gate_bias, a_head_norm, a_w_out, kv_norm, w_kv, b_norm, b_w_q, b_sinks,
           b_w_out, c_norm, peer_w_q, peer_k1, peer_k2, peer_u, peer_v, ple_norm, ple_w_gate, ple_w_proj,
           final_norm):
    B, S, D = x.shape
    T = B * S
    depth = p.shape[0]
    n_a = a_norm.shape[0]
    bf = lambda t: t.astype(BF16)
    h = x.reshape(T, D)
    u_all, v_all = bf(peer_u), bf(peer_v)
    for i in range(depth):
        if i < n_a:
            w_in = a_w_in[i]
            n_main = w_in.shape[1] - 2 * M_HEADS
            w_gates = jnp.pad(w_in[:, n_main:], ((0, 0), (0, V7X_LANES - 2 * M_HEADS)))
            proj, gates = rms_matmul(h, a_norm[i], bf(w_in[:, :n_main]), BF16, tm=1024, tn=1024, w_side=bf(w_gates))
            gates = gates[:, :2 * M_HEADS]
            y = mlstm_scan(proj, gates, a_gate_bias[i], a_head_norm[i], B, S)
            h = matmul_residual(y, bf(a_w_out[i]), h)
        else:
            j = i - n_a
            if j == 0:
                kv = rms_matmul(h, kv_norm, bf(w_kv), BF16)
            q = rms_matmul(h, b_norm[j], bf(b_w_q[j]), BF16, tm=512, tn=2048)
            o = swa_attention(q, kv, b_sinks[j], B, S)
            h = matmul_residual(o, bf(b_w_out[j]), h)
        peer_out = peer_layer(h, c_norm[i], bf(peer_w_q[i]), bf(peer_k1[i]), bf(peer_k2[i]), u_all, v_all, i)
        h = ple_gate(h, peer_out, ple_norm[i], bf(ple_w_gate[i]), p[i].reshape(T, -1), bf(ple_w_proj[i]),
                     final_norm, final=(i == depth - 1))
    return h.reshape(B, S, D)
```

```python
import functools

import jax
import jax.numpy as jnp
from jax import lax
from jax.experimental import pallas as pl
from jax.experimental.pallas import tpu as pltpu

F32 = jnp.float32
BF16 = jnp.bfloat16

EPS = 1e-6
GATE_CAP = 15.0

M_HEADS = 4
A_HEAD_DIM = 64
A_GROUP = 8
WINDOW = 128
P_HEADS = 8
P_NKEYS = 128
P_TOPK = 16

V7X_LANES = 128
V7X_SUBLANES = 8
V7X_VMEM_BYTES = 64 * 1024 * 1024
VMEM_LIMIT = (V7X_VMEM_BYTES * 3) // 4
VMEM_LIMIT_LARGE = (V7X_VMEM_BYTES * 7) // 8

NT_DIMS = (((1,), (1,)), ((), ()))
TN_DIMS = (((0,), (0,)), ((), ()))


def _params(*sem):
    return pltpu.CompilerParams(dimension_semantics=sem, vmem_limit_bytes=VMEM_LIMIT)


def _rms(x, gain):
    ms = jnp.mean(x * x, axis=-1, keepdims=True)
    return x * lax.rsqrt(ms + EPS) * gain


def _tile(n, pref):
    t = min(n, pref)
    assert n % t == 0, (n, pref)
    return t


def _rms_matmul_body(x_ref, g_ref, w_ref, *rest, side):
    if side:
        gs_ref, ws_ref, o_ref, os_ref, xn_ref = rest
    else:
        o_ref, xn_ref = rest

    @pl.when(pl.program_id(1) == 0)
    def _():
        x = x_ref[...]
        y = x * lax.rsqrt(jnp.mean(x * x, axis=-1, keepdims=True) + EPS)
        xn_ref[...] = (y * g_ref[...]).astype(BF16)
        if side:
            os_ref[...] = jnp.dot((y * gs_ref[...]).astype(BF16), ws_ref[...],
                                  preferred_element_type=F32).astype(os_ref.dtype)

    o_ref[...] = jnp.dot(xn_ref[...], w_ref[...], preferred_element_type=F32).astype(o_ref.dtype)


def rms_matmul(x, gain, w, out_dtype, tm=512, tn=512, side=None):
    T, D = x.shape
    N = w.shape[1]
    tm, tn = _tile(T, tm), _tile(N, tn)
    in_specs = [pl.BlockSpec((tm, D), lambda i, j: (i, 0)),
                pl.BlockSpec((1, D), lambda i, j: (0, 0)),
                pl.BlockSpec((D, tn), lambda i, j: (0, j))]
    out_specs = [pl.BlockSpec((tm, tn), lambda i, j: (i, j))]
    out_shape = [jax.ShapeDtypeStruct((T, N), out_dtype)]
    args = [x, gain.reshape(1, D), w]
    if side:
        gain_s, w_s, dtype_s = side
        NS = w_s.shape[1]
        in_specs += [pl.BlockSpec((1, D), lambda i, j: (0, 0)), pl.BlockSpec((D, NS), lambda i, j: (0, 0))]
        out_specs.append(pl.BlockSpec((tm, NS), lambda i, j: (i, 0)))
        out_shape.append(jax.ShapeDtypeStruct((T, NS), dtype_s))
        args += [gain_s.reshape(1, D), w_s]
    res = pl.pallas_call(
        functools.partial(_rms_matmul_body, side=bool(side)),
        grid=(T // tm, N // tn),
        in_specs=in_specs, out_specs=out_specs, out_shape=out_shape,
        scratch_shapes=[pltpu.VMEM((tm, D), BF16)],
        compiler_params=_params("parallel", "arbitrary"),
        name="rms_matmul",
    )(*args)
    return res if side else res[0]


def _matmul_residual_body(y_ref, w_ref, h_ref, o_ref):
    o_ref[...] = h_ref[...] + jnp.dot(y_ref[...], w_ref[...], preferred_element_type=F32)


def matmul_residual(y, w, h, tm=512, tn=2048):
    T, K = y.shape
    N = w.shape[1]
    tm, tn = _tile(T, tm), _tile(N, tn)
    return pl.pallas_call(
        _matmul_residual_body,
        grid=(T // tm, N // tn),
        in_specs=[pl.BlockSpec((tm, K), lambda i, j: (i, 0)),
                  pl.BlockSpec((K, tn), lambda i, j: (0, j)),
                  pl.BlockSpec((tm, tn), lambda i, j: (i, j))],
        out_specs=pl.BlockSpec((tm, tn), lambda i, j: (i, j)),
        out_shape=jax.ShapeDtypeStruct((T, N), F32),
        compiler_params=_params("parallel", "arbitrary"),
        name="matmul_residual",
    )(y, w, h)


def _ple_body(h_ref, d_ref, g_ref, wg_ref, p_ref, wp_ref, fn_ref, o_ref, *, final):
    h = h_ref[...] + d_ref[...]
    xn = _rms(h, g_ref[...]).astype(BF16)
    gate = jax.nn.sigmoid(jnp.dot(xn, wg_ref[...], preferred_element_type=F32))
    emb = jnp.dot(p_ref[...].astype(BF16), wp_ref[...], preferred_element_type=F32)
    out = h + gate * emb
    if final:
        out = _rms(out, fn_ref[...])
    o_ref[...] = out


def ple_gate(h, peer_out, gain, wg, p, wp, final_gain, final, tm=256):
    T, D = h.shape
    PD = p.shape[1]
    tm = _tile(T, tm)
    return pl.pallas_call(
        functools.partial(_ple_body, final=final),
        grid=(T // tm,),
        in_specs=[pl.BlockSpec((tm, D), lambda i: (i, 0)),
                  pl.BlockSpec((tm, D), lambda i: (i, 0)),
                  pl.BlockSpec((1, D), lambda i: (0, 0)),
                  pl.BlockSpec((D, D), lambda i: (0, 0)),
                  pl.BlockSpec((tm, PD), lambda i: (i, 0)),
                  pl.BlockSpec((PD, D), lambda i: (0, 0)),
                  pl.BlockSpec((1, D), lambda i: (0, 0))],
        out_specs=pl.BlockSpec((tm, D), lambda i: (i, 0)),
        out_shape=jax.ShapeDtypeStruct((T, D), F32),
        compiler_params=_params("parallel"),
        name="ple_gate",
    )(h, peer_out, gain.reshape(1, D), wg, p, wp, final_gain.reshape(1, D))


def _soft_cap(t):
    return GATE_CAP * jnp.tanh(t / GATE_CAP)


def _log_sigmoid(t):
    return jnp.minimum(t, 0.0) - jnp.log(1.0 + jnp.exp(-jnp.abs(t)))


def _split3(x):
    hi = x.astype(BF16)
    r = x - hi.astype(F32)
    mid = r.astype(BF16)
    lo = (r - mid.astype(F32)).astype(BF16)
    return hi, mid, lo


MLSTM_HEADS_PER_STEP = 4


def _mlstm_body(q_ref, k_ref, v_ref, og_ref, gc_ref, gr_ref, bc_ref, br_ref, hn_ref, o_ref, *state, L, DK, HP):
    @pl.when(pl.program_id(1) == 0)
    def _():
        for ref in state:
            ref[...] = jnp.zeros_like(ref)

    DV = o_ref.shape[1] // HP
    row = lax.broadcasted_iota(jnp.int32, (L, L), 0)
    col = lax.broadcasted_iota(jnp.int32, (L, L), 1)
    causal = row >= col
    tri_l = causal.astype(BF16)
    tri_u = (row <= col).astype(BF16)
    for i in range(HP):
        dk, dv = slice(i * DK, (i + 1) * DK), slice(i * DV, (i + 1) * DV)
        _mlstm_head(q_ref.at[:, dk], k_ref.at[:, dk], v_ref.at[:, dv], og_ref.at[:, dv], gc_ref.at[i], gr_ref.at[i],
                    bc_ref.at[i], br_ref.at[i], hn_ref.at[:, dv], o_ref.at[:, dv], *state[3 * i:3 * i + 3],
                    causal, tri_l, tri_u, L=L, DK=DK)


def _mlstm_head(q_ref, k_ref, v_ref, og_ref, gc_ref, gr_ref, bc_ref, br_ref, hn_ref, o_ref,
                c_ref, n_ref, m_ref, causal, tri_l, tri_u, *, L, DK):
    q = (q_ref[...].astype(F32) * (DK ** -0.5)).astype(BF16)
    k = k_ref[...]
    v = v_ref[...]

    gc = gc_ref[...] + bc_ref[...]
    gr = gr_ref[...] + br_ref[...]
    li_c = _soft_cap(gc[:, 0:1])
    lf_c = _log_sigmoid(_soft_cap(gc[:, 1:2]))
    li_r = _soft_cap(gr[0:1, :])
    lf_r = _log_sigmoid(_soft_cap(gr[1:2, :]))

    b_c = sum(jnp.dot(tri_l, part, preferred_element_type=F32)
              for part in _split3(jnp.broadcast_to(lf_c, (L, V7X_LANES))))[:, 0:1]
    b_r = sum(jnp.dot(part, tri_u, preferred_element_type=F32)
              for part in _split3(jnp.broadcast_to(lf_r, (V7X_SUBLANES, L))))[0:1, :]

    m_prev = m_ref[0:1, 0:1]
    dmat = jnp.where(causal, b_c - b_r + li_r, -jnp.inf)
    inter = b_c + m_prev
    m_t = jnp.maximum(inter, jnp.max(dmat, axis=-1, keepdims=True))
    w_intra = jnp.exp(dmat - m_t)
    w_inter = jnp.exp(inter - m_t)

    s = lax.dot_general(q, k, NT_DIMS, preferred_element_type=F32) * w_intra
    c_old = c_ref[...]
    n_old = n_ref[...]
    num = (w_inter * jnp.dot(q, c_old.astype(BF16), preferred_element_type=F32)
           + jnp.dot(s.astype(BF16), v, preferred_element_type=F32))
    qn = lax.dot_general(q, jnp.broadcast_to(n_old, (V7X_LANES, DK)).astype(BF16), NT_DIMS,
                         preferred_element_type=F32)[:, 0:1]
    den = w_inter * qn + jnp.sum(s, axis=-1, keepdims=True)
    hh = num / jnp.maximum(jnp.abs(den), jnp.exp(-m_t))

    m_new = m_t[L - 1:L, :]
    b_last = b_c[L - 1:L, :]
    w_state = jnp.exp(b_last - b_c + li_c - m_new)
    decay = jnp.exp(b_last + m_prev - m_new)
    kw = k.astype(F32) * w_state
    c_ref[...] = decay * c_old + lax.dot_general(kw.astype(BF16), v, TN_DIMS, preferred_element_type=F32)
    n_ref[...] = decay * n_old + jnp.sum(kw, axis=0, keepdims=True)
    m_ref[...] = jnp.broadcast_to(m_new, m_ref.shape)

    ms = jnp.mean(hh * hh, axis=-1, keepdims=True)
    y = hh * lax.rsqrt(ms + EPS) * hn_ref[...] * jax.nn.sigmoid(og_ref[...].astype(F32))
    o_ref[...] = y.astype(o_ref.dtype)


def mlstm_scan(proj, gates, gate_bias, head_norm, B, S, chunk=256):
    T = proj.shape[0]
    H = M_HEADS
    HDV = head_norm.shape[0]
    DV = HDV // H
    DK = (proj.shape[1] - 2 * HDV) // (2 * H)
    L = _tile(S, chunk)
    NC = S // L
    assert DV % DK == 0 or DK % DV == 0
    k_off = (H * DK) // DK
    v_off = (2 * H * DK) // DV
    og_off = (2 * H * DK + H * DV) // DV
    g2 = gates.reshape(T, 2, H)
    g_col = g2.transpose(2, 0, 1)
    g_row = g2.transpose(2, 1, 0)
    b_col = gate_bias.T.reshape(H, 1, 2)
    b_row = gate_bias.T.reshape(H, 2, 1)

    HP = MLSTM_HEADS_PER_STEP
    assert H % HP == 0 and k_off % HP == 0 and v_off % HP == 0 and og_off % HP == 0
    groups = H // HP

    def tok(g, c):
        return (g // groups) * NC + c

    return pl.pallas_call(
        functools.partial(_mlstm_body, L=L, DK=DK, HP=HP),
        grid=(B * groups, NC),
        in_specs=[pl.BlockSpec((L, HP * DK), lambda g, c: (tok(g, c), g % groups)),
                  pl.BlockSpec((L, HP * DK), lambda g, c: (tok(g, c), k_off // HP + g % groups)),
                  pl.BlockSpec((L, HP * DV), lambda g, c: (tok(g, c), v_off // HP + g % groups)),
                  pl.BlockSpec((L, HP * DV), lambda g, c: (tok(g, c), og_off // HP + g % groups)),
                  pl.BlockSpec((HP, L, 2), lambda g, c: (g % groups, tok(g, c), 0)),
                  pl.BlockSpec((HP, 2, L), lambda g, c: (g % groups, 0, tok(g, c))),
                  pl.BlockSpec((HP, 1, 2), lambda g, c: (g % groups, 0, 0)),
                  pl.BlockSpec((HP, 2, 1), lambda g, c: (g % groups, 0, 0)),
                  pl.BlockSpec((1, HP * DV), lambda g, c: (0, g % groups))],
        out_specs=pl.BlockSpec((L, HP * DV), lambda g, c: (tok(g, c), g % groups)),
        out_shape=jax.ShapeDtypeStruct((T, HDV), BF16),
        scratch_shapes=[pltpu.VMEM((DK, DV), F32),
                        pltpu.VMEM((1, DK), F32),
                        pltpu.VMEM((V7X_SUBLANES, V7X_LANES), F32)] * HP,
        compiler_params=_params("parallel", "arbitrary"),
        name="mlstm_scan",
    )(proj, proj, proj, proj, g_col, g_row, b_col, b_row, head_norm.reshape(1, HDV))


def _swa_body(sink_ref, q_ref, kp_ref, kc_ref, o_ref, *, BLK, KVH, GROUP, HD):
    first = pl.program_id(1) == 0
    ri = lax.broadcasted_iota(jnp.int32, (BLK, BLK), 0)
    ci = lax.broadcasted_iota(jnp.int32, (BLK, BLK), 1)
    from_cur = ci <= ri
    prev_bias = jnp.where(jnp.logical_and(first, ci > ri), -jnp.inf, 0.0)
    lo = lax.broadcasted_iota(jnp.int32, (1, 2 * HD), 1) < HD
    scale = HD ** -0.5

    def head_pair_operands(kv, pair):
        xk = kv[:, pair * 2 * HD:(pair + 1) * 2 * HD]
        xv = kv[:, (KVH + pair * 2) * HD:(KVH + pair * 2 + 2) * HD]
        xk_r = pltpu.roll(xk, HD, 1)
        xv_r = pltpu.roll(xv, HD, 1)
        head0 = (jnp.where(lo, xk, xk_r), jnp.where(lo, xv, 0.0), jnp.where(lo, 0.0, xv_r))
        head1 = (jnp.where(lo, xk_r, xk), jnp.where(lo, xv_r, 0.0), jnp.where(lo, 0.0, xv))
        return [tuple(t.astype(BF16) for t in head) for head in (head0, head1)]

    kv = jnp.concatenate([kp_ref[...], kc_ref[...]], axis=0).astype(F32)
    for pair in range(KVH // 2):
        operands = head_pair_operands(kv, pair)
        for sub in range(2):
            kh = 2 * pair + sub
            k2, *v_halves = operands[sub]
            for gp in range(GROUP // 2):
                h0 = kh * GROUP + 2 * gp
                q2 = (q_ref[:, h0 * HD:(h0 + 2) * HD].astype(F32) * scale).astype(BF16)
                acc = None
                for e in range(2):
                    qm = jnp.where(lo if e == 0 else jnp.logical_not(lo), q2, jnp.zeros_like(q2))
                    s2 = lax.dot_general(qm, k2, NT_DIMS, preferred_element_type=F32)
                    s = jnp.where(from_cur, s2[:, BLK:], s2[:, :BLK] + prev_bias)
                    sink = sink_ref[h0 + e]
                    mx = jnp.maximum(jnp.max(s, axis=-1, keepdims=True), sink)
                    pr = jnp.exp(s - mx)
                    den = jnp.sum(pr, axis=-1, keepdims=True) + jnp.exp(sink - mx)
                    p2 = jnp.concatenate([jnp.where(from_cur, 0.0, pr), jnp.where(from_cur, pr, 0.0)], axis=1)
                    o = jnp.dot(p2.astype(BF16), v_halves[e], preferred_element_type=F32) / den
                    acc = o if acc is None else acc + o
                o_ref[:, h0 * HD:(h0 + 2) * HD] = acc.astype(o_ref.dtype)


def swa_attention(q, kv, sinks, B, S):
    T, QD = q.shape
    HD, GROUP, BLK = A_HEAD_DIM, A_GROUP, WINDOW
    KVH = kv.shape[1] // (2 * HD)
    assert QD == KVH * GROUP * HD and 2 * HD == V7X_LANES and KVH % 2 == 0 and GROUP % 2 == 0
    assert HD in (4, 16, 64, 256)
    NB = S // BLK
    return pl.pallas_call(
        functools.partial(_swa_body, BLK=BLK, KVH=KVH, GROUP=GROUP, HD=HD),
        grid=(B, NB),
        in_specs=[pl.BlockSpec(memory_space=pltpu.SMEM),
                  pl.BlockSpec((BLK, QD), lambda b, n: (b * NB + n, 0)),
                  pl.BlockSpec((BLK, 2 * KVH * HD), lambda b, n: (b * NB + jnp.maximum(n - 1, 0), 0)),
                  pl.BlockSpec((BLK, 2 * KVH * HD), lambda b, n: (b * NB + n, 0))],
        out_specs=pl.BlockSpec((BLK, QD), lambda b, n: (b * NB + n, 0)),
        out_shape=jax.ShapeDtypeStruct((T, QD), BF16),
        compiler_params=_params("parallel", "arbitrary"),
        name="swa_attention",
    )(sinks, q, kv, kv)


KEY_BIG = 1 << 30


def _topk_rows(problems, k):
    tm = problems[0][0].shape[1]
    slot = lax.broadcasted_iota(jnp.int32, (k, tm), 0)
    state = [[vals, keys, [], [], jnp.zeros((k, tm), vals.dtype), jnp.zeros((k, tm), keys.dtype)]
             for vals, keys in problems]
    for it in range(k):
        for st in state:
            vals, keys = st[0], st[1]
            m = jnp.max(vals, axis=0, keepdims=True)
            kmin = jnp.min(jnp.where(vals == m, keys, KEY_BIG), axis=0, keepdims=True)
            st[0] = jnp.where(keys == kmin, -jnp.inf, vals)
            st[2].append(m)
            st[3].append(kmin)
            st[4] = jnp.where(slot == it, m, st[4])
            st[5] = jnp.where(slot == it, kmin, st[5])
    return [tuple(st[2:]) for st in state]


def _oddeven_merge_sort(n):
    pairs = []
    p = 1
    while p < n:
        k = p
        while k >= 1:
            for j in range(k % p, n - k, 2 * k):
                for i in range(min(k, n - j - k)):
                    if (i + j) // (2 * p) == (i + j + k) // (2 * p):
                        pairs.append((i + j, i + j + k))
            k //= 2
        p *= 2
    return pairs


def _topk_rows_distinct(problems, k):
    sub = V7X_SUBLANES
    tl = problems[0][0].shape[1]
    slot = lax.broadcasted_iota(jnp.int32, (k, tl), 0)
    state = []
    for vals, keys in problems:
        n = vals.shape[0] // sub
        v = [vals[j * sub:(j + 1) * sub] for j in range(n)]
        q = [keys[j * sub:(j + 1) * sub] for j in range(n)]
        size = 1
        while size < n:
            size *= 2
        for i, j in _oddeven_merge_sort(size):
            if j < n:
                first = v[i] >= v[j]
                v[i], v[j] = jnp.maximum(v[i], v[j]), jnp.minimum(v[i], v[j])
                q[i], q[j] = jnp.where(first, q[i], q[j]), jnp.where(first, q[j], q[i])
        state.append(dict(v=v, q=q, n=n, rows_v=[], rows_k=[], arr_v=jnp.zeros((k, tl), vals.dtype),
                          arr_k=jnp.zeros((k, tl), keys.dtype), tie=jnp.zeros((1, tl), jnp.bool_),
                          pops=jnp.zeros((sub, tl), F32)))
    for it in range(k):
        for st in state:
            v, q, n = st["v"], st["q"], st["n"]
            m = jnp.max(v[0], axis=0, keepdims=True)
            pop = v[0] == m
            key = jnp.max(jnp.where(pop, q[0], -1), axis=0, keepdims=True)
            st["pops"] = st["pops"] + pop.astype(F32)
            if st["rows_v"]:
                st["tie"] = st["tie"] | (m == st["rows_v"][-1])
            depth = min(n, k - it)
            for j in range(depth):
                v[j] = jnp.where(pop, v[j + 1] if j + 1 < n else -jnp.inf, v[j])
                if j + 1 < n:
                    q[j] = jnp.where(pop, q[j + 1], q[j])
            st["rows_v"].append(m)
            st["rows_k"].append(key)
            st["arr_v"] = jnp.where(slot == it, m, st["arr_v"])
            st["arr_k"] = jnp.where(slot == it, key, st["arr_k"])
    out = []
    for st in state:
        tie = (st["tie"] | (jnp.max(st["v"][0], axis=0, keepdims=True) == st["rows_v"][-1])
               | (jnp.sum(st["pops"], axis=0, keepdims=True) != float(k)))
        out.append(((st["rows_v"], st["rows_k"], st["arr_v"], st["arr_k"]), tie))
    return out


def _candidates(top1, top2, k):
    r1_v, r1_i, v1, i1 = top1
    r2_v, r2_i, v2, i2 = top2
    tm = v1.shape[1]
    sub = V7X_SUBLANES
    vals, keys = [], []

    def pack(code, e1, e2):
        return (code << 14) | (e1 << 7) | e2

    b_iota = lax.broadcasted_iota(jnp.int32, (k, tm), 0)
    vals.append(r1_v[0] + v2)
    keys.append(pack(b_iota, r1_i[0], i2))
    a = 1
    while a < k and k // (a + 1) >= 2:
        nb = k // (a + 1)
        rows = -(-nb // sub) * sub
        bi = lax.broadcasted_iota(jnp.int32, (rows, tm), 0)
        vals.append(jnp.where(bi < nb, r1_v[a] + v2[0:rows], -jnp.inf))
        keys.append(pack(a * k + bi, r1_i[a], i2[0:rows]))
        a += 1
    if a < k:
        a0 = (a // sub) * sub
        ai = lax.broadcasted_iota(jnp.int32, (k - a0, tm), 0) + a0
        vals.append(jnp.where(ai >= a, v1[a0:k] + r2_v[0], -jnp.inf))
        keys.append(pack(ai * k, i1[a0:k], r2_i[0]))
    return jnp.concatenate(vals, axis=0), jnp.concatenate(keys, axis=0)


def _peer_qproj_body(x_ref, g_ref, w_ref, xn_ref, q_ref):
    @pl.when(pl.program_id(1) == 0)
    def _():
        xn_ref[...] = _rms(x_ref[...], g_ref[...]).astype(BF16)

    q_ref[...] = jnp.dot(xn_ref[...], w_ref[...], preferred_element_type=F32).astype(q_ref.dtype)


def peer_qproj(h, gain, wq, tm=512, tn=2048):
    T, D = h.shape
    N = wq.shape[1]
    tm, tn = _tile(T, tm), _tile(N, tn)
    return pl.pallas_call(
        _peer_qproj_body,
        grid=(T // tm, N // tn),
        in_specs=[pl.BlockSpec((tm, D), lambda i, j: (i, 0)),
                  pl.BlockSpec((1, D), lambda i, j: (0, 0)),
                  pl.BlockSpec((D, tn), lambda i, j: (0, j))],
        out_specs=[pl.BlockSpec((tm, D), lambda i, j: (i, 0)),
                   pl.BlockSpec((tm, tn), lambda i, j: (i, j))],
        out_shape=[jax.ShapeDtypeStruct((T, D), BF16), jax.ShapeDtypeStruct((T, N), BF16)],
        compiler_params=_params("parallel", "arbitrary"),
        name="peer_qproj",
    )(h, gain.reshape(1, D), wq)


TOPK_TOKENS = 256


def _key_scores(q_ref, k1, k2, s1_ref, s2_ref):
    half = k1.shape[1]
    s1_ref[...] = lax.dot_general(k1, q_ref[:, 0:half], NT_DIMS, preferred_element_type=F32)
    s2_ref[...] = lax.dot_general(k2, q_ref[:, half:2 * half], NT_DIMS, preferred_element_type=F32)


def _select_from_scores(s1_ref, s2_ref, gate_ref, e1_ref, e2_ref, *, K):
    NK = s1_ref.shape[0]
    tl = V7X_LANES
    key_iota = lax.broadcasted_iota(jnp.int32, (NK, tl), 0)
    groups = [slice(c * tl, (c + 1) * tl) for c in range(s1_ref.shape[1] // tl)]

    def emit(lanes, top):
        rows_s, _, top_s, top_key = top
        ex = jnp.exp(top_s - rows_s[0])
        gate_ref[:, lanes] = ex / jnp.sum(ex, axis=0, keepdims=True)
        e1_ref[:, lanes] = ((top_key >> 7) & 127).astype(F32)
        e2_ref[:, lanes] = (top_key & 127).astype(F32)

    stage1 = _topk_rows_distinct(
        [(ref[:, lanes], key_iota) for lanes in groups for ref in (s1_ref, s2_ref)], K)
    stage2 = _topk_rows_distinct(
        [_candidates(stage1[2 * c][0], stage1[2 * c + 1][0], K) for c in range(len(groups))], K)
    for c, lanes in enumerate(groups):
        emit(lanes, stage2[c][0])

    for c, lanes in enumerate(groups):
        any_tie = jnp.max((stage1[2 * c][1] | stage1[2 * c + 1][1] | stage2[c][1]).astype(jnp.int32))

        @pl.when(any_tie > 0)
        def _():
            top1, top2 = _topk_rows([(s1_ref[:, lanes], key_iota), (s2_ref[:, lanes], key_iota)], K)
            emit(lanes, _topk_rows([_candidates(top1, top2, K)], K)[0])


def _peer_topk_body(q_ref, k1_ref, k2_ref, gate_ref, e1_ref, e2_ref, s1_ref, s2_ref, *, K):
    hd = pl.program_id(1)
    _key_scores(q_ref, k1_ref[hd], k2_ref[hd], s1_ref, s2_ref)
    _select_from_scores(s1_ref, s2_ref, gate_ref, e1_ref, e2_ref, K=K)


def peer_topk(q, k1, k2):
    T = q.shape[0]
    PH, NK, HALF = k1.shape
    K = P_TOPK
    assert NK == V7X_LANES and HALF == V7X_LANES and q.shape[1] == PH * 2 * HALF
    tq = _tile(T, TOPK_TOKENS)
    k_spec = pl.BlockSpec((PH, NK, HALF), lambda c, hd: (0, 0, 0))
    sel_spec = pl.BlockSpec((K, tq), lambda c, hd: (hd, c))
    sel = jax.ShapeDtypeStruct((PH * K, T), F32)
    return pl.pallas_call(
        functools.partial(_peer_topk_body, K=K),
        grid=(T // tq, PH),
        in_specs=[pl.BlockSpec((tq, 2 * HALF), lambda c, hd: (c, hd)), k_spec, k_spec],
        out_specs=[sel_spec] * 3, out_shape=[sel] * 3,
        scratch_shapes=[pltpu.VMEM((NK, tq), F32)] * 2,
        compiler_params=_params("parallel", "arbitrary"),
        name="peer_topk",
    )(q, k1, k2)


SCATTER_PITCH_PAD = V7X_SUBLANES // 2
BF16_SUBLANES = 2 * V7X_SUBLANES


def _peer_scatter_body(gate_ref, e1_ref, e2_ref, w_ref, scr_ref, gt_ref, e1t_ref, *, NK):
    tmb, HK = gt_ref.shape
    pad = SCATTER_PITCH_PAD
    pitch = NK + pad
    gt_ref[...] = gate_ref[...].T
    e1t_ref[...] = e1_ref[...].T
    key_iota = lax.broadcasted_iota(jnp.int32, (NK, HK), 0).astype(F32)
    tall = NK + BF16_SUBLANES
    key_iota_odd = (lax.broadcasted_iota(jnp.int32, (tall, HK), 0) - pad).astype(F32)
    lane_iota = lax.broadcasted_iota(jnp.int32, (HK, NK), 1).astype(F32).astype(BF16)
    e2_cols = e2_ref[...].astype(BF16)
    one, zero = jnp.ones((), BF16), jnp.zeros((), BF16)
    for t in range(tmb):
        odd = t % 2
        g_row = gt_ref[t:t + 1, :]
        sel1 = jnp.where(e1t_ref[t:t + 1, :] == (key_iota_odd if odd else key_iota), g_row, 0.0)
        sel2t = jnp.where(e2_cols[:, t:t + 1] == lane_iota, one, zero)
        w_t = jnp.dot(sel1.astype(BF16), sel2t, preferred_element_type=F32)
        rows = NK + 2 * pad * odd
        start = t * pitch - pad * odd
        scr_ref[start:start + rows, :] = w_t[0:rows]
    for j in range(NK):
        w_ref[j] = scr_ref[pl.ds(j, tmb, stride=pitch), :].astype(w_ref.dtype)


def peer_scatter(gate, e1, e2, NK, tmb=128):
    HK, T = gate.shape
    tmb = _tile(T, tmb)
    assert tmb % 2 == 0
    spec = pl.BlockSpec((HK, tmb), lambda i: (0, i))
    return pl.pallas_call(
        functools.partial(_peer_scatter_body, NK=NK),
        grid=(T // tmb,),
        in_specs=[spec, spec, spec],
        out_specs=pl.BlockSpec((NK, tmb, NK), lambda i: (0, i, 0)),
        out_shape=jax.ShapeDtypeStruct((NK, T, NK), BF16),
        scratch_shapes=[pltpu.VMEM((tmb * (NK + SCATTER_PITCH_PAD), NK), F32),
                        pltpu.VMEM((tmb, HK), F32),
                        pltpu.VMEM((tmb, HK), F32)],
        compiler_params=_params("parallel"),
        name="peer_scatter",
    )(gate, e1, e2)


def _gelu(x):
    return 0.5 * x * (1.0 + lax.erf(x * (2.0 ** -0.5)))


def _peer_dense_body(xn_ref, u_ref, v_ref, w_ref, o_ref, *, NK):
    @pl.when(pl.program_id(1) == 0)
    def _():
        o_ref[...] = jnp.zeros_like(o_ref)

    act = lax.dot_general(xn_ref[...], u_ref[...], NT_DIMS, preferred_element_type=F32)
    coef = jnp.concatenate(
        [(w_ref[c].astype(F32) * _gelu(act[:, c * NK:(c + 1) * NK])).astype(BF16)
         for c in range(w_ref.shape[0])], axis=1)
    o_ref[...] += jnp.dot(coef, v_ref[...], preferred_element_type=F32)


def peer_dense(xn, u_all, v_all, layer, w, tm=1024, te=1024):
    T, D = xn.shape
    E = u_all.shape[1]
    NK = w.shape[0]
    tm, te = _tile(T, tm), _tile(E, te)
    assert te % NK == 0 and E == NK * NK
    return pl.pallas_call(
        functools.partial(_peer_dense_body, NK=NK),
        grid=(T // tm, E // te),
        in_specs=[pl.BlockSpec((tm, D), lambda i, j: (i, 0)),
                  pl.BlockSpec((None, te, D), lambda i, j: (layer, j, 0)),
                  pl.BlockSpec((None, te, D), lambda i, j: (layer, j, 0)),
                  pl.BlockSpec((te // NK, tm, NK), lambda i, j: (j, i, 0))],
        out_specs=pl.BlockSpec((tm, D), lambda i, j: (i, 0)),
        out_shape=jax.ShapeDtypeStruct((T, D), F32),
        compiler_params=pltpu.CompilerParams(dimension_semantics=("parallel", "arbitrary"),
                                             vmem_limit_bytes=VMEM_LIMIT_LARGE),
        name="peer_dense",
    )(xn, u_all, v_all, w)


def peer_layer(h, gain, wq, k1, k2, u_all, v_all, layer):
    xn, q = peer_qproj(h, gain, wq)
    gate, e1, e2 = peer_topk(q, k1, k2)
    w = peer_scatter(gate, e1, e2, k1.shape[1])
    return peer_dense(xn, u_all, v_all, layer, w)


def kernel(x, p, a_norm, a_w_in, a_gate_bias, a_head_norm, a_w_out, kv_norm, w_kv, b_norm, b_w_q, b_sinks,
           b_w_out, c_norm, peer_w_q, peer_k1, peer_k2, peer_u, peer_v, ple_norm, ple_w_gate, ple_w_proj,
           final_norm):
    B, S, D = x.shape
    T = B * S
    depth = p.shape[0]
    n_a = a_norm.shape[0]
    bf = lambda t: t.astype(BF16)
    h = x.reshape(T, D)
    u_all, v_all = bf(peer_u), bf(peer_v)
    for i in range(depth):
        if i < n_a:
            w_in = a_w_in[i]
            n_main = w_in.shape[1] - 2 * M_HEADS
            w_gates = jnp.pad(w_in[:, n_main:], ((0, 0), (0, V7X_LANES - 2 * M_HEADS)))
            proj, gates = rms_matmul(h, a_norm[i], bf(w_in[:, :n_main]), BF16, tm=1024, tn=1024,
                                     side=(a_norm[i], bf(w_gates), F32))
            gates = gates[:, :2 * M_HEADS]
            y = mlstm_scan(proj, gates, a_gate_bias[i], a_head_norm[i], B, S)
            h = matmul_residual(y, bf(a_w_out[i]), h)
        else:
            j = i - n_a
            if j == 0:
                q, kv = rms_matmul(h, b_norm[j], bf(b_w_q[j]), BF16, tm=512, tn=2048,
                                   side=(kv_norm, bf(w_kv), BF16))
            else:
                q = rms_matmul(h, b_norm[j], bf(b_w_q[j]), BF16, tm=512, tn=2048)
            o = swa_attention(q, kv, b_sinks[j], B, S)
            h = matmul_residual(o, bf(b_w_out[j]), h)
        peer_out = peer_layer(h, c_norm[i], bf(peer_w_q[i]), bf(peer_k1[i]), bf(peer_k2[i]), u_all, v_all, i)
        h = ple_gate(h, peer_out, ple_norm[i], bf(ple_w_gate[i]), p[i].reshape(T, -1), bf(ple_w_proj[i]),
                     final_norm, final=(i == depth - 1))
    return h.reshape(B, S, D)
```

```python
import functools

import jax
import jax.numpy as jnp
from jax import lax
from jax.experimental import pallas as pl
from jax.experimental.pallas import tpu as pltpu

F32 = jnp.float32
BF16 = jnp.bfloat16

EPS = 1e-6
GATE_CAP = 15.0

M_HEADS = 4
A_HEAD_DIM = 64
A_GROUP = 8
WINDOW = 128
P_HEADS = 8
P_NKEYS = 128
P_TOPK = 16

V7X_LANES = 128
V7X_SUBLANES = 8
V7X_VMEM_BYTES = 64 * 1024 * 1024
VMEM_LIMIT = (V7X_VMEM_BYTES * 3) // 4
VMEM_LIMIT_LARGE = (V7X_VMEM_BYTES * 7) // 8

NT_DIMS = (((1,), (1,)), ((), ()))
TN_DIMS = (((0,), (0,)), ((), ()))


def _params(*sem):
    return pltpu.CompilerParams(dimension_semantics=sem, vmem_limit_bytes=VMEM_LIMIT)


def _rms(x, gain):
    ms = jnp.mean(x * x, axis=-1, keepdims=True)
    return x * lax.rsqrt(ms + EPS) * gain


def _tile(n, pref):
    t = min(n, pref)
    assert n % t == 0, (n, pref)
    return t


def _rms_matmul_body(x_ref, g_ref, w_ref, *rest, side):
    if side:
        gs_ref, ws_ref, o_ref, os_ref, xn_ref = rest
    else:
        o_ref, xn_ref = rest

    @pl.when(pl.program_id(1) == 0)
    def _():
        x = x_ref[...]
        y = x * lax.rsqrt(jnp.mean(x * x, axis=-1, keepdims=True) + EPS)
        xn_ref[...] = (y * g_ref[...]).astype(BF16)
        if side:
            os_ref[...] = jnp.dot((y * gs_ref[...]).astype(BF16), ws_ref[...],
                                  preferred_element_type=F32).astype(os_ref.dtype)

    o_ref[...] = jnp.dot(xn_ref[...], w_ref[...], preferred_element_type=F32).astype(o_ref.dtype)


def rms_matmul(x, gain, w, out_dtype, tm=512, tn=512, side=None):
    T, D = x.shape
    N = w.shape[1]
    tm, tn = _tile(T, tm), _tile(N, tn)
    in_specs = [pl.BlockSpec((tm, D), lambda i, j: (i, 0)),
                pl.BlockSpec((1, D), lambda i, j: (0, 0)),
                pl.BlockSpec((D, tn), lambda i, j: (0, j))]
    out_specs = [pl.BlockSpec((tm, tn), lambda i, j: (i, j))]
    out_shape = [jax.ShapeDtypeStruct((T, N), out_dtype)]
    args = [x, gain.reshape(1, D), w]
    if side:
        gain_s, w_s, dtype_s = side
        NS = w_s.shape[1]
        in_specs += [pl.BlockSpec((1, D), lambda i, j: (0, 0)), pl.BlockSpec((D, NS), lambda i, j: (0, 0))]
        out_specs.append(pl.BlockSpec((tm, NS), lambda i, j: (i, 0)))
        out_shape.append(jax.ShapeDtypeStruct((T, NS), dtype_s))
        args += [gain_s.reshape(1, D), w_s]
    res = pl.pallas_call(
        functools.partial(_rms_matmul_body, side=bool(side)),
        grid=(T // tm, N // tn),
        in_specs=in_specs, out_specs=out_specs, out_shape=out_shape,
        scratch_shapes=[pltpu.VMEM((tm, D), BF16)],
        compiler_params=_params("parallel", "arbitrary"),
        name="rms_matmul",
    )(*args)
    return res if side else res[0]


def _matmul_residual_body(y_ref, w_ref, h_ref, o_ref):
    o_ref[...] = h_ref[...] + jnp.dot(y_ref[...], w_ref[...], preferred_element_type=F32)


def matmul_residual(y, w, h, tm=512, tn=2048):
    T, K = y.shape
    N = w.shape[1]
    tm, tn = _tile(T, tm), _tile(N, tn)
    return pl.pallas_call(
        _matmul_residual_body,
        grid=(T // tm, N // tn),
        in_specs=[pl.BlockSpec((tm, K), lambda i, j: (i, 0)),
                  pl.BlockSpec((K, tn), lambda i, j: (0, j)),
                  pl.BlockSpec((tm, tn), lambda i, j: (i, j))],
        out_specs=pl.BlockSpec((tm, tn), lambda i, j: (i, j)),
        out_shape=jax.ShapeDtypeStruct((T, N), F32),
        compiler_params=_params("parallel", "arbitrary"),
        name="matmul_residual",
    )(y, w, h)


def _ple_body(h_ref, d_ref, g_ref, wg_ref, p_ref, wp_ref, fn_ref, o_ref, *, final):
    h = h_ref[...] + d_ref[...]
    xn = _rms(h, g_ref[...]).astype(BF16)
    gate = jax.nn.sigmoid(jnp.dot(xn, wg_ref[...], preferred_element_type=F32))
    emb = jnp.dot(p_ref[...].astype(BF16), wp_ref[...], preferred_element_type=F32)
    out = h + gate * emb
    if final:
        out = _rms(out, fn_ref[...])
    o_ref[...] = out


def ple_gate(h, peer_out, gain, wg, p, wp, final_gain, final, tm=256):
    T, D = h.shape
    PD = p.shape[1]
    tm = _tile(T, tm)
    return pl.pallas_call(
        functools.partial(_ple_body, final=final),
        grid=(T // tm,),
        in_specs=[pl.BlockSpec((tm, D), lambda i: (i, 0)),
                  pl.BlockSpec((tm, D), lambda i: (i, 0)),
                  pl.BlockSpec((1, D), lambda i: (0, 0)),
                  pl.BlockSpec((D, D), lambda i: (0, 0)),
                  pl.BlockSpec((tm, PD), lambda i: (i, 0)),
                  pl.BlockSpec((PD, D), lambda i: (0, 0)),
                  pl.BlockSpec((1, D), lambda i: (0, 0))],
        out_specs=pl.BlockSpec((tm, D), lambda i: (i, 0)),
        out_shape=jax.ShapeDtypeStruct((T, D), F32),
        compiler_params=_params("parallel"),
        name="ple_gate",
    )(h, peer_out, gain.reshape(1, D), wg, p, wp, final_gain.reshape(1, D))


def _soft_cap(t):
    return GATE_CAP * jnp.tanh(t / GATE_CAP)


def _log_sigmoid(t):
    return jnp.minimum(t, 0.0) - jnp.log(1.0 + jnp.exp(-jnp.abs(t)))


def _split3(x):
    hi = x.astype(BF16)
    r = x - hi.astype(F32)
    mid = r.astype(BF16)
    lo = (r - mid.astype(F32)).astype(BF16)
    return hi, mid, lo


MLSTM_HEADS_PER_STEP = 4


def _mlstm_body(q_ref, k_ref, v_ref, og_ref, gc_ref, gr_ref, bc_ref, br_ref, hn_ref, o_ref, *state, L, DK, HP):
    @pl.when(pl.program_id(1) == 0)
    def _():
        for ref in state:
            ref[...] = jnp.zeros_like(ref)

    DV = o_ref.shape[1] // HP
    row = lax.broadcasted_iota(jnp.int32, (L, L), 0)
    col = lax.broadcasted_iota(jnp.int32, (L, L), 1)
    causal = row >= col
    tri_l = causal.astype(BF16)
    tri_u = (row <= col).astype(BF16)
    gate_cols = gc_ref[...] + bc_ref[...]
    for i in range(HP):
        dk, dv = slice(i * DK, (i + 1) * DK), slice(i * DV, (i + 1) * DV)
        _mlstm_head(q_ref.at[:, dk], k_ref.at[:, dk], v_ref.at[:, dv], og_ref.at[:, dv],
                    gate_cols[:, i:i + 1], gate_cols[:, HP + i:HP + i + 1], gr_ref.at[i], br_ref.at[i],
                    hn_ref.at[:, dv], o_ref.at[:, dv], *state[3 * i:3 * i + 3],
                    causal, tri_l, tri_u, L=L, DK=DK)


def _mlstm_head(q_ref, k_ref, v_ref, og_ref, gi_col, gf_col, gr_ref, br_ref, hn_ref, o_ref,
                c_ref, n_ref, m_ref, causal, tri_l, tri_u, *, L, DK):
    q = (q_ref[...].astype(F32) * (DK ** -0.5)).astype(BF16)
    k = k_ref[...]
    v = v_ref[...]

    gr = gr_ref[...] + br_ref[...]
    li_c = _soft_cap(gi_col)
    lf_c = _log_sigmoid(_soft_cap(gf_col))
    li_r = _soft_cap(gr[0:1, :])
    lf_r = _log_sigmoid(_soft_cap(gr[1:2, :]))

    b_c = sum(jnp.dot(tri_l, part, preferred_element_type=F32)
              for part in _split3(jnp.broadcast_to(lf_c, (L, V7X_LANES))))[:, 0:1]
    b_r = sum(jnp.dot(part, tri_u, preferred_element_type=F32)
              for part in _split3(jnp.broadcast_to(lf_r, (V7X_SUBLANES, L))))[0:1, :]

    m_prev = m_ref[0:1, 0:1]
    dmat = jnp.where(causal, b_c - b_r + li_r, -jnp.inf)
    inter = b_c + m_prev
    m_t = jnp.maximum(inter, jnp.max(dmat, axis=-1, keepdims=True))
    w_intra = jnp.exp(dmat - m_t)
    w_inter = jnp.exp(inter - m_t)

    s = lax.dot_general(q, k, NT_DIMS, preferred_element_type=F32) * w_intra
    c_old = c_ref[...]
    n_old = n_ref[...]
    num = (w_inter * jnp.dot(q, c_old.astype(BF16), preferred_element_type=F32)
           + jnp.dot(s.astype(BF16), v, preferred_element_type=F32))
    qn = lax.dot_general(q, jnp.broadcast_to(n_old, (V7X_LANES, DK)).astype(BF16), NT_DIMS,
                         preferred_element_type=F32)[:, 0:1]
    den = w_inter * qn + jnp.sum(s, axis=-1, keepdims=True)
    hh = num / jnp.maximum(jnp.abs(den), jnp.exp(-m_t))

    m_new = m_t[L - 1:L, :]
    b_last = b_c[L - 1:L, :]
    w_state = jnp.exp(b_last - b_c + li_c - m_new)
    decay = jnp.exp(b_last + m_prev - m_new)
    kw = k.astype(F32) * w_state
    c_ref[...] = decay * c_old + lax.dot_general(kw.astype(BF16), v, TN_DIMS, preferred_element_type=F32)
    n_ref[...] = decay * n_old + jnp.sum(kw, axis=0, keepdims=True)
    m_ref[...] = jnp.broadcast_to(m_new, m_ref.shape)

    ms = jnp.mean(hh * hh, axis=-1, keepdims=True)
    y = hh * lax.rsqrt(ms + EPS) * hn_ref[...] * jax.nn.sigmoid(og_ref[...].astype(F32))
    o_ref[...] = y.astype(o_ref.dtype)


def mlstm_scan(proj, gates, gate_bias, head_norm, B, S, chunk=256):
    T = proj.shape[0]
    H = M_HEADS
    HDV = head_norm.shape[0]
    DV = HDV // H
    DK = (proj.shape[1] - 2 * HDV) // (2 * H)
    L = _tile(S, chunk)
    NC = S // L
    assert DV % DK == 0 or DK % DV == 0
    k_off = (H * DK) // DK
    v_off = (2 * H * DK) // DV
    og_off = (2 * H * DK + H * DV) // DV
    GW = gates.shape[1]
    g_row = gates[:, :2 * H].reshape(T, 2, H).transpose(2, 1, 0)
    b_col = jnp.pad(gate_bias.reshape(1, 2 * H), ((0, 0), (0, GW - 2 * H)))
    b_row = gate_bias.T.reshape(H, 2, 1)

    HP = MLSTM_HEADS_PER_STEP
    assert H == HP and k_off % HP == 0 and v_off % HP == 0 and og_off % HP == 0
    groups = H // HP

    def tok(g, c):
        return (g // groups) * NC + c

    return pl.pallas_call(
        functools.partial(_mlstm_body, L=L, DK=DK, HP=HP),
        grid=(B * groups, NC),
        in_specs=[pl.BlockSpec((L, HP * DK), lambda g, c: (tok(g, c), g % groups)),
                  pl.BlockSpec((L, HP * DK), lambda g, c: (tok(g, c), k_off // HP + g % groups)),
                  pl.BlockSpec((L, HP * DV), lambda g, c: (tok(g, c), v_off // HP + g % groups)),
                  pl.BlockSpec((L, HP * DV), lambda g, c: (tok(g, c), og_off // HP + g % groups)),
                  pl.BlockSpec((L, GW), lambda g, c: (tok(g, c), 0)),
                  pl.BlockSpec((HP, 2, L), lambda g, c: (g % groups, 0, tok(g, c))),
                  pl.BlockSpec((1, GW), lambda g, c: (0, 0)),
                  pl.BlockSpec((HP, 2, 1), lambda g, c: (g % groups, 0, 0)),
                  pl.BlockSpec((1, HP * DV), lambda g, c: (0, g % groups))],
        out_specs=pl.BlockSpec((L, HP * DV), lambda g, c: (tok(g, c), g % groups)),
        out_shape=jax.ShapeDtypeStruct((T, HDV), BF16),
        scratch_shapes=[pltpu.VMEM((DK, DV), F32),
                        pltpu.VMEM((1, DK), F32),
                        pltpu.VMEM((V7X_SUBLANES, V7X_LANES), F32)] * HP,
        compiler_params=_params("parallel", "arbitrary"),
        name="mlstm_scan",
    )(proj, proj, proj, proj, gates, g_row, b_col, b_row, head_norm.reshape(1, HDV))


def _swa_body(sink_ref, q_ref, kp_ref, kc_ref, o_ref, *, BLK, KVH, GROUP, HD):
    first = pl.program_id(1) == 0
    ri = lax.broadcasted_iota(jnp.int32, (BLK, BLK), 0)
    ci = lax.broadcasted_iota(jnp.int32, (BLK, BLK), 1)
    from_cur = ci <= ri
    prev_bias = jnp.where(jnp.logical_and(first, ci > ri), -jnp.inf, 0.0)
    lo = lax.broadcasted_iota(jnp.int32, (1, 2 * HD), 1) < HD
    scale = HD ** -0.5

    def head_pair_operands(kv, pair):
        xk = kv[:, pair * 2 * HD:(pair + 1) * 2 * HD]
        xv = kv[:, (KVH + pair * 2) * HD:(KVH + pair * 2 + 2) * HD]
        xk_r = pltpu.roll(xk, HD, 1)
        xv_r = pltpu.roll(xv, HD, 1)
        head0 = (jnp.where(lo, xk, xk_r), jnp.where(lo, xv, 0.0), jnp.where(lo, 0.0, xv_r))
        head1 = (jnp.where(lo, xk_r, xk), jnp.where(lo, xv_r, 0.0), jnp.where(lo, 0.0, xv))
        return [tuple(t.astype(BF16) for t in head) for head in (head0, head1)]

    kv = jnp.concatenate([kp_ref[...], kc_ref[...]], axis=0).astype(F32)
    for pair in range(KVH // 2):
        operands = head_pair_operands(kv, pair)
        for sub in range(2):
            kh = 2 * pair + sub
            k2, *v_halves = operands[sub]
            for gp in range(GROUP // 2):
                h0 = kh * GROUP + 2 * gp
                q2 = (q_ref[:, h0 * HD:(h0 + 2) * HD].astype(F32) * scale).astype(BF16)
                acc = None
                for e in range(2):
                    qm = jnp.where(lo if e == 0 else jnp.logical_not(lo), q2, jnp.zeros_like(q2))
                    s2 = lax.dot_general(qm, k2, NT_DIMS, preferred_element_type=F32)
                    s = jnp.where(from_cur, s2[:, BLK:], s2[:, :BLK] + prev_bias)
                    sink = sink_ref[h0 + e]
                    mx = jnp.maximum(jnp.max(s, axis=-1, keepdims=True), sink)
                    pr = jnp.exp(s - mx)
                    den = jnp.sum(pr, axis=-1, keepdims=True) + jnp.exp(sink - mx)
                    p2 = jnp.concatenate([jnp.where(from_cur, 0.0, pr), jnp.where(from_cur, pr, 0.0)], axis=1)
                    o = jnp.dot(p2.astype(BF16), v_halves[e], preferred_element_type=F32) / den
                    acc = o if acc is None else acc + o
                o_ref[:, h0 * HD:(h0 + 2) * HD] = acc.astype(o_ref.dtype)


def swa_attention(q, kv, sinks, B, S):
    T, QD = q.shape
    HD, GROUP, BLK = A_HEAD_DIM, A_GROUP, WINDOW
    KVH = kv.shape[1] // (2 * HD)
    assert QD == KVH * GROUP * HD and 2 * HD == V7X_LANES and KVH % 2 == 0 and GROUP % 2 == 0
    assert HD in (4, 16, 64, 256)
    NB = S // BLK
    return pl.pallas_call(
        functools.partial(_swa_body, BLK=BLK, KVH=KVH, GROUP=GROUP, HD=HD),
        grid=(B, NB),
        in_specs=[pl.BlockSpec(memory_space=pltpu.SMEM),
                  pl.BlockSpec((BLK, QD), lambda b, n: (b * NB + n, 0)),
                  pl.BlockSpec((BLK, 2 * KVH * HD), lambda b, n: (b * NB + jnp.maximum(n - 1, 0), 0)),
                  pl.BlockSpec((BLK, 2 * KVH * HD), lambda b, n: (b * NB + n, 0))],
        out_specs=pl.BlockSpec((BLK, QD), lambda b, n: (b * NB + n, 0)),
        out_shape=jax.ShapeDtypeStruct((T, QD), BF16),
        compiler_params=_params("parallel", "arbitrary"),
        name="swa_attention",
    )(sinks, q, kv, kv)


KEY_BIG = 1 << 30


def _topk_rows(problems, k):
    tm = problems[0][0].shape[1]
    slot = lax.broadcasted_iota(jnp.int32, (k, tm), 0)
    state = [[vals, keys, [], [], jnp.zeros((k, tm), vals.dtype), jnp.zeros((k, tm), keys.dtype)]
             for vals, keys in problems]
    for it in range(k):
        for st in state:
            vals, keys = st[0], st[1]
            m = jnp.max(vals, axis=0, keepdims=True)
            kmin = jnp.min(jnp.where(vals == m, keys, KEY_BIG), axis=0, keepdims=True)
            st[0] = jnp.where(keys == kmin, -jnp.inf, vals)
            st[2].append(m)
            st[3].append(kmin)
            st[4] = jnp.where(slot == it, m, st[4])
            st[5] = jnp.where(slot == it, kmin, st[5])
    return [tuple(st[2:]) for st in state]


def _oddeven_merge_sort(n):
    pairs = []
    p = 1
    while p < n:
        k = p
        while k >= 1:
            for j in range(k % p, n - k, 2 * k):
                for i in range(min(k, n - j - k)):
                    if (i + j) // (2 * p) == (i + j + k) // (2 * p):
                        pairs.append((i + j, i + j + k))
            k //= 2
        p *= 2
    return pairs


def _topk_rows_distinct(problems, k):
    sub = V7X_SUBLANES
    tl = problems[0][0].shape[1]
    slot = lax.broadcasted_iota(jnp.int32, (k, tl), 0)
    state = []
    for vals, keys in problems:
        n = vals.shape[0] // sub
        v = [vals[j * sub:(j + 1) * sub] for j in range(n)]
        q = [keys[j * sub:(j + 1) * sub] for j in range(n)]
        size = 1
        while size < n:
            size *= 2
        for i, j in _oddeven_merge_sort(size):
            if j < n:
                first = v[i] >= v[j]
                v[i], v[j] = jnp.maximum(v[i], v[j]), jnp.minimum(v[i], v[j])
                q[i], q[j] = jnp.where(first, q[i], q[j]), jnp.where(first, q[j], q[i])
        state.append(dict(v=v, q=q, n=n, rows_v=[], rows_k=[], arr_v=jnp.zeros((k, tl), vals.dtype),
                          arr_k=jnp.zeros((k, tl), keys.dtype), tie=jnp.zeros((1, tl), jnp.bool_),
                          pops=jnp.zeros((sub, tl), F32)))
    for it in range(k):
        for st in state:
            v, q, n = st["v"], st["q"], st["n"]
            m = jnp.max(v[0], axis=0, keepdims=True)
            pop = v[0] == m
            key = jnp.max(jnp.where(pop, q[0], -1), axis=0, keepdims=True)
            st["pops"] = st["pops"] + pop.astype(F32)
            if st["rows_v"]:
                st["tie"] = st["tie"] | (m == st["rows_v"][-1])
            depth = min(n, k - it)
            for j in range(depth):
                v[j] = jnp.where(pop, v[j + 1] if j + 1 < n else -jnp.inf, v[j])
                if j + 1 < n:
                    q[j] = jnp.where(pop, q[j + 1], q[j])
            st["rows_v"].append(m)
            st["rows_k"].append(key)
            st["arr_v"] = jnp.where(slot == it, m, st["arr_v"])
            st["arr_k"] = jnp.where(slot == it, key, st["arr_k"])
    out = []
    for st in state:
        tie = (st["tie"] | (jnp.max(st["v"][0], axis=0, keepdims=True) == st["rows_v"][-1])
               | (jnp.sum(st["pops"], axis=0, keepdims=True) != float(k)))
        out.append(((st["rows_v"], st["rows_k"], st["arr_v"], st["arr_k"]), tie))
    return out


def _candidates(top1, top2, k):
    r1_v, r1_i, v1, i1 = top1
    r2_v, r2_i, v2, i2 = top2
    tm = v1.shape[1]
    sub = V7X_SUBLANES
    vals, keys = [], []

    def pack(code, e1, e2):
        return (code << 14) | (e1 << 7) | e2

    b_iota = lax.broadcasted_iota(jnp.int32, (k, tm), 0)
    vals.append(r1_v[0] + v2)
    keys.append(pack(b_iota, r1_i[0], i2))
    a = 1
    while a < k and k // (a + 1) >= 2:
        nb = k // (a + 1)
        rows = -(-nb // sub) * sub
        bi = lax.broadcasted_iota(jnp.int32, (rows, tm), 0)
        vals.append(jnp.where(bi < nb, r1_v[a] + v2[0:rows], -jnp.inf))
        keys.append(pack(a * k + bi, r1_i[a], i2[0:rows]))
        a += 1
    if a < k:
        a0 = (a // sub) * sub
        ai = lax.broadcasted_iota(jnp.int32, (k - a0, tm), 0) + a0
        vals.append(jnp.where(ai >= a, v1[a0:k] + r2_v[0], -jnp.inf))
        keys.append(pack(ai * k, i1[a0:k], r2_i[0]))
    return jnp.concatenate(vals, axis=0), jnp.concatenate(keys, axis=0)


def _peer_qproj_body(x_ref, g_ref, w_ref, xn_ref, q_ref):
    @pl.when(pl.program_id(1) == 0)
    def _():
        xn_ref[...] = _rms(x_ref[...], g_ref[...]).astype(BF16)

    q_ref[...] = jnp.dot(xn_ref[...], w_ref[...], preferred_element_type=F32).astype(q_ref.dtype)


def peer_qproj(h, gain, wq, tm=512, tn=2048):
    T, D = h.shape
    N = wq.shape[1]
    tm, tn = _tile(T, tm), _tile(N, tn)
    return pl.pallas_call(
        _peer_qproj_body,
        grid=(T // tm, N // tn),
        in_specs=[pl.BlockSpec((tm, D), lambda i, j: (i, 0)),
                  pl.BlockSpec((1, D), lambda i, j: (0, 0)),
                  pl.BlockSpec((D, tn), lambda i, j: (0, j))],
        out_specs=[pl.BlockSpec((tm, D), lambda i, j: (i, 0)),
                   pl.BlockSpec((tm, tn), lambda i, j: (i, j))],
        out_shape=[jax.ShapeDtypeStruct((T, D), BF16), jax.ShapeDtypeStruct((T, N), BF16)],
        compiler_params=_params("parallel", "arbitrary"),
        name="peer_qproj",
    )(h, gain.reshape(1, D), wq)


TOPK_TOKENS = 256


def _key_scores(q_ref, k1, k2, s1_ref, s2_ref):
    half = k1.shape[1]
    s1_ref[...] = lax.dot_general(k1, q_ref[:, 0:half], NT_DIMS, preferred_element_type=F32)
    s2_ref[...] = lax.dot_general(k2, q_ref[:, half:2 * half], NT_DIMS, preferred_element_type=F32)


def _select_from_scores(s1_ref, s2_ref, gate_ref, e1_ref, e2_ref, *, K):
    NK = s1_ref.shape[0]
    tl = V7X_LANES
    key_iota = lax.broadcasted_iota(jnp.int32, (NK, tl), 0)
    groups = [slice(c * tl, (c + 1) * tl) for c in range(s1_ref.shape[1] // tl)]

    def emit(lanes, top):
        rows_s, _, top_s, top_key = top
        ex = jnp.exp(top_s - rows_s[0])
        gate_ref[:, lanes] = ex / jnp.sum(ex, axis=0, keepdims=True)
        e1_ref[:, lanes] = ((top_key >> 7) & 127).astype(F32)
        e2_ref[:, lanes] = (top_key & 127).astype(F32)

    stage1 = _topk_rows_distinct(
        [(ref[:, lanes], key_iota) for lanes in groups for ref in (s1_ref, s2_ref)], K)
    stage2 = _topk_rows_distinct(
        [_candidates(stage1[2 * c][0], stage1[2 * c + 1][0], K) for c in range(len(groups))], K)
    for c, lanes in enumerate(groups):
        emit(lanes, stage2[c][0])

    for c, lanes in enumerate(groups):
        any_tie = jnp.max((stage1[2 * c][1] | stage1[2 * c + 1][1] | stage2[c][1]).astype(jnp.int32))

        @pl.when(any_tie > 0)
        def _():
            top1, top2 = _topk_rows([(s1_ref[:, lanes], key_iota), (s2_ref[:, lanes], key_iota)], K)
            emit(lanes, _topk_rows([_candidates(top1, top2, K)], K)[0])


def _peer_topk_body(q_ref, k1_ref, k2_ref, gate_ref, e1_ref, e2_ref, s1_ref, s2_ref, *, K):
    hd = pl.program_id(1)
    _key_scores(q_ref, k1_ref[hd], k2_ref[hd], s1_ref, s2_ref)
    _select_from_scores(s1_ref, s2_ref, gate_ref, e1_ref, e2_ref, K=K)


def peer_topk(q, k1, k2):
    T = q.shape[0]
    PH, NK, HALF = k1.shape
    K = P_TOPK
    assert NK == V7X_LANES and HALF == V7X_LANES and q.shape[1] == PH * 2 * HALF
    tq = _tile(T, TOPK_TOKENS)
    k_spec = pl.BlockSpec((PH, NK, HALF), lambda c, hd: (0, 0, 0))
    sel_spec = pl.BlockSpec((K, tq), lambda c, hd: (hd, c))
    sel = jax.ShapeDtypeStruct((PH * K, T), F32)
    return pl.pallas_call(
        functools.partial(_peer_topk_body, K=K),
        grid=(T // tq, PH),
        in_specs=[pl.BlockSpec((tq, 2 * HALF), lambda c, hd: (c, hd)), k_spec, k_spec],
        out_specs=[sel_spec] * 3, out_shape=[sel] * 3,
        scratch_shapes=[pltpu.VMEM((NK, tq), F32)] * 2,
        compiler_params=_params("parallel", "arbitrary"),
        name="peer_topk",
    )(q, k1, k2)


SCATTER_PITCH_PAD = V7X_SUBLANES // 2
BF16_SUBLANES = 2 * V7X_SUBLANES


def _peer_scatter_body(gate_ref, e1_ref, e2_ref, w_ref, scr_ref, gt_ref, e1t_ref, *, NK):
    tmb, HK = gt_ref.shape
    pad = SCATTER_PITCH_PAD
    pitch = NK + pad
    gt_ref[...] = gate_ref[...].T
    e1t_ref[...] = e1_ref[...].T
    key_iota = lax.broadcasted_iota(jnp.int32, (NK, HK), 0).astype(F32)
    tall = NK + BF16_SUBLANES
    key_iota_odd = (lax.broadcasted_iota(jnp.int32, (tall, HK), 0) - pad).astype(F32)
    lane_iota = lax.broadcasted_iota(jnp.int32, (HK, NK), 1).astype(F32).astype(BF16)
    e2_cols = e2_ref[...].astype(BF16)
    one, zero = jnp.ones((), BF16), jnp.zeros((), BF16)
    for t in range(tmb):
        odd = t % 2
        g_row = gt_ref[t:t + 1, :]
        sel1 = jnp.where(e1t_ref[t:t + 1, :] == (key_iota_odd if odd else key_iota), g_row, 0.0)
        sel2t = jnp.where(e2_cols[:, t:t + 1] == lane_iota, one, zero)
        w_t = jnp.dot(sel1.astype(BF16), sel2t, preferred_element_type=F32)
        rows = NK + 2 * pad * odd
        start = t * pitch - pad * odd
        scr_ref[start:start + rows, :] = w_t[0:rows]
    for j in range(NK):
        w_ref[j] = scr_ref[pl.ds(j, tmb, stride=pitch), :].astype(w_ref.dtype)


def peer_scatter(gate, e1, e2, NK, tmb=128):
    HK, T = gate.shape
    tmb = _tile(T, tmb)
    assert tmb % 2 == 0
    spec = pl.BlockSpec((HK, tmb), lambda i: (0, i))
    return pl.pallas_call(
        functools.partial(_peer_scatter_body, NK=NK),
        grid=(T // tmb,),
        in_specs=[spec, spec, spec],
        out_specs=pl.BlockSpec((NK, tmb, NK), lambda i: (0, i, 0)),
        out_shape=jax.ShapeDtypeStruct((NK, T, NK), BF16),
        scratch_shapes=[pltpu.VMEM((tmb * (NK + SCATTER_PITCH_PAD), NK), F32),
                        pltpu.VMEM((tmb, HK), F32),
                        pltpu.VMEM((tmb, HK), F32)],
        compiler_params=_params("parallel"),
        name="peer_scatter",
    )(gate, e1, e2)


def _gelu(x):
    return 0.5 * x * (1.0 + lax.erf(x * (2.0 ** -0.5)))


def _peer_dense_body(xn_ref, u_ref, v_ref, w_ref, o_ref, *, NK):
    @pl.when(pl.program_id(1) == 0)
    def _():
        o_ref[...] = jnp.zeros_like(o_ref)

    act = lax.dot_general(xn_ref[...], u_ref[...], NT_DIMS, preferred_element_type=F32)
    coef = jnp.concatenate(
        [(w_ref[c].astype(F32) * _gelu(act[:, c * NK:(c + 1) * NK])).astype(BF16)
         for c in range(w_ref.shape[0])], axis=1)
    o_ref[...] += jnp.dot(coef, v_ref[...], preferred_element_type=F32)


def peer_dense(xn, u_all, v_all, layer, w, tm=1024, te=1024):
    T, D = xn.shape
    E = u_all.shape[1]
    NK = w.shape[0]
    tm, te = _tile(T, tm), _tile(E, te)
    assert te % NK == 0 and E == NK * NK
    return pl.pallas_call(
        functools.partial(_peer_dense_body, NK=NK),
        grid=(T // tm, E // te),
        in_specs=[pl.BlockSpec((tm, D), lambda i, j: (i, 0)),
                  pl.BlockSpec((None, te, D), lambda i, j: (layer, j, 0)),
                  pl.BlockSpec((None, te, D), lambda i, j: (layer, j, 0)),
                  pl.BlockSpec((te // NK, tm, NK), lambda i, j: (j, i, 0))],
        out_specs=pl.BlockSpec((tm, D), lambda i, j: (i, 0)),
        out_shape=jax.ShapeDtypeStruct((T, D), F32),
        compiler_params=pltpu.CompilerParams(dimension_semantics=("parallel", "arbitrary"),
                                             vmem_limit_bytes=VMEM_LIMIT_LARGE),
        name="peer_dense",
    )(xn, u_all, v_all, w)


def peer_layer(h, gain, wq, k1, k2, u_all, v_all, layer):
    xn, q = peer_qproj(h, gain, wq)
    gate, e1, e2 = peer_topk(q, k1, k2)
    w = peer_scatter(gate, e1, e2, k1.shape[1])
    return peer_dense(xn, u_all, v_all, layer, w)


def kernel(x, p, a_norm, a_w_in, a_gate_bias, a_head_norm, a_w_out, kv_norm, w_kv, b_norm, b_w_q, b_sinks,
           b_w_out, c_norm, peer_w_q, peer_k1, peer_k2, peer_u, peer_v, ple_norm, ple_w_gate, ple_w_proj,
           final_norm):
    B, S, D = x.shape
    T = B * S
    depth = p.shape[0]
    n_a = a_norm.shape[0]
    bf = lambda t: t.astype(BF16)
    h = x.reshape(T, D)
    u_all, v_all = bf(peer_u), bf(peer_v)
    for i in range(depth):
        if i < n_a:
            w_in = a_w_in[i]
            n_main = w_in.shape[1] - 2 * M_HEADS
            w_gates = jnp.pad(w_in[:, n_main:], ((0, 0), (0, V7X_LANES - 2 * M_HEADS)))
            proj, gates = rms_matmul(h, a_norm[i], bf(w_in[:, :n_main]), BF16, tm=1024, tn=1024,
                                     side=(a_norm[i], bf(w_gates), F32))
            y = mlstm_scan(proj, gates, a_gate_bias[i], a_head_norm[i], B, S)
            h = matmul_residual(y, bf(a_w_out[i]), h)
        else:
            j = i - n_a
            if j == 0:
                q, kv = rms_matmul(h, b_norm[j], bf(b_w_q[j]), BF16, tm=512, tn=2048,
                                   side=(kv_norm, bf(w_kv), BF16))
            else:
                q = rms_matmul(h, b_norm[j], bf(b_w_q[j]), BF16, tm=512, tn=2048)
            o = swa_attention(q, kv, b_sinks[j], B, S)
            h = matmul_residual(o, bf(b_w_out[j]), h)
        peer_out = peer_layer(h, c_norm[i], bf(peer_w_q[i]), bf(peer_k1[i]), bf(peer_k2[i]), u_all, v_all, i)
        h = ple_gate(h, peer_out, ple_norm[i], bf(ple_w_gate[i]), p[i].reshape(T, -1), bf(ple_w_proj[i]),
                     final_norm, final=(i == depth - 1))
    return h.reshape(B, S, D)
```

```python
import functools

import jax
import jax.numpy as jnp
from jax import lax
from jax.experimental import pallas as pl
from jax.experimental.pallas import tpu as pltpu

F32 = jnp.float32
BF16 = jnp.bfloat16

EPS = 1e-6
GATE_CAP = 15.0

M_HEADS = 4
A_HEAD_DIM = 64
A_GROUP = 8
WINDOW = 128
P_HEADS = 8
P_NKEYS = 128
P_TOPK = 16

V7X_LANES = 128
V7X_SUBLANES = 8
V7X_VMEM_BYTES = 64 * 1024 * 1024
VMEM_LIMIT = (V7X_VMEM_BYTES * 3) // 4
VMEM_LIMIT_LARGE = (V7X_VMEM_BYTES * 7) // 8

NT_DIMS = (((1,), (1,)), ((), ()))
TN_DIMS = (((0,), (0,)), ((), ()))


def _params(*sem):
    return pltpu.CompilerParams(dimension_semantics=sem, vmem_limit_bytes=VMEM_LIMIT)


def _rms(x, gain):
    ms = jnp.mean(x * x, axis=-1, keepdims=True)
    return x * lax.rsqrt(ms + EPS) * gain


def _tile(n, pref):
    t = min(n, pref)
    assert n % t == 0, (n, pref)
    return t


def _rms_matmul_body(x_ref, g_ref, w_ref, *rest, side):
    if side:
        gs_ref, ws_ref, o_ref, os_ref, xn_ref = rest
    else:
        o_ref, xn_ref = rest

    @pl.when(pl.program_id(1) == 0)
    def _():
        x = x_ref[...]
        y = x * lax.rsqrt(jnp.mean(x * x, axis=-1, keepdims=True) + EPS)
        xn_ref[...] = (y * g_ref[...]).astype(BF16)
        if side:
            os_ref[...] = jnp.dot((y * gs_ref[...]).astype(BF16), ws_ref[...],
                                  preferred_element_type=F32).astype(os_ref.dtype)

    o_ref[...] = jnp.dot(xn_ref[...], w_ref[...], preferred_element_type=F32).astype(o_ref.dtype)


def rms_matmul(x, gain, w, out_dtype, tm=512, tn=512, side=None, n_cols=None):
    T, D = x.shape
    N = w.shape[1] if n_cols is None else n_cols
    tm, tn = _tile(T, tm), _tile(N, tn)
    in_specs = [pl.BlockSpec((tm, D), lambda i, j: (i, 0)),
                pl.BlockSpec((1, D), lambda i, j: (0, 0)),
                pl.BlockSpec((D, tn), lambda i, j: (0, j))]
    out_specs = [pl.BlockSpec((tm, tn), lambda i, j: (i, j))]
    out_shape = [jax.ShapeDtypeStruct((T, N), out_dtype)]
    args = [x, gain.reshape(1, D), w]
    if side:
        gain_s, w_s, dtype_s = side
        NS = w_s.shape[1]
        in_specs += [pl.BlockSpec((1, D), lambda i, j: (0, 0)), pl.BlockSpec((D, NS), lambda i, j: (0, 0))]
        out_specs.append(pl.BlockSpec((tm, NS), lambda i, j: (i, 0)))
        out_shape.append(jax.ShapeDtypeStruct((T, NS), dtype_s))
        args += [gain_s.reshape(1, D), w_s]
    res = pl.pallas_call(
        functools.partial(_rms_matmul_body, side=bool(side)),
        grid=(T // tm, N // tn),
        in_specs=in_specs, out_specs=out_specs, out_shape=out_shape,
        scratch_shapes=[pltpu.VMEM((tm, D), BF16)],
        compiler_params=_params("parallel", "arbitrary"),
        name="rms_matmul",
    )(*args)
    return res if side else res[0]


def _matmul_residual_body(y_ref, w_ref, h_ref, o_ref):
    o_ref[...] = h_ref[...] + jnp.dot(y_ref[...], w_ref[...], preferred_element_type=F32)


def matmul_residual(y, w, h, tm=512, tn=2048):
    T, K = y.shape
    N = w.shape[1]
    tm, tn = _tile(T, tm), _tile(N, tn)
    return pl.pallas_call(
        _matmul_residual_body,
        grid=(T // tm, N // tn),
        in_specs=[pl.BlockSpec((tm, K), lambda i, j: (i, 0)),
                  pl.BlockSpec((K, tn), lambda i, j: (0, j)),
                  pl.BlockSpec((tm, tn), lambda i, j: (i, j))],
        out_specs=pl.BlockSpec((tm, tn), lambda i, j: (i, j)),
        out_shape=jax.ShapeDtypeStruct((T, N), F32),
        compiler_params=_params("parallel", "arbitrary"),
        name="matmul_residual",
    )(y, w, h)


def _ple_body(h_ref, d_ref, g_ref, wg_ref, p_ref, wp_ref, fn_ref, o_ref, *, final):
    h = h_ref[...] + d_ref[...]
    xn = _rms(h, g_ref[...]).astype(BF16)
    gate = jax.nn.sigmoid(jnp.dot(xn, wg_ref[...], preferred_element_type=F32))
    emb = jnp.dot(p_ref[...].astype(BF16), wp_ref[...], preferred_element_type=F32)
    out = h + gate * emb
    if final:
        out = _rms(out, fn_ref[...])
    o_ref[...] = out


def ple_gate(h, peer_out, gain, wg, p_all, layer, wp, final_gain, final, tm=256):
    T, D = h.shape
    PD = p_all.shape[2]
    tm = _tile(T, tm)
    return pl.pallas_call(
        functools.partial(_ple_body, final=final),
        grid=(T // tm,),
        in_specs=[pl.BlockSpec((tm, D), lambda i: (i, 0)),
                  pl.BlockSpec((tm, D), lambda i: (i, 0)),
                  pl.BlockSpec((1, D), lambda i: (0, 0)),
                  pl.BlockSpec((D, D), lambda i: (0, 0)),
                  pl.BlockSpec((None, tm, PD), lambda i: (layer, i, 0)),
                  pl.BlockSpec((PD, D), lambda i: (0, 0)),
                  pl.BlockSpec((1, D), lambda i: (0, 0))],
        out_specs=pl.BlockSpec((tm, D), lambda i: (i, 0)),
        out_shape=jax.ShapeDtypeStruct((T, D), F32),
        compiler_params=_params("parallel"),
        name="ple_gate",
    )(h, peer_out, gain.reshape(1, D), wg, p_all, wp, final_gain.reshape(1, D))


def _soft_cap(t):
    return GATE_CAP * jnp.tanh(t / GATE_CAP)


def _log_sigmoid(t):
    return jnp.minimum(t, 0.0) - jnp.log(1.0 + jnp.exp(-jnp.abs(t)))


def _split3(x):
    hi = x.astype(BF16)
    r = x - hi.astype(F32)
    mid = r.astype(BF16)
    lo = (r - mid.astype(F32)).astype(BF16)
    return hi, mid, lo


MLSTM_HEADS_PER_STEP = 4


def _mlstm_body(q_ref, k_ref, v_ref, og_ref, gc_ref, gr_ref, bc_ref, br_ref, hn_ref, o_ref, *state, L, DK, HP):
    @pl.when(pl.program_id(1) == 0)
    def _():
        for ref in state:
            ref[...] = jnp.zeros_like(ref)

    DV = o_ref.shape[1] // HP
    row = lax.broadcasted_iota(jnp.int32, (L, L), 0)
    col = lax.broadcasted_iota(jnp.int32, (L, L), 1)
    causal = row >= col
    tri_l = causal.astype(BF16)
    tri_u = (row <= col).astype(BF16)
    gate_cols = gc_ref[...] + bc_ref[...]
    for i in range(HP):
        dk, dv = slice(i * DK, (i + 1) * DK), slice(i * DV, (i + 1) * DV)
        _mlstm_head(q_ref.at[:, dk], k_ref.at[:, dk], v_ref.at[:, dv], og_ref.at[:, dv],
                    gate_cols[:, i:i + 1], gate_cols[:, HP + i:HP + i + 1], gr_ref.at[i], br_ref.at[i],
                    hn_ref.at[:, dv], o_ref.at[:, dv], *state[3 * i:3 * i + 3],
                    causal, tri_l, tri_u, L=L, DK=DK)


def _mlstm_head(q_ref, k_ref, v_ref, og_ref, gi_col, gf_col, gr_ref, br_ref, hn_ref, o_ref,
                c_ref, n_ref, m_ref, causal, tri_l, tri_u, *, L, DK):
    q = (q_ref[...].astype(F32) * (DK ** -0.5)).astype(BF16)
    k = k_ref[...]
    v = v_ref[...]

    gr = gr_ref[...] + br_ref[...]
    li_c = _soft_cap(gi_col)
    lf_c = _log_sigmoid(_soft_cap(gf_col))
    li_r = _soft_cap(gr[0:1, :])
    lf_r = _log_sigmoid(_soft_cap(gr[1:2, :]))

    b_c = sum(jnp.dot(tri_l, part, preferred_element_type=F32)
              for part in _split3(jnp.broadcast_to(lf_c, (L, V7X_LANES))))[:, 0:1]
    b_r = sum(jnp.dot(part, tri_u, preferred_element_type=F32)
              for part in _split3(jnp.broadcast_to(lf_r, (V7X_SUBLANES, L))))[0:1, :]

    m_prev = m_ref[0:1, 0:1]
    dmat = jnp.where(causal, b_c - b_r + li_r, -jnp.inf)
    inter = b_c + m_prev
    m_t = jnp.maximum(inter, jnp.max(dmat, axis=-1, keepdims=True))
    w_intra = jnp.exp(dmat - m_t)
    w_inter = jnp.exp(inter - m_t)

    s = lax.dot_general(q, k, NT_DIMS, preferred_element_type=F32) * w_intra
    c_old = c_ref[...]
    n_old = n_ref[...]
    num = (w_inter * jnp.dot(q, c_old.astype(BF16), preferred_element_type=F32)
           + jnp.dot(s.astype(BF16), v, preferred_element_type=F32))
    qn = lax.dot_general(q, jnp.broadcast_to(n_old, (V7X_LANES, DK)).astype(BF16), NT_DIMS,
                         preferred_element_type=F32)[:, 0:1]
    den = w_inter * qn + jnp.sum(s, axis=-1, keepdims=True)
    hh = num / jnp.maximum(jnp.abs(den), jnp.exp(-m_t))

    m_new = m_t[L - 1:L, :]
    b_last = b_c[L - 1:L, :]
    w_state = jnp.exp(b_last - b_c + li_c - m_new)
    decay = jnp.exp(b_last + m_prev - m_new)
    kw = k.astype(F32) * w_state
    c_ref[...] = decay * c_old + lax.dot_general(kw.astype(BF16), v, TN_DIMS, preferred_element_type=F32)
    n_ref[...] = decay * n_old + jnp.sum(kw, axis=0, keepdims=True)
    m_ref[...] = jnp.broadcast_to(m_new, m_ref.shape)

    ms = jnp.mean(hh * hh, axis=-1, keepdims=True)
    y = hh * lax.rsqrt(ms + EPS) * hn_ref[...] * jax.nn.sigmoid(og_ref[...].astype(F32))
    o_ref[...] = y.astype(o_ref.dtype)


def mlstm_scan(proj, gates, gate_bias, head_norm, B, S, chunk=256):
    T = proj.shape[0]
    H = M_HEADS
    HDV = head_norm.shape[0]
    DV = HDV // H
    DK = (proj.shape[1] - 2 * HDV) // (2 * H)
    L = _tile(S, chunk)
    NC = S // L
    assert DV % DK == 0 or DK % DV == 0
    k_off = (H * DK) // DK
    v_off = (2 * H * DK) // DV
    og_off = (2 * H * DK + H * DV) // DV
    GW = gates.shape[1]
    g_row = gates[:, :2 * H].reshape(T, 2, H).transpose(2, 1, 0)
    b_col = jnp.pad(gate_bias.reshape(1, 2 * H), ((0, 0), (0, GW - 2 * H)))
    b_row = gate_bias.T.reshape(H, 2, 1)

    HP = MLSTM_HEADS_PER_STEP
    assert H == HP and k_off % HP == 0 and v_off % HP == 0 and og_off % HP == 0
    groups = H // HP

    def tok(g, c):
        return (g // groups) * NC + c

    return pl.pallas_call(
        functools.partial(_mlstm_body, L=L, DK=DK, HP=HP),
        grid=(B * groups, NC),
        in_specs=[pl.BlockSpec((L, HP * DK), lambda g, c: (tok(g, c), g % groups)),
                  pl.BlockSpec((L, HP * DK), lambda g, c: (tok(g, c), k_off // HP + g % groups)),
                  pl.BlockSpec((L, HP * DV), lambda g, c: (tok(g, c), v_off // HP + g % groups)),
                  pl.BlockSpec((L, HP * DV), lambda g, c: (tok(g, c), og_off // HP + g % groups)),
                  pl.BlockSpec((L, GW), lambda g, c: (tok(g, c), 0)),
                  pl.BlockSpec((HP, 2, L), lambda g, c: (g % groups, 0, tok(g, c))),
                  pl.BlockSpec((1, GW), lambda g, c: (0, 0)),
                  pl.BlockSpec((HP, 2, 1), lambda g, c: (g % groups, 0, 0)),
                  pl.BlockSpec((1, HP * DV), lambda g, c: (0, g % groups))],
        out_specs=pl.BlockSpec((L, HP * DV), lambda g, c: (tok(g, c), g % groups)),
        out_shape=jax.ShapeDtypeStruct((T, HDV), BF16),
        scratch_shapes=[pltpu.VMEM((DK, DV), F32),
                        pltpu.VMEM((1, DK), F32),
                        pltpu.VMEM((V7X_SUBLANES, V7X_LANES), F32)] * HP,
        compiler_params=_params("parallel", "arbitrary"),
        name="mlstm_scan",
    )(proj, proj, proj, proj, gates, g_row, b_col, b_row, head_norm.reshape(1, HDV))


def _swa_body(sink_ref, q_ref, kp_ref, kc_ref, o_ref, *, BLK, KVH, GROUP, HD):
    first = pl.program_id(1) == 0
    ri = lax.broadcasted_iota(jnp.int32, (BLK, BLK), 0)
    ci = lax.broadcasted_iota(jnp.int32, (BLK, BLK), 1)
    from_cur = ci <= ri
    prev_bias = jnp.where(jnp.logical_and(first, ci > ri), -jnp.inf, 0.0)
    lo = lax.broadcasted_iota(jnp.int32, (1, 2 * HD), 1) < HD
    scale = HD ** -0.5

    def head_pair_operands(kv, pair):
        xk = kv[:, pair * 2 * HD:(pair + 1) * 2 * HD]
        xv = kv[:, (KVH + pair * 2) * HD:(KVH + pair * 2 + 2) * HD]
        xk_r = pltpu.roll(xk, HD, 1)
        xv_r = pltpu.roll(xv, HD, 1)
        head0 = (jnp.where(lo, xk, xk_r), jnp.where(lo, xv, 0.0), jnp.where(lo, 0.0, xv_r))
        head1 = (jnp.where(lo, xk_r, xk), jnp.where(lo, xv_r, 0.0), jnp.where(lo, 0.0, xv))
        return [tuple(t.astype(BF16) for t in head) for head in (head0, head1)]

    kv = jnp.concatenate([kp_ref[...], kc_ref[...]], axis=0).astype(F32)
    for pair in range(KVH // 2):
        operands = head_pair_operands(kv, pair)
        for sub in range(2):
            kh = 2 * pair + sub
            k2, *v_halves = operands[sub]
            for gp in range(GROUP // 2):
                h0 = kh * GROUP + 2 * gp
                q2 = (q_ref[:, h0 * HD:(h0 + 2) * HD].astype(F32) * scale).astype(BF16)
                acc = None
                for e in range(2):
                    qm = jnp.where(lo if e == 0 else jnp.logical_not(lo), q2, jnp.zeros_like(q2))
                    s2 = lax.dot_general(qm, k2, NT_DIMS, preferred_element_type=F32)
                    s = jnp.where(from_cur, s2[:, BLK:], s2[:, :BLK] + prev_bias)
                    sink = sink_ref[h0 + e]
                    mx = jnp.maximum(jnp.max(s, axis=-1, keepdims=True), sink)
                    pr = jnp.exp(s - mx)
                    den = jnp.sum(pr, axis=-1, keepdims=True) + jnp.exp(sink - mx)
                    p2 = jnp.concatenate([jnp.where(from_cur, 0.0, pr), jnp.where(from_cur, pr, 0.0)], axis=1)
                    o = jnp.dot(p2.astype(BF16), v_halves[e], preferred_element_type=F32) / den
                    acc = o if acc is None else acc + o
                o_ref[:, h0 * HD:(h0 + 2) * HD] = acc.astype(o_ref.dtype)


def swa_attention(q, kv, sinks, B, S):
    T, QD = q.shape
    HD, GROUP, BLK = A_HEAD_DIM, A_GROUP, WINDOW
    KVH = kv.shape[1] // (2 * HD)
    assert QD == KVH * GROUP * HD and 2 * HD == V7X_LANES and KVH % 2 == 0 and GROUP % 2 == 0
    assert HD in (4, 16, 64, 256)
    NB = S // BLK
    return pl.pallas_call(
        functools.partial(_swa_body, BLK=BLK, KVH=KVH, GROUP=GROUP, HD=HD),
        grid=(B, NB),
        in_specs=[pl.BlockSpec(memory_space=pltpu.SMEM),
                  pl.BlockSpec((BLK, QD), lambda b, n: (b * NB + n, 0)),
                  pl.BlockSpec((BLK, 2 * KVH * HD), lambda b, n: (b * NB + jnp.maximum(n - 1, 0), 0)),
                  pl.BlockSpec((BLK, 2 * KVH * HD), lambda b, n: (b * NB + n, 0))],
        out_specs=pl.BlockSpec((BLK, QD), lambda b, n: (b * NB + n, 0)),
        out_shape=jax.ShapeDtypeStruct((T, QD), BF16),
        compiler_params=_params("parallel", "arbitrary"),
        name="swa_attention",
    )(sinks, q, kv, kv)


KEY_BIG = 1 << 30


def _topk_rows(problems, k):
    tm = problems[0][0].shape[1]
    slot = lax.broadcasted_iota(jnp.int32, (k, tm), 0)
    state = [[vals, keys, [], [], jnp.zeros((k, tm), vals.dtype), jnp.zeros((k, tm), keys.dtype)]
             for vals, keys in problems]
    for it in range(k):
        for st in state:
            vals, keys = st[0], st[1]
            m = jnp.max(vals, axis=0, keepdims=True)
            kmin = jnp.min(jnp.where(vals == m, keys, KEY_BIG), axis=0, keepdims=True)
            st[0] = jnp.where(keys == kmin, -jnp.inf, vals)
            st[2].append(m)
            st[3].append(kmin)
            st[4] = jnp.where(slot == it, m, st[4])
            st[5] = jnp.where(slot == it, kmin, st[5])
    return [tuple(st[2:]) for st in state]


def _oddeven_merge_sort(n):
    pairs = []
    p = 1
    while p < n:
        k = p
        while k >= 1:
            for j in range(k % p, n - k, 2 * k):
                for i in range(min(k, n - j - k)):
                    if (i + j) // (2 * p) == (i + j + k) // (2 * p):
                        pairs.append((i + j, i + j + k))
            k //= 2
        p *= 2
    return pairs


def _topk_rows_distinct(problems, k):
    sub = V7X_SUBLANES
    tl = problems[0][0].shape[1]
    slot = lax.broadcasted_iota(jnp.int32, (k, tl), 0)
    state = []
    for vals, keys in problems:
        n = vals.shape[0] // sub
        v = [vals[j * sub:(j + 1) * sub] for j in range(n)]
        q = [keys[j * sub:(j + 1) * sub] for j in range(n)]
        size = 1
        while size < n:
            size *= 2
        for i, j in _oddeven_merge_sort(size):
            if j < n:
                first = v[i] >= v[j]
                v[i], v[j] = jnp.maximum(v[i], v[j]), jnp.minimum(v[i], v[j])
                q[i], q[j] = jnp.where(first, q[i], q[j]), jnp.where(first, q[j], q[i])
        state.append(dict(v=v, q=q, n=n, rows_v=[], rows_k=[], arr_v=jnp.zeros((k, tl), vals.dtype),
                          arr_k=jnp.zeros((k, tl), keys.dtype), tie=jnp.zeros((1, tl), jnp.bool_),
                          pops=jnp.zeros((sub, tl), F32)))
    for it in range(k):
        for st in state:
            v, q, n = st["v"], st["q"], st["n"]
            m = jnp.max(v[0], axis=0, keepdims=True)
            pop = v[0] == m
            key = jnp.max(jnp.where(pop, q[0], -1), axis=0, keepdims=True)
            st["pops"] = st["pops"] + pop.astype(F32)
            if st["rows_v"]:
                st["tie"] = st["tie"] | (m == st["rows_v"][-1])
            depth = min(n, k - it)
            for j in range(depth):
                v[j] = jnp.where(pop, v[j + 1] if j + 1 < n else -jnp.inf, v[j])
                if j + 1 < n:
                    q[j] = jnp.where(pop, q[j + 1], q[j])
            st["rows_v"].append(m)
            st["rows_k"].append(key)
            st["arr_v"] = jnp.where(slot == it, m, st["arr_v"])
            st["arr_k"] = jnp.where(slot == it, key, st["arr_k"])
    out = []
    for st in state:
        tie = (st["tie"] | (jnp.max(st["v"][0], axis=0, keepdims=True) == st["rows_v"][-1])
               | (jnp.sum(st["pops"], axis=0, keepdims=True) != float(k)))
        out.append(((st["rows_v"], st["rows_k"], st["arr_v"], st["arr_k"]), tie))
    return out


def _candidates(top1, top2, k):
    r1_v, r1_i, v1, i1 = top1
    r2_v, r2_i, v2, i2 = top2
    tm = v1.shape[1]
    sub = V7X_SUBLANES
    vals, keys = [], []

    def pack(code, e1, e2):
        return (code << 14) | (e1 << 7) | e2

    b_iota = lax.broadcasted_iota(jnp.int32, (k, tm), 0)
    vals.append(r1_v[0] + v2)
    keys.append(pack(b_iota, r1_i[0], i2))
    a = 1
    while a < k and k // (a + 1) >= 2:
        nb = k // (a + 1)
        rows = -(-nb // sub) * sub
        bi = lax.broadcasted_iota(jnp.int32, (rows, tm), 0)
        vals.append(jnp.where(bi < nb, r1_v[a] + v2[0:rows], -jnp.inf))
        keys.append(pack(a * k + bi, r1_i[a], i2[0:rows]))
        a += 1
    if a < k:
        a0 = (a // sub) * sub
        ai = lax.broadcasted_iota(jnp.int32, (k - a0, tm), 0) + a0
        vals.append(jnp.where(ai >= a, v1[a0:k] + r2_v[0], -jnp.inf))
        keys.append(pack(ai * k, i1[a0:k], r2_i[0]))
    return jnp.concatenate(vals, axis=0), jnp.concatenate(keys, axis=0)


def _peer_qproj_body(x_ref, g_ref, w_ref, xn_ref, q_ref):
    @pl.when(pl.program_id(1) == 0)
    def _():
        xn_ref[...] = _rms(x_ref[...], g_ref[...]).astype(BF16)

    q_ref[...] = jnp.dot(xn_ref[...], w_ref[...], preferred_element_type=F32).astype(q_ref.dtype)


def peer_qproj(h, gain, wq, tm=512, tn=2048):
    T, D = h.shape
    N = wq.shape[1]
    tm, tn = _tile(T, tm), _tile(N, tn)
    return pl.pallas_call(
        _peer_qproj_body,
        grid=(T // tm, N // tn),
        in_specs=[pl.BlockSpec((tm, D), lambda i, j: (i, 0)),
                  pl.BlockSpec((1, D), lambda i, j: (0, 0)),
                  pl.BlockSpec((D, tn), lambda i, j: (0, j))],
        out_specs=[pl.BlockSpec((tm, D), lambda i, j: (i, 0)),
                   pl.BlockSpec((tm, tn), lambda i, j: (i, j))],
        out_shape=[jax.ShapeDtypeStruct((T, D), BF16), jax.ShapeDtypeStruct((T, N), BF16)],
        compiler_params=_params("parallel", "arbitrary"),
        name="peer_qproj",
    )(h, gain.reshape(1, D), wq)


TOPK_TOKENS = 512


def _key_scores(q_ref, k1, k2, s1_ref, s2_ref):
    half = k1.shape[1]
    s1_ref[...] = lax.dot_general(k1, q_ref[:, 0:half], NT_DIMS, preferred_element_type=F32)
    s2_ref[...] = lax.dot_general(k2, q_ref[:, half:2 * half], NT_DIMS, preferred_element_type=F32)


def _select_from_scores(s1_ref, s2_ref, gate_ref, e1_ref, e2_ref, *, K):
    NK = s1_ref.shape[0]
    tl = V7X_LANES
    key_iota = lax.broadcasted_iota(jnp.int32, (NK, tl), 0)
    groups = [slice(c * tl, (c + 1) * tl) for c in range(s1_ref.shape[1] // tl)]

    def emit(lanes, top):
        rows_s, _, top_s, top_key = top
        ex = jnp.exp(top_s - rows_s[0])
        gate_ref[:, lanes] = ex / jnp.sum(ex, axis=0, keepdims=True)
        e1_ref[:, lanes] = ((top_key >> 7) & 127).astype(F32)
        e2_ref[:, lanes] = (top_key & 127).astype(F32)

    stage1 = _topk_rows_distinct(
        [(ref[:, lanes], key_iota) for lanes in groups for ref in (s1_ref, s2_ref)], K)
    stage2 = _topk_rows_distinct(
        [_candidates(stage1[2 * c][0], stage1[2 * c + 1][0], K) for c in range(len(groups))], K)
    for c, lanes in enumerate(groups):
        emit(lanes, stage2[c][0])

    for c, lanes in enumerate(groups):
        any_tie = jnp.max((stage1[2 * c][1] | stage1[2 * c + 1][1] | stage2[c][1]).astype(jnp.int32))

        @pl.when(any_tie > 0)
        def _():
            top1, top2 = _topk_rows([(s1_ref[:, lanes], key_iota), (s2_ref[:, lanes], key_iota)], K)
            emit(lanes, _topk_rows([_candidates(top1, top2, K)], K)[0])


def _peer_topk_body(q_ref, k1_ref, k2_ref, gate_ref, e1_ref, e2_ref, s1_ref, s2_ref, *, K):
    hd = pl.program_id(1)
    _key_scores(q_ref, k1_ref[hd], k2_ref[hd], s1_ref, s2_ref)
    _select_from_scores(s1_ref, s2_ref, gate_ref, e1_ref, e2_ref, K=K)


def peer_topk(q, k1, k2):
    T = q.shape[0]
    PH, NK, HALF = k1.shape
    K = P_TOPK
    assert NK == V7X_LANES and HALF == V7X_LANES and q.shape[1] == PH * 2 * HALF
    tq = _tile(T, TOPK_TOKENS)
    k_spec = pl.BlockSpec((PH, NK, HALF), lambda c, hd: (0, 0, 0))
    sel_spec = pl.BlockSpec((K, tq), lambda c, hd: (hd, c))
    sel = jax.ShapeDtypeStruct((PH * K, T), F32)
    return pl.pallas_call(
        functools.partial(_peer_topk_body, K=K),
        grid=(T // tq, PH),
        in_specs=[pl.BlockSpec((tq, 2 * HALF), lambda c, hd: (c, hd)), k_spec, k_spec],
        out_specs=[sel_spec] * 3, out_shape=[sel] * 3,
        scratch_shapes=[pltpu.VMEM((NK, tq), F32)] * 2,
        compiler_params=_params("parallel", "arbitrary"),
        name="peer_topk",
    )(q, k1, k2)


SCATTER_PITCH_PAD = V7X_SUBLANES // 2
BF16_SUBLANES = 2 * V7X_SUBLANES


def _peer_scatter_body(gate_ref, e1_ref, e2_ref, w_ref, scr_ref, gt_ref, e1t_ref, *, NK):
    tmb, HK = gt_ref.shape
    pad = SCATTER_PITCH_PAD
    pitch = NK + pad
    gt_ref[...] = gate_ref[...].T
    e1t_ref[...] = e1_ref[...].T
    key_iota = lax.broadcasted_iota(jnp.int32, (NK, HK), 0).astype(F32)
    tall = NK + BF16_SUBLANES
    key_iota_odd = (lax.broadcasted_iota(jnp.int32, (tall, HK), 0) - pad).astype(F32)
    lane_iota = lax.broadcasted_iota(jnp.int32, (HK, NK), 1).astype(F32).astype(BF16)
    e2_cols = e2_ref[...].astype(BF16)
    one, zero = jnp.ones((), BF16), jnp.zeros((), BF16)
    for t in range(tmb):
        odd = t % 2
        g_row = gt_ref[t:t + 1, :]
        sel1 = jnp.where(e1t_ref[t:t + 1, :] == (key_iota_odd if odd else key_iota), g_row, 0.0)
        sel2t = jnp.where(e2_cols[:, t:t + 1] == lane_iota, one, zero)
        w_t = jnp.dot(sel1.astype(BF16), sel2t, preferred_element_type=F32)
        rows = NK + 2 * pad * odd
        start = t * pitch - pad * odd
        scr_ref[start:start + rows, :] = w_t[0:rows]
    for j in range(NK):
        w_ref[j] = scr_ref[pl.ds(j, tmb, stride=pitch), :].astype(w_ref.dtype)


def peer_scatter(gate, e1, e2, NK, tmb=128):
    HK, T = gate.shape
    tmb = _tile(T, tmb)
    assert tmb % 2 == 0
    spec = pl.BlockSpec((HK, tmb), lambda i: (0, i))
    return pl.pallas_call(
        functools.partial(_peer_scatter_body, NK=NK),
        grid=(T // tmb,),
        in_specs=[spec, spec, spec],
        out_specs=pl.BlockSpec((NK, tmb, NK), lambda i: (0, i, 0)),
        out_shape=jax.ShapeDtypeStruct((NK, T, NK), BF16),
        scratch_shapes=[pltpu.VMEM((tmb * (NK + SCATTER_PITCH_PAD), NK), F32),
                        pltpu.VMEM((tmb, HK), F32),
                        pltpu.VMEM((tmb, HK), F32)],
        compiler_params=_params("parallel"),
        name="peer_scatter",
    )(gate, e1, e2)


def _gelu(x):
    return 0.5 * x * (1.0 + lax.erf(x * (2.0 ** -0.5)))


def _peer_dense_body(xn_ref, u_ref, v_ref, w_ref, o_ref, *, NK):
    @pl.when(pl.program_id(1) == 0)
    def _():
        o_ref[...] = jnp.zeros_like(o_ref)

    act = lax.dot_general(xn_ref[...], u_ref[...], NT_DIMS, preferred_element_type=F32)
    coef = jnp.concatenate(
        [(w_ref[c].astype(F32) * _gelu(act[:, c * NK:(c + 1) * NK])).astype(BF16)
         for c in range(w_ref.shape[0])], axis=1)
    o_ref[...] += jnp.dot(coef, v_ref[...], preferred_element_type=F32)


def peer_dense(xn, u_all, v_all, layer, w, tm=1024, te=1024):
    T, D = xn.shape
    E = u_all.shape[1]
    NK = w.shape[0]
    tm, te = _tile(T, tm), _tile(E, te)
    assert te % NK == 0 and E == NK * NK
    return pl.pallas_call(
        functools.partial(_peer_dense_body, NK=NK),
        grid=(T // tm, E // te),
        in_specs=[pl.BlockSpec((tm, D), lambda i, j: (i, 0)),
                  pl.BlockSpec((None, te, D), lambda i, j: (layer, j, 0)),
                  pl.BlockSpec((None, te, D), lambda i, j: (layer, j, 0)),
                  pl.BlockSpec((te // NK, tm, NK), lambda i, j: (j, i, 0))],
        out_specs=pl.BlockSpec((tm, D), lambda i, j: (i, 0)),
        out_shape=jax.ShapeDtypeStruct((T, D), F32),
        compiler_params=pltpu.CompilerParams(dimension_semantics=("parallel", "arbitrary"),
                                             vmem_limit_bytes=VMEM_LIMIT_LARGE),
        name="peer_dense",
    )(xn, u_all, v_all, w)


def peer_layer(h, gain, wq, k1, k2, u_all, v_all, layer):
    xn, q = peer_qproj(h, gain, wq)
    gate, e1, e2 = peer_topk(q, k1, k2)
    w = peer_scatter(gate, e1, e2, k1.shape[1])
    return peer_dense(xn, u_all, v_all, layer, w)


def kernel(x, p, a_norm, a_w_in, a_gate_bias, a_head_norm, a_w_out, kv_norm, w_kv, b_norm, b_w_q, b_sinks,
           b_w_out, c_norm, peer_w_q, peer_k1, peer_k2, peer_u, peer_v, ple_norm, ple_w_gate, ple_w_proj,
           final_norm):
    B, S, D = x.shape
    T = B * S
    depth = p.shape[0]
    n_a = a_norm.shape[0]
    bf = lambda t: t.astype(BF16)
    h = x.reshape(T, D)
    u_all, v_all = bf(peer_u), bf(peer_v)
    p_all = p.reshape(depth, T, -1)
    for i in range(depth):
        if i < n_a:
            w_in = a_w_in[i]
            n_main = w_in.shape[1] - 2 * M_HEADS
            w_gates = jnp.pad(w_in[:, n_main:], ((0, 0), (0, V7X_LANES - 2 * M_HEADS)))
            proj, gates = rms_matmul(h, a_norm[i], bf(w_in), BF16, tm=1024, tn=1024, n_cols=n_main,
                                     side=(a_norm[i], bf(w_gates), F32))
            y = mlstm_scan(proj, gates, a_gate_bias[i], a_head_norm[i], B, S)
            h = matmul_residual(y, bf(a_w_out[i]), h)
        else:
            j = i - n_a
            if j == 0:
                q, kv = rms_matmul(h, b_norm[j], bf(b_w_q[j]), BF16, tm=512, tn=2048,
                                   side=(kv_norm, bf(w_kv), BF16))
            else:
                q = rms_matmul(h, b_norm[j], bf(b_w_q[j]), BF16, tm=512, tn=2048)
            o = swa_attention(q, kv, b_sinks[j], B, S)
            h = matmul_residual(o, bf(b_w_out[j]), h)
        peer_out = peer_layer(h, c_norm[i], bf(peer_w_q[i]), bf(peer_k1[i]), bf(peer_k2[i]), u_all, v_all, i)
        h = ple_gate(h, peer_out, ple_norm[i], bf(ple_w_gate[i]), p_all, i, bf(ple_w_proj[i]),
                     final_norm, final=(i == depth - 1))
    return h.reshape(B, S, D)
```

```python
import functools

import jax
import jax.numpy as jnp
from jax import lax
from jax.experimental import pallas as pl
from jax.experimental.pallas import tpu as pltpu

F32 = jnp.float32
BF16 = jnp.bfloat16

EPS = 1e-6
GATE_CAP = 15.0

M_HEADS = 4
A_HEAD_DIM = 64
A_GROUP = 8
WINDOW = 128
P_HEADS = 8
P_NKEYS = 128
P_TOPK = 16

V7X_LANES = 128
V7X_SUBLANES = 8
V7X_VMEM_BYTES = 64 * 1024 * 1024
VMEM_LIMIT = (V7X_VMEM_BYTES * 3) // 4
VMEM_LIMIT_LARGE = (V7X_VMEM_BYTES * 7) // 8

NT_DIMS = (((1,), (1,)), ((), ()))
TN_DIMS = (((0,), (0,)), ((), ()))


def _params(*sem):
    return pltpu.CompilerParams(dimension_semantics=sem, vmem_limit_bytes=VMEM_LIMIT)


def _rms(x, gain):
    ms = jnp.mean(x * x, axis=-1, keepdims=True)
    return x * lax.rsqrt(ms + EPS) * gain


def _tile(n, pref):
    t = min(n, pref)
    assert n % t == 0, (n, pref)
    return t


def _rms_matmul_body(x_ref, g_ref, w_ref, *rest, side):
    if side:
        gs_ref, ws_ref, o_ref, os_ref, xn_ref = rest
    else:
        o_ref, xn_ref = rest

    @pl.when(pl.program_id(1) == 0)
    def _():
        x = x_ref[...]
        y = x * lax.rsqrt(jnp.mean(x * x, axis=-1, keepdims=True) + EPS)
        xn_ref[...] = (y * g_ref[...]).astype(BF16)
        if side:
            os_ref[...] = jnp.dot((y * gs_ref[...]).astype(BF16), ws_ref[...],
                                  preferred_element_type=F32).astype(os_ref.dtype)

    o_ref[...] = jnp.dot(xn_ref[...], w_ref[...], preferred_element_type=F32).astype(o_ref.dtype)


def rms_matmul(x, gain, w, out_dtype, tm=512, tn=512, side=None, n_cols=None):
    T, D = x.shape
    N = w.shape[1] if n_cols is None else n_cols
    tm, tn = _tile(T, tm), _tile(N, tn)
    in_specs = [pl.BlockSpec((tm, D), lambda i, j: (i, 0)),
                pl.BlockSpec((1, D), lambda i, j: (0, 0)),
                pl.BlockSpec((D, tn), lambda i, j: (0, j))]
    out_specs = [pl.BlockSpec((tm, tn), lambda i, j: (i, j))]
    out_shape = [jax.ShapeDtypeStruct((T, N), out_dtype)]
    args = [x, gain.reshape(1, D), w]
    if side:
        gain_s, w_s, dtype_s = side
        NS = w_s.shape[1]
        in_specs += [pl.BlockSpec((1, D), lambda i, j: (0, 0)), pl.BlockSpec((D, NS), lambda i, j: (0, 0))]
        out_specs.append(pl.BlockSpec((tm, NS), lambda i, j: (i, 0)))
        out_shape.append(jax.ShapeDtypeStruct((T, NS), dtype_s))
        args += [gain_s.reshape(1, D), w_s]
    res = pl.pallas_call(
        functools.partial(_rms_matmul_body, side=bool(side)),
        grid=(T // tm, N // tn),
        in_specs=in_specs, out_specs=out_specs, out_shape=out_shape,
        scratch_shapes=[pltpu.VMEM((tm, D), BF16)],
        compiler_params=_params("parallel", "arbitrary"),
        name="rms_matmul",
    )(*args)
    return res if side else res[0]


def _matmul_residual_body(y_ref, w_ref, h_ref, g_ref, o_ref, xn_ref):
    out = h_ref[...] + jnp.dot(y_ref[...], w_ref[...], preferred_element_type=F32)
    o_ref[...] = out
    xn_ref[...] = _rms(out, g_ref[...]).astype(xn_ref.dtype)


def matmul_residual(y, w, h, next_gain, tm=512):
    T, K = y.shape
    N = w.shape[1]
    tm = _tile(T, tm)
    row = pl.BlockSpec((tm, N), lambda i: (i, 0))
    return pl.pallas_call(
        _matmul_residual_body,
        grid=(T // tm,),
        in_specs=[pl.BlockSpec((tm, K), lambda i: (i, 0)),
                  pl.BlockSpec((K, N), lambda i: (0, 0)),
                  row,
                  pl.BlockSpec((1, N), lambda i: (0, 0))],
        out_specs=[row, row],
        out_shape=[jax.ShapeDtypeStruct((T, N), F32), jax.ShapeDtypeStruct((T, N), BF16)],
        compiler_params=_params("parallel"),
        name="matmul_residual",
    )(y, w, h, next_gain.reshape(1, N))


def _ple_body(h_ref, d_ref, g_ref, wg_ref, p_ref, wp_ref, fn_ref, o_ref, *, final):
    h = h_ref[...] + d_ref[...]
    xn = _rms(h, g_ref[...]).astype(BF16)
    gate = jax.nn.sigmoid(jnp.dot(xn, wg_ref[...], preferred_element_type=F32))
    emb = jnp.dot(p_ref[...].astype(BF16), wp_ref[...], preferred_element_type=F32)
    out = h + gate * emb
    if final:
        out = _rms(out, fn_ref[...])
    o_ref[...] = out


def ple_gate(h, peer_out, gain, wg, p_all, layer, wp, final_gain, final, tm=256):
    T, D = h.shape
    PD = p_all.shape[2]
    tm = _tile(T, tm)
    return pl.pallas_call(
        functools.partial(_ple_body, final=final),
        grid=(T // tm,),
        in_specs=[pl.BlockSpec((tm, D), lambda i: (i, 0)),
                  pl.BlockSpec((tm, D), lambda i: (i, 0)),
                  pl.BlockSpec((1, D), lambda i: (0, 0)),
                  pl.BlockSpec((D, D), lambda i: (0, 0)),
                  pl.BlockSpec((None, tm, PD), lambda i: (layer, i, 0)),
                  pl.BlockSpec((PD, D), lambda i: (0, 0)),
                  pl.BlockSpec((1, D), lambda i: (0, 0))],
        out_specs=pl.BlockSpec((tm, D), lambda i: (i, 0)),
        out_shape=jax.ShapeDtypeStruct((T, D), F32),
        compiler_params=_params("parallel"),
        name="ple_gate",
    )(h, peer_out, gain.reshape(1, D), wg, p_all, wp, final_gain.reshape(1, D))


def _soft_cap(t):
    return GATE_CAP * jnp.tanh(t / GATE_CAP)


def _log_sigmoid(t):
    return jnp.minimum(t, 0.0) - jnp.log(1.0 + jnp.exp(-jnp.abs(t)))


def _split3(x):
    hi = x.astype(BF16)
    r = x - hi.astype(F32)
    mid = r.astype(BF16)
    lo = (r - mid.astype(F32)).astype(BF16)
    return hi, mid, lo


MLSTM_HEADS_PER_STEP = 4


def _mlstm_body(q_ref, k_ref, v_ref, og_ref, gc_ref, gr_ref, bc_ref, br_ref, hn_ref, o_ref, *state, L, DK, HP):
    @pl.when(pl.program_id(1) == 0)
    def _():
        for ref in state:
            ref[...] = jnp.zeros_like(ref)

    DV = o_ref.shape[1] // HP
    row = lax.broadcasted_iota(jnp.int32, (L, L), 0)
    col = lax.broadcasted_iota(jnp.int32, (L, L), 1)
    causal = row >= col
    tri_l = causal.astype(BF16)
    tri_u = (row <= col).astype(BF16)
    gate_cols = gc_ref[...] + bc_ref[...]
    for i in range(HP):
        dk, dv = slice(i * DK, (i + 1) * DK), slice(i * DV, (i + 1) * DV)
        _mlstm_head(q_ref.at[:, dk], k_ref.at[:, dk], v_ref.at[:, dv], og_ref.at[:, dv],
                    gate_cols[:, i:i + 1], gate_cols[:, HP + i:HP + i + 1], gr_ref.at[i], br_ref.at[i],
                    hn_ref.at[:, dv], o_ref.at[:, dv], *state[3 * i:3 * i + 3],
                    causal, tri_l, tri_u, L=L, DK=DK)


def _mlstm_head(q_ref, k_ref, v_ref, og_ref, gi_col, gf_col, gr_ref, br_ref, hn_ref, o_ref,
                c_ref, n_ref, m_ref, causal, tri_l, tri_u, *, L, DK):
    q = (q_ref[...].astype(F32) * (DK ** -0.5)).astype(BF16)
    k = k_ref[...]
    v = v_ref[...]

    gr = gr_ref[...] + br_ref[...]
    li_c = _soft_cap(gi_col)
    lf_c = _log_sigmoid(_soft_cap(gf_col))
    li_r = _soft_cap(gr[0:1, :])
    lf_r = _log_sigmoid(_soft_cap(gr[1:2, :]))

    b_c = sum(jnp.dot(tri_l, part, preferred_element_type=F32)
              for part in _split3(jnp.broadcast_to(lf_c, (L, V7X_LANES))))[:, 0:1]
    b_r = sum(jnp.dot(part, tri_u, preferred_element_type=F32)
              for part in _split3(jnp.broadcast_to(lf_r, (V7X_SUBLANES, L))))[0:1, :]

    m_prev = m_ref[0:1, 0:1]
    dmat = jnp.where(causal, b_c - b_r + li_r, -jnp.inf)
    inter = b_c + m_prev
    m_t = jnp.maximum(inter, jnp.max(dmat, axis=-1, keepdims=True))
    w_intra = jnp.exp(dmat - m_t)
    w_inter = jnp.exp(inter - m_t)

    s = lax.dot_general(q, k, NT_DIMS, preferred_element_type=F32) * w_intra
    c_old = c_ref[...]
    n_old = n_ref[...]
    num = (w_inter * jnp.dot(q, c_old.astype(BF16), preferred_element_type=F32)
           + jnp.dot(s.astype(BF16), v, preferred_element_type=F32))
    qn = lax.dot_general(q, jnp.broadcast_to(n_old, (V7X_LANES, DK)).astype(BF16), NT_DIMS,
                         preferred_element_type=F32)[:, 0:1]
    den = w_inter * qn + jnp.sum(s, axis=-1, keepdims=True)
    hh = num / jnp.maximum(jnp.abs(den), jnp.exp(-m_t))

    m_new = m_t[L - 1:L, :]
    b_last = b_c[L - 1:L, :]
    w_state = jnp.exp(b_last - b_c + li_c - m_new)
    decay = jnp.exp(b_last + m_prev - m_new)
    kw = k.astype(F32) * w_state
    c_ref[...] = decay * c_old + lax.dot_general(kw.astype(BF16), v, TN_DIMS, preferred_element_type=F32)
    n_ref[...] = decay * n_old + jnp.sum(kw, axis=0, keepdims=True)
    m_ref[...] = jnp.broadcast_to(m_new, m_ref.shape)

    ms = jnp.mean(hh * hh, axis=-1, keepdims=True)
    y = hh * lax.rsqrt(ms + EPS) * hn_ref[...] * jax.nn.sigmoid(og_ref[...].astype(F32))
    o_ref[...] = y.astype(o_ref.dtype)


def mlstm_scan(proj, gates, gate_bias, head_norm, B, S, chunk=256):
    T = proj.shape[0]
    H = M_HEADS
    HDV = head_norm.shape[0]
    DV = HDV // H
    DK = (proj.shape[1] - 2 * HDV) // (2 * H)
    L = _tile(S, chunk)
    NC = S // L
    assert DV % DK == 0 or DK % DV == 0
    k_off = (H * DK) // DK
    v_off = (2 * H * DK) // DV
    og_off = (2 * H * DK + H * DV) // DV
    GW = gates.shape[1]
    g_row = gates[:, :2 * H].reshape(T, 2, H).transpose(2, 1, 0)
    b_col = jnp.pad(gate_bias.reshape(1, 2 * H), ((0, 0), (0, GW - 2 * H)))
    b_row = gate_bias.T.reshape(H, 2, 1)

    HP = MLSTM_HEADS_PER_STEP
    assert H == HP and k_off % HP == 0 and v_off % HP == 0 and og_off % HP == 0
    groups = H // HP

    def tok(g, c):
        return (g // groups) * NC + c

    return pl.pallas_call(
        functools.partial(_mlstm_body, L=L, DK=DK, HP=HP),
        grid=(B * groups, NC),
        in_specs=[pl.BlockSpec((L, HP * DK), lambda g, c: (tok(g, c), g % groups)),
                  pl.BlockSpec((L, HP * DK), lambda g, c: (tok(g, c), k_off // HP + g % groups)),
                  pl.BlockSpec((L, HP * DV), lambda g, c: (tok(g, c), v_off // HP + g % groups)),
                  pl.BlockSpec((L, HP * DV), lambda g, c: (tok(g, c), og_off // HP + g % groups)),
                  pl.BlockSpec((L, GW), lambda g, c: (tok(g, c), 0)),
                  pl.BlockSpec((HP, 2, L), lambda g, c: (g % groups, 0, tok(g, c))),
                  pl.BlockSpec((1, GW), lambda g, c: (0, 0)),
                  pl.BlockSpec((HP, 2, 1), lambda g, c: (g % groups, 0, 0)),
                  pl.BlockSpec((1, HP * DV), lambda g, c: (0, g % groups))],
        out_specs=pl.BlockSpec((L, HP * DV), lambda g, c: (tok(g, c), g % groups)),
        out_shape=jax.ShapeDtypeStruct((T, HDV), BF16),
        scratch_shapes=[pltpu.VMEM((DK, DV), F32),
                        pltpu.VMEM((1, DK), F32),
                        pltpu.VMEM((V7X_SUBLANES, V7X_LANES), F32)] * HP,
        compiler_params=_params("parallel", "arbitrary"),
        name="mlstm_scan",
    )(proj, proj, proj, proj, gates, g_row, b_col, b_row, head_norm.reshape(1, HDV))


def _swa_body(sink_ref, q_ref, kp_ref, kc_ref, o_ref, *, BLK, KVH, GROUP, HD):
    first = pl.program_id(1) == 0
    ri = lax.broadcasted_iota(jnp.int32, (BLK, BLK), 0)
    ci = lax.broadcasted_iota(jnp.int32, (BLK, BLK), 1)
    from_cur = ci <= ri
    prev_bias = jnp.where(jnp.logical_and(first, ci > ri), -jnp.inf, 0.0)
    lo = lax.broadcasted_iota(jnp.int32, (1, 2 * HD), 1) < HD
    scale = HD ** -0.5

    def head_pair_operands(kv, pair):
        xk = kv[:, pair * 2 * HD:(pair + 1) * 2 * HD]
        xv = kv[:, (KVH + pair * 2) * HD:(KVH + pair * 2 + 2) * HD]
        xk_r = pltpu.roll(xk, HD, 1)
        xv_r = pltpu.roll(xv, HD, 1)
        head0 = (jnp.where(lo, xk, xk_r), jnp.where(lo, xv, 0.0), jnp.where(lo, 0.0, xv_r))
        head1 = (jnp.where(lo, xk_r, xk), jnp.where(lo, xv_r, 0.0), jnp.where(lo, 0.0, xv))
        return [tuple(t.astype(BF16) for t in head) for head in (head0, head1)]

    kv = jnp.concatenate([kp_ref[...], kc_ref[...]], axis=0).astype(F32)
    for pair in range(KVH // 2):
        operands = head_pair_operands(kv, pair)
        for sub in range(2):
            kh = 2 * pair + sub
            k2, *v_halves = operands[sub]
            for gp in range(GROUP // 2):
                h0 = kh * GROUP + 2 * gp
                q2 = (q_ref[:, h0 * HD:(h0 + 2) * HD].astype(F32) * scale).astype(BF16)
                acc = None
                for e in range(2):
                    qm = jnp.where(lo if e == 0 else jnp.logical_not(lo), q2, jnp.zeros_like(q2))
                    s2 = lax.dot_general(qm, k2, NT_DIMS, preferred_element_type=F32)
                    s = jnp.where(from_cur, s2[:, BLK:], s2[:, :BLK] + prev_bias)
                    sink = sink_ref[h0 + e]
                    mx = jnp.maximum(jnp.max(s, axis=-1, keepdims=True), sink)
                    pr = jnp.exp(s - mx)
                    den = jnp.sum(pr, axis=-1, keepdims=True) + jnp.exp(sink - mx)
                    p2 = jnp.concatenate([jnp.where(from_cur, 0.0, pr), jnp.where(from_cur, pr, 0.0)], axis=1)
                    o = jnp.dot(p2.astype(BF16), v_halves[e], preferred_element_type=F32) / den
                    acc = o if acc is None else acc + o
                o_ref[:, h0 * HD:(h0 + 2) * HD] = acc.astype(o_ref.dtype)


def swa_attention(q, kv, sinks, B, S):
    T, QD = q.shape
    HD, GROUP, BLK = A_HEAD_DIM, A_GROUP, WINDOW
    KVH = kv.shape[1] // (2 * HD)
    assert QD == KVH * GROUP * HD and 2 * HD == V7X_LANES and KVH % 2 == 0 and GROUP % 2 == 0
    assert HD in (4, 16, 64, 256)
    NB = S // BLK
    return pl.pallas_call(
        functools.partial(_swa_body, BLK=BLK, KVH=KVH, GROUP=GROUP, HD=HD),
        grid=(B, NB),
        in_specs=[pl.BlockSpec(memory_space=pltpu.SMEM),
                  pl.BlockSpec((BLK, QD), lambda b, n: (b * NB + n, 0)),
                  pl.BlockSpec((BLK, 2 * KVH * HD), lambda b, n: (b * NB + jnp.maximum(n - 1, 0), 0)),
                  pl.BlockSpec((BLK, 2 * KVH * HD), lambda b, n: (b * NB + n, 0))],
        out_specs=pl.BlockSpec((BLK, QD), lambda b, n: (b * NB + n, 0)),
        out_shape=jax.ShapeDtypeStruct((T, QD), BF16),
        compiler_params=_params("parallel", "arbitrary"),
        name="swa_attention",
    )(sinks, q, kv, kv)


KEY_BIG = 1 << 30


def _topk_rows(problems, k):
    tm = problems[0][0].shape[1]
    slot = lax.broadcasted_iota(jnp.int32, (k, tm), 0)
    state = [[vals, keys, [], [], jnp.zeros((k, tm), vals.dtype), jnp.zeros((k, tm), keys.dtype)]
             for vals, keys in problems]
    for it in range(k):
        for st in state:
            vals, keys = st[0], st[1]
            m = jnp.max(vals, axis=0, keepdims=True)
            kmin = jnp.min(jnp.where(vals == m, keys, KEY_BIG), axis=0, keepdims=True)
            st[0] = jnp.where(keys == kmin, -jnp.inf, vals)
            st[2].append(m)
            st[3].append(kmin)
            st[4] = jnp.where(slot == it, m, st[4])
            st[5] = jnp.where(slot == it, kmin, st[5])
    return [tuple(st[2:]) for st in state]


def _oddeven_merge_sort(n):
    pairs = []
    p = 1
    while p < n:
        k = p
        while k >= 1:
            for j in range(k % p, n - k, 2 * k):
                for i in range(min(k, n - j - k)):
                    if (i + j) // (2 * p) == (i + j + k) // (2 * p):
                        pairs.append((i + j, i + j + k))
            k //= 2
        p *= 2
    return pairs


def _topk_rows_distinct(problems, k):
    sub = V7X_SUBLANES
    tl = problems[0][0].shape[1]
    slot = lax.broadcasted_iota(jnp.int32, (k, tl), 0)
    state = []
    for vals, keys in problems:
        n = vals.shape[0] // sub
        v = [vals[j * sub:(j + 1) * sub] for j in range(n)]
        q = [keys[j * sub:(j + 1) * sub] for j in range(n)]
        size = 1
        while size < n:
            size *= 2
        for i, j in _oddeven_merge_sort(size):
            if j < n:
                first = v[i] >= v[j]
                v[i], v[j] = jnp.maximum(v[i], v[j]), jnp.minimum(v[i], v[j])
                q[i], q[j] = jnp.where(first, q[i], q[j]), jnp.where(first, q[j], q[i])
        state.append(dict(v=v, q=q, n=n, rows_v=[], rows_k=[], arr_v=jnp.zeros((k, tl), vals.dtype),
                          arr_k=jnp.zeros((k, tl), keys.dtype), tie=jnp.zeros((1, tl), jnp.bool_),
                          pops=jnp.zeros((sub, tl), F32)))
    for it in range(k):
        for st in state:
            v, q, n = st["v"], st["q"], st["n"]
            m = jnp.max(v[0], axis=0, keepdims=True)
            pop = v[0] == m
            key = jnp.max(jnp.where(pop, q[0], -1), axis=0, keepdims=True)
            st["pops"] = st["pops"] + pop.astype(F32)
            if st["rows_v"]:
                st["tie"] = st["tie"] | (m == st["rows_v"][-1])
            depth = min(n, k - it)
            for j in range(depth):
                v[j] = jnp.where(pop, v[j + 1] if j + 1 < n else -jnp.inf, v[j])
                if j + 1 < n:
                    q[j] = jnp.where(pop, q[j + 1], q[j])
            st["rows_v"].append(m)
            st["rows_k"].append(key)
            st["arr_v"] = jnp.where(slot == it, m, st["arr_v"])
            st["arr_k"] = jnp.where(slot == it, key, st["arr_k"])
    out = []
    for st in state:
        tie = (st["tie"] | (jnp.max(st["v"][0], axis=0, keepdims=True) == st["rows_v"][-1])
               | (jnp.sum(st["pops"], axis=0, keepdims=True) != float(k)))
        out.append(((st["rows_v"], st["rows_k"], st["arr_v"], st["arr_k"]), tie))
    return out


def _candidates(top1, top2, k):
    r1_v, r1_i, v1, i1 = top1
    r2_v, r2_i, v2, i2 = top2
    tm = v1.shape[1]
    sub = V7X_SUBLANES
    vals, keys = [], []

    def pack(code, e1, e2):
        return (code << 14) | (e1 << 7) | e2

    b_iota = lax.broadcasted_iota(jnp.int32, (k, tm), 0)
    vals.append(r1_v[0] + v2)
    keys.append(pack(b_iota, r1_i[0], i2))
    a = 1
    while a < k and k // (a + 1) >= 2:
        nb = k // (a + 1)
        rows = -(-nb // sub) * sub
        bi = lax.broadcasted_iota(jnp.int32, (rows, tm), 0)
        vals.append(jnp.where(bi < nb, r1_v[a] + v2[0:rows], -jnp.inf))
        keys.append(pack(a * k + bi, r1_i[a], i2[0:rows]))
        a += 1
    if a < k:
        a0 = (a // sub) * sub
        ai = lax.broadcasted_iota(jnp.int32, (k - a0, tm), 0) + a0
        vals.append(jnp.where(ai >= a, v1[a0:k] + r2_v[0], -jnp.inf))
        keys.append(pack(ai * k, i1[a0:k], r2_i[0]))
    return jnp.concatenate(vals, axis=0), jnp.concatenate(keys, axis=0)


def _peer_qproj_body(xn_ref, w_ref, q_ref):
    q_ref[...] = jnp.dot(xn_ref[...], w_ref[...], preferred_element_type=F32).astype(q_ref.dtype)


def peer_qproj(xn, wq, tm=1024):
    T, D = xn.shape
    N = wq.shape[1]
    tm = _tile(T, tm)
    return pl.pallas_call(
        _peer_qproj_body,
        grid=(T // tm,),
        in_specs=[pl.BlockSpec((tm, D), lambda i: (i, 0)),
                  pl.BlockSpec((D, N), lambda i: (0, 0))],
        out_specs=pl.BlockSpec((tm, N), lambda i: (i, 0)),
        out_shape=jax.ShapeDtypeStruct((T, N), BF16),
        compiler_params=_params("parallel"),
        name="peer_qproj",
    )(xn, wq)


TOPK_TOKENS = 512


def _key_scores(q_ref, k1, k2, s1_ref, s2_ref):
    half = k1.shape[1]
    s1_ref[...] = lax.dot_general(k1, q_ref[:, 0:half], NT_DIMS, preferred_element_type=F32)
    s2_ref[...] = lax.dot_general(k2, q_ref[:, half:2 * half], NT_DIMS, preferred_element_type=F32)


def _select_from_scores(s1_ref, s2_ref, gate_ref, e1_ref, e2_ref, *, K):
    NK = s1_ref.shape[0]
    tl = V7X_LANES
    key_iota = lax.broadcasted_iota(jnp.int32, (NK, tl), 0)
    groups = [slice(c * tl, (c + 1) * tl) for c in range(s1_ref.shape[1] // tl)]

    def emit(lanes, top):
        rows_s, _, top_s, top_key = top
        ex = jnp.exp(top_s - rows_s[0])
        gate_ref[:, lanes] = ex / jnp.sum(ex, axis=0, keepdims=True)
        e1_ref[:, lanes] = ((top_key >> 7) & 127).astype(F32)
        e2_ref[:, lanes] = (top_key & 127).astype(F32)

    stage1 = _topk_rows_distinct(
        [(ref[:, lanes], key_iota) for lanes in groups for ref in (s1_ref, s2_ref)], K)
    stage2 = _topk_rows_distinct(
        [_candidates(stage1[2 * c][0], stage1[2 * c + 1][0], K) for c in range(len(groups))], K)
    for c, lanes in enumerate(groups):
        emit(lanes, stage2[c][0])

    for c, lanes in enumerate(groups):
        any_tie = jnp.max((stage1[2 * c][1] | stage1[2 * c + 1][1] | stage2[c][1]).astype(jnp.int32))

        @pl.when(any_tie > 0)
        def _():
            top1, top2 = _topk_rows([(s1_ref[:, lanes], key_iota), (s2_ref[:, lanes], key_iota)], K)
            emit(lanes, _topk_rows([_candidates(top1, top2, K)], K)[0])


def _peer_topk_body(q_ref, k1_ref, k2_ref, gate_ref, e1_ref, e2_ref, s1_ref, s2_ref, *, K):
    hd = pl.program_id(1)
    _key_scores(q_ref, k1_ref[hd], k2_ref[hd], s1_ref, s2_ref)
    _select_from_scores(s1_ref, s2_ref, gate_ref, e1_ref, e2_ref, K=K)


def peer_topk(q, k1, k2):
    T = q.shape[0]
    PH, NK, HALF = k1.shape
    K = P_TOPK
    assert NK == V7X_LANES and HALF == V7X_LANES and q.shape[1] == PH * 2 * HALF
    tq = _tile(T, TOPK_TOKENS)
    k_spec = pl.BlockSpec((PH, NK, HALF), lambda c, hd: (0, 0, 0))
    sel_spec = pl.BlockSpec((K, tq), lambda c, hd: (hd, c))
    sel = jax.ShapeDtypeStruct((PH * K, T), F32)
    return pl.pallas_call(
        functools.partial(_peer_topk_body, K=K),
        grid=(T // tq, PH),
        in_specs=[pl.BlockSpec((tq, 2 * HALF), lambda c, hd: (c, hd)), k_spec, k_spec],
        out_specs=[sel_spec] * 3, out_shape=[sel] * 3,
        scratch_shapes=[pltpu.VMEM((NK, tq), F32)] * 2,
        compiler_params=_params("parallel", "arbitrary"),
        name="peer_topk",
    )(q, k1, k2)


SCATTER_PITCH_PAD = V7X_SUBLANES // 2
BF16_SUBLANES = 2 * V7X_SUBLANES


def _peer_scatter_body(gate_ref, e1_ref, e2_ref, w_ref, scr_ref, gt_ref, e1t_ref, *, NK):
    tmb, HK = gt_ref.shape
    pad = SCATTER_PITCH_PAD
    pitch = NK + pad
    gt_ref[...] = gate_ref[...].T
    e1t_ref[...] = e1_ref[...].T
    key_iota = lax.broadcasted_iota(jnp.int32, (NK, HK), 0).astype(F32)
    tall = NK + BF16_SUBLANES
    key_iota_odd = (lax.broadcasted_iota(jnp.int32, (tall, HK), 0) - pad).astype(F32)
    lane_iota = lax.broadcasted_iota(jnp.int32, (HK, NK), 1).astype(F32).astype(BF16)
    e2_cols = e2_ref[...].astype(BF16)
    one, zero = jnp.ones((), BF16), jnp.zeros((), BF16)
    for t in range(tmb):
        odd = t % 2
        g_row = gt_ref[t:t + 1, :]
        sel1 = jnp.where(e1t_ref[t:t + 1, :] == (key_iota_odd if odd else key_iota), g_row, 0.0)
        sel2t = jnp.where(e2_cols[:, t:t + 1] == lane_iota, one, zero)
        w_t = jnp.dot(sel1.astype(BF16), sel2t, preferred_element_type=F32)
        rows = NK + 2 * pad * odd
        start = t * pitch - pad * odd
        scr_ref[start:start + rows, :] = w_t[0:rows]
    for j in range(NK):
        w_ref[j] = scr_ref[pl.ds(j, tmb, stride=pitch), :].astype(w_ref.dtype)


def peer_scatter(gate, e1, e2, NK, tmb=128):
    HK, T = gate.shape
    tmb = _tile(T, tmb)
    assert tmb % 2 == 0
    spec = pl.BlockSpec((HK, tmb), lambda i: (0, i))
    return pl.pallas_call(
        functools.partial(_peer_scatter_body, NK=NK),
        grid=(T // tmb,),
        in_specs=[spec, spec, spec],
        out_specs=pl.BlockSpec((NK, tmb, NK), lambda i: (0, i, 0)),
        out_shape=jax.ShapeDtypeStruct((NK, T, NK), BF16),
        scratch_shapes=[pltpu.VMEM((tmb * (NK + SCATTER_PITCH_PAD), NK), F32),
                        pltpu.VMEM((tmb, HK), F32),
                        pltpu.VMEM((tmb, HK), F32)],
        compiler_params=_params("parallel"),
        name="peer_scatter",
    )(gate, e1, e2)


def _gelu(x):
    return 0.5 * x * (1.0 + lax.erf(x * (2.0 ** -0.5)))


def _peer_dense_body(xn_ref, u_ref, v_ref, w_ref, o_ref, *, NK):
    @pl.when(pl.program_id(1) == 0)
    def _():
        o_ref[...] = jnp.zeros_like(o_ref)

    act = lax.dot_general(xn_ref[...], u_ref[...], NT_DIMS, preferred_element_type=F32)
    coef = jnp.concatenate(
        [(w_ref[c].astype(F32) * _gelu(act[:, c * NK:(c + 1) * NK])).astype(BF16)
         for c in range(w_ref.shape[0])], axis=1)
    o_ref[...] += jnp.dot(coef, v_ref[...], preferred_element_type=F32)


def peer_dense(xn, u_all, v_all, layer, w, tm=1024, te=1024):
    T, D = xn.shape
    E = u_all.shape[1]
    NK = w.shape[0]
    tm, te = _tile(T, tm), _tile(E, te)
    assert te % NK == 0 and E == NK * NK
    return pl.pallas_call(
        functools.partial(_peer_dense_body, NK=NK),
        grid=(T // tm, E // te),
        in_specs=[pl.BlockSpec((tm, D), lambda i, j: (i, 0)),
                  pl.BlockSpec((None, te, D), lambda i, j: (layer, j, 0)),
                  pl.BlockSpec((None, te, D), lambda i, j: (layer, j, 0)),
                  pl.BlockSpec((te // NK, tm, NK), lambda i, j: (j, i, 0))],
        out_specs=pl.BlockSpec((tm, D), lambda i, j: (i, 0)),
        out_shape=jax.ShapeDtypeStruct((T, D), F32),
        compiler_params=pltpu.CompilerParams(dimension_semantics=("parallel", "arbitrary"),
                                             vmem_limit_bytes=VMEM_LIMIT_LARGE),
        name="peer_dense",
    )(xn, u_all, v_all, w)


def peer_layer(xn, wq, k1, k2, u_all, v_all, layer):
    q = peer_qproj(xn, wq)
    gate, e1, e2 = peer_topk(q, k1, k2)
    w = peer_scatter(gate, e1, e2, k1.shape[1])
    return peer_dense(xn, u_all, v_all, layer, w)


def kernel(x, p, a_norm, a_w_in, a_gate_bias, a_head_norm, a_w_out, kv_norm, w_kv, b_norm, b_w_q, b_sinks,
           b_w_out, c_norm, peer_w_q, peer_k1, peer_k2, peer_u, peer_v, ple_norm, ple_w_gate, ple_w_proj,
           final_norm):
    B, S, D = x.shape
    T = B * S
    depth = p.shape[0]
    n_a = a_norm.shape[0]
    bf = lambda t: t.astype(BF16)
    h = x.reshape(T, D)
    u_all, v_all = bf(peer_u), bf(peer_v)
    p_all = p.reshape(depth, T, -1)
    for i in range(depth):
        if i < n_a:
            w_in = a_w_in[i]
            n_main = w_in.shape[1] - 2 * M_HEADS
            w_gates = jnp.pad(w_in[:, n_main:], ((0, 0), (0, V7X_LANES - 2 * M_HEADS)))
            proj, gates = rms_matmul(h, a_norm[i], bf(w_in), BF16, tm=1024, tn=1024, n_cols=n_main,
                                     side=(a_norm[i], bf(w_gates), F32))
            y = mlstm_scan(proj, gates, a_gate_bias[i], a_head_norm[i], B, S)
            h, xn = matmul_residual(y, bf(a_w_out[i]), h, c_norm[i])
        else:
            j = i - n_a
            if j == 0:
                q, kv = rms_matmul(h, b_norm[j], bf(b_w_q[j]), BF16, tm=512, tn=2048,
                                   side=(kv_norm, bf(w_kv), BF16))
            else:
                q = rms_matmul(h, b_norm[j], bf(b_w_q[j]), BF16, tm=512, tn=2048)
            o = swa_attention(q, kv, b_sinks[j], B, S)
            h, xn = matmul_residual(o, bf(b_w_out[j]), h, c_norm[i])
        peer_out = peer_layer(xn, bf(peer_w_q[i]), bf(peer_k1[i]), bf(peer_k2[i]), u_all, v_all, i)
        h = ple_gate(h, peer_out, ple_norm[i], bf(ple_w_gate[i]), p_all, i, bf(ple_w_proj[i]),
                     final_norm, final=(i == depth - 1))
    return h.reshape(B, S, D)
```

```python
import functools

import jax
import jax.numpy as jnp
from jax import lax
from jax.experimental import pallas as pl
from jax.experimental.pallas import tpu as pltpu

F32 = jnp.float32
BF16 = jnp.bfloat16

EPS = 1e-6
GATE_CAP = 15.0

M_HEADS = 4
A_HEAD_DIM = 64
A_GROUP = 8
WINDOW = 128
P_HEADS = 8
P_NKEYS = 128
P_TOPK = 16

V7X_LANES = 128
V7X_SUBLANES = 8
V7X_VMEM_BYTES = 64 * 1024 * 1024
VMEM_LIMIT = (V7X_VMEM_BYTES * 3) // 4
VMEM_LIMIT_LARGE = (V7X_VMEM_BYTES * 7) // 8

NT_DIMS = (((1,), (1,)), ((), ()))
TN_DIMS = (((0,), (0,)), ((), ()))


def _params(*sem):
    return pltpu.CompilerParams(dimension_semantics=sem, vmem_limit_bytes=VMEM_LIMIT)


def _rms(x, gain):
    ms = jnp.mean(x * x, axis=-1, keepdims=True)
    return x * lax.rsqrt(ms + EPS) * gain


def _tile(n, pref):
    t = min(n, pref)
    assert n % t == 0, (n, pref)
    return t


def _rms_matmul_body(x_ref, g_ref, w_ref, *rest, side):
    if side:
        gs_ref, ws_ref, o_ref, os_ref, xn_ref = rest
    else:
        o_ref, xn_ref = rest

    @pl.when(pl.program_id(1) == 0)
    def _():
        x = x_ref[...]
        y = x * lax.rsqrt(jnp.mean(x * x, axis=-1, keepdims=True) + EPS)
        xn_ref[...] = (y * g_ref[...]).astype(BF16)
        if side:
            os_ref[...] = jnp.dot((y * gs_ref[...]).astype(BF16), ws_ref[...],
                                  preferred_element_type=F32).astype(os_ref.dtype)

    o_ref[...] = jnp.dot(xn_ref[...], w_ref[...], preferred_element_type=F32).astype(o_ref.dtype)


def rms_matmul(x, gain, w, out_dtype, tm=512, tn=512, side=None, n_cols=None):
    T, D = x.shape
    N = w.shape[1] if n_cols is None else n_cols
    tm, tn = _tile(T, tm), _tile(N, tn)
    in_specs = [pl.BlockSpec((tm, D), lambda i, j: (i, 0)),
                pl.BlockSpec((1, D), lambda i, j: (0, 0)),
                pl.BlockSpec((D, tn), lambda i, j: (0, j))]
    out_specs = [pl.BlockSpec((tm, tn), lambda i, j: (i, j))]
    out_shape = [jax.ShapeDtypeStruct((T, N), out_dtype)]
    args = [x, gain.reshape(1, D), w]
    if side:
        gain_s, w_s, dtype_s = side
        NS = w_s.shape[1]
        in_specs += [pl.BlockSpec((1, D), lambda i, j: (0, 0)), pl.BlockSpec((D, NS), lambda i, j: (0, 0))]
        out_specs.append(pl.BlockSpec((tm, NS), lambda i, j: (i, 0)))
        out_shape.append(jax.ShapeDtypeStruct((T, NS), dtype_s))
        args += [gain_s.reshape(1, D), w_s]
    res = pl.pallas_call(
        functools.partial(_rms_matmul_body, side=bool(side)),
        grid=(T // tm, N // tn),
        in_specs=in_specs, out_specs=out_specs, out_shape=out_shape,
        scratch_shapes=[pltpu.VMEM((tm, D), BF16)],
        compiler_params=_params("parallel", "arbitrary"),
        name="rms_matmul",
    )(*args)
    return res if side else res[0]


def _matmul_residual_body(y_ref, w_ref, h_ref, o_ref):
    o_ref[...] = h_ref[...] + jnp.dot(y_ref[...], w_ref[...], preferred_element_type=F32)


def matmul_residual(y, w, h, tm=512, tn=2048):
    T, K = y.shape
    N = w.shape[1]
    tm, tn = _tile(T, tm), _tile(N, tn)
    return pl.pallas_call(
        _matmul_residual_body,
        grid=(T // tm, N // tn),
        in_specs=[pl.BlockSpec((tm, K), lambda i, j: (i, 0)),
                  pl.BlockSpec((K, tn), lambda i, j: (0, j)),
                  pl.BlockSpec((tm, tn), lambda i, j: (i, j))],
        out_specs=pl.BlockSpec((tm, tn), lambda i, j: (i, j)),
        out_shape=jax.ShapeDtypeStruct((T, N), F32),
        compiler_params=_params("parallel", "arbitrary"),
        name="matmul_residual",
    )(y, w, h)


def _ple_body(h_ref, d_ref, g_ref, wg_ref, p_ref, wp_ref, fn_ref, o_ref, *, final):
    h = h_ref[...] + d_ref[...]
    xn = _rms(h, g_ref[...]).astype(BF16)
    gate = jax.nn.sigmoid(jnp.dot(xn, wg_ref[...], preferred_element_type=F32))
    emb = jnp.dot(p_ref[...].astype(BF16), wp_ref[...], preferred_element_type=F32)
    out = h + gate * emb
    if final:
        out = _rms(out, fn_ref[...])
    o_ref[...] = out


def ple_gate(h, peer_out, gain, wg, p_all, layer, wp, final_gain, final, tm=256):
    T, D = h.shape
    PD = p_all.shape[2]
    tm = _tile(T, tm)
    return pl.pallas_call(
        functools.partial(_ple_body, final=final),
        grid=(T // tm,),
        in_specs=[pl.BlockSpec((tm, D), lambda i: (i, 0)),
                  pl.BlockSpec((tm, D), lambda i: (i, 0)),
                  pl.BlockSpec((1, D), lambda i: (0, 0)),
                  pl.BlockSpec((D, D), lambda i: (0, 0)),
                  pl.BlockSpec((None, tm, PD), lambda i: (layer, i, 0)),
                  pl.BlockSpec((PD, D), lambda i: (0, 0)),
                  pl.BlockSpec((1, D), lambda i: (0, 0))],
        out_specs=pl.BlockSpec((tm, D), lambda i: (i, 0)),
        out_shape=jax.ShapeDtypeStruct((T, D), F32),
        compiler_params=_params("parallel"),
        name="ple_gate",
    )(h, peer_out, gain.reshape(1, D), wg, p_all, wp, final_gain.reshape(1, D))


def _soft_cap(t):
    return GATE_CAP * jnp.tanh(t / GATE_CAP)


def _log_sigmoid(t):
    return jnp.minimum(t, 0.0) - jnp.log(1.0 + jnp.exp(-jnp.abs(t)))


def _split3(x):
    hi = x.astype(BF16)
    r = x - hi.astype(F32)
    mid = r.astype(BF16)
    lo = (r - mid.astype(F32)).astype(BF16)
    return hi, mid, lo


MLSTM_HEADS_PER_STEP = 4


def _mlstm_body(q_ref, k_ref, v_ref, og_ref, gc_ref, gr_ref, bc_ref, br_ref, hn_ref, o_ref, *state, L, DK, HP):
    @pl.when(pl.program_id(1) == 0)
    def _():
        for ref in state:
            ref[...] = jnp.zeros_like(ref)

    DV = o_ref.shape[1] // HP
    row = lax.broadcasted_iota(jnp.int32, (L, L), 0)
    col = lax.broadcasted_iota(jnp.int32, (L, L), 1)
    causal = row >= col
    tri_l = causal.astype(BF16)
    tri_u = (row <= col).astype(BF16)
    gate_cols = gc_ref[...] + bc_ref[...]
    for i in range(HP):
        dk, dv = slice(i * DK, (i + 1) * DK), slice(i * DV, (i + 1) * DV)
        _mlstm_head(q_ref.at[:, dk], k_ref.at[:, dk], v_ref.at[:, dv], og_ref.at[:, dv],
                    gate_cols[:, i:i + 1], gate_cols[:, HP + i:HP + i + 1], gr_ref.at[i], br_ref.at[i],
                    hn_ref.at[:, dv], o_ref.at[:, dv], *state[3 * i:3 * i + 3],
                    causal, tri_l, tri_u, L=L, DK=DK)


def _mlstm_head(q_ref, k_ref, v_ref, og_ref, gi_col, gf_col, gr_ref, br_ref, hn_ref, o_ref,
                c_ref, n_ref, m_ref, causal, tri_l, tri_u, *, L, DK):
    q = (q_ref[...].astype(F32) * (DK ** -0.5)).astype(BF16)
    k = k_ref[...]
    v = v_ref[...]

    gr = gr_ref[...] + br_ref[...]
    li_c = _soft_cap(gi_col)
    lf_c = _log_sigmoid(_soft_cap(gf_col))
    li_r = _soft_cap(gr[0:1, :])
    lf_r = _log_sigmoid(_soft_cap(gr[1:2, :]))

    b_c = sum(jnp.dot(tri_l, part, preferred_element_type=F32)
              for part in _split3(jnp.broadcast_to(lf_c, (L, V7X_LANES))))[:, 0:1]
    b_r = sum(jnp.dot(part, tri_u, preferred_element_type=F32)
              for part in _split3(jnp.broadcast_to(lf_r, (V7X_SUBLANES, L))))[0:1, :]

    m_prev = m_ref[0:1, 0:1]
    dmat = jnp.where(causal, b_c - b_r + li_r, -jnp.inf)
    inter = b_c + m_prev
    m_t = jnp.maximum(inter, jnp.max(dmat, axis=-1, keepdims=True))
    w_intra = jnp.exp(dmat - m_t)
    w_inter = jnp.exp(inter - m_t)

    s = lax.dot_general(q, k, NT_DIMS, preferred_element_type=F32) * w_intra
    c_old = c_ref[...]
    n_old = n_ref[...]
    num = (w_inter * jnp.dot(q, c_old.astype(BF16), preferred_element_type=F32)
           + jnp.dot(s.astype(BF16), v, preferred_element_type=F32))
    qn = lax.dot_general(q, jnp.broadcast_to(n_old, (V7X_LANES, DK)).astype(BF16), NT_DIMS,
                         preferred_element_type=F32)[:, 0:1]
    den = w_inter * qn + jnp.sum(s, axis=-1, keepdims=True)
    hh = num / jnp.maximum(jnp.abs(den), jnp.exp(-m_t))

    m_new = m_t[L - 1:L, :]
    b_last = b_c[L - 1:L, :]
    w_state = jnp.exp(b_last - b_c + li_c - m_new)
    decay = jnp.exp(b_last + m_prev - m_new)
    kw = k.astype(F32) * w_state
    c_ref[...] = decay * c_old + lax.dot_general(kw.astype(BF16), v, TN_DIMS, preferred_element_type=F32)
    n_ref[...] = decay * n_old + jnp.sum(kw, axis=0, keepdims=True)
    m_ref[...] = jnp.broadcast_to(m_new, m_ref.shape)

    ms = jnp.mean(hh * hh, axis=-1, keepdims=True)
    y = hh * lax.rsqrt(ms + EPS) * hn_ref[...] * jax.nn.sigmoid(og_ref[...].astype(F32))
    o_ref[...] = y.astype(o_ref.dtype)


def mlstm_scan(proj, gates, gate_bias, head_norm, B, S, chunk=256):
    T = proj.shape[0]
    H = M_HEADS
    HDV = head_norm.shape[0]
    DV = HDV // H
    DK = (proj.shape[1] - 2 * HDV) // (2 * H)
    L = _tile(S, chunk)
    NC = S // L
    assert DV % DK == 0 or DK % DV == 0
    k_off = (H * DK) // DK
    v_off = (2 * H * DK) // DV
    og_off = (2 * H * DK + H * DV) // DV
    GW = gates.shape[1]
    g_row = gates[:, :2 * H].reshape(T, 2, H).transpose(2, 1, 0)
    b_col = jnp.pad(gate_bias.reshape(1, 2 * H), ((0, 0), (0, GW - 2 * H)))
    b_row = gate_bias.T.reshape(H, 2, 1)

    HP = MLSTM_HEADS_PER_STEP
    assert H == HP and k_off % HP == 0 and v_off % HP == 0 and og_off % HP == 0
    groups = H // HP

    def tok(g, c):
        return (g // groups) * NC + c

    return pl.pallas_call(
        functools.partial(_mlstm_body, L=L, DK=DK, HP=HP),
        grid=(B * groups, NC),
        in_specs=[pl.BlockSpec((L, HP * DK), lambda g, c: (tok(g, c), g % groups)),
                  pl.BlockSpec((L, HP * DK), lambda g, c: (tok(g, c), k_off // HP + g % groups)),
                  pl.BlockSpec((L, HP * DV), lambda g, c: (tok(g, c), v_off // HP + g % groups)),
                  pl.BlockSpec((L, HP * DV), lambda g, c: (tok(g, c), og_off // HP + g % groups)),
                  pl.BlockSpec((L, GW), lambda g, c: (tok(g, c), 0)),
                  pl.BlockSpec((HP, 2, L), lambda g, c: (g % groups, 0, tok(g, c))),
                  pl.BlockSpec((1, GW), lambda g, c: (0, 0)),
                  pl.BlockSpec((HP, 2, 1), lambda g, c: (g % groups, 0, 0)),
                  pl.BlockSpec((1, HP * DV), lambda g, c: (0, g % groups))],
        out_specs=pl.BlockSpec((L, HP * DV), lambda g, c: (tok(g, c), g % groups)),
        out_shape=jax.ShapeDtypeStruct((T, HDV), BF16),
        scratch_shapes=[pltpu.VMEM((DK, DV), F32),
                        pltpu.VMEM((1, DK), F32),
                        pltpu.VMEM((V7X_SUBLANES, V7X_LANES), F32)] * HP,
        compiler_params=_params("parallel", "arbitrary"),
        name="mlstm_scan",
    )(proj, proj, proj, proj, gates, g_row, b_col, b_row, head_norm.reshape(1, HDV))


def _swa_body(sink_ref, q_ref, kp_ref, kc_ref, o_ref, *, BLK, KVH, GROUP, HD):
    first = pl.program_id(1) == 0
    ri = lax.broadcasted_iota(jnp.int32, (BLK, BLK), 0)
    ci = lax.broadcasted_iota(jnp.int32, (BLK, BLK), 1)
    from_cur = ci <= ri
    prev_bias = jnp.where(jnp.logical_and(first, ci > ri), -jnp.inf, 0.0)
    lo = lax.broadcasted_iota(jnp.int32, (1, 2 * HD), 1) < HD
    scale = HD ** -0.5

    def head_pair_operands(kv, pair):
        xk = kv[:, pair * 2 * HD:(pair + 1) * 2 * HD]
        xv = kv[:, (KVH + pair * 2) * HD:(KVH + pair * 2 + 2) * HD]
        xk_r = pltpu.roll(xk, HD, 1)
        xv_r = pltpu.roll(xv, HD, 1)
        head0 = (jnp.where(lo, xk, xk_r), jnp.where(lo, xv, 0.0), jnp.where(lo, 0.0, xv_r))
        head1 = (jnp.where(lo, xk_r, xk), jnp.where(lo, xv_r, 0.0), jnp.where(lo, 0.0, xv))
        return [tuple(t.astype(BF16) for t in head) for head in (head0, head1)]

    kv = jnp.concatenate([kp_ref[...], kc_ref[...]], axis=0).astype(F32)
    for pair in range(KVH // 2):
        operands = head_pair_operands(kv, pair)
        for sub in range(2):
            kh = 2 * pair + sub
            k2, *v_halves = operands[sub]
            for gp in range(GROUP // 2):
                h0 = kh * GROUP + 2 * gp
                q2 = (q_ref[:, h0 * HD:(h0 + 2) * HD].astype(F32) * scale).astype(BF16)
                acc = None
                for e in range(2):
                    qm = jnp.where(lo if e == 0 else jnp.logical_not(lo), q2, jnp.zeros_like(q2))
                    s2 = lax.dot_general(qm, k2, NT_DIMS, preferred_element_type=F32)
                    s = jnp.where(from_cur, s2[:, BLK:], s2[:, :BLK] + prev_bias)
                    sink = sink_ref[h0 + e]
                    mx = jnp.maximum(jnp.max(s, axis=-1, keepdims=True), sink)
                    pr = jnp.exp(s - mx)
                    den = jnp.sum(pr, axis=-1, keepdims=True) + jnp.exp(sink - mx)
                    p2 = jnp.concatenate([jnp.where(from_cur, 0.0, pr), jnp.where(from_cur, pr, 0.0)], axis=1)
                    o = jnp.dot(p2.astype(BF16), v_halves[e], preferred_element_type=F32) / den
                    acc = o if acc is None else acc + o
                o_ref[:, h0 * HD:(h0 + 2) * HD] = acc.astype(o_ref.dtype)


def swa_attention(q, kv, sinks, B, S):
    T, QD = q.shape
    HD, GROUP, BLK = A_HEAD_DIM, A_GROUP, WINDOW
    KVH = kv.shape[1] // (2 * HD)
    assert QD == KVH * GROUP * HD and 2 * HD == V7X_LANES and KVH % 2 == 0 and GROUP % 2 == 0
    assert HD in (4, 16, 64, 256)
    NB = S // BLK
    return pl.pallas_call(
        functools.partial(_swa_body, BLK=BLK, KVH=KVH, GROUP=GROUP, HD=HD),
        grid=(B, NB),
        in_specs=[pl.BlockSpec(memory_space=pltpu.SMEM),
                  pl.BlockSpec((BLK, QD), lambda b, n: (b * NB + n, 0)),
                  pl.BlockSpec((BLK, 2 * KVH * HD), lambda b, n: (b * NB + jnp.maximum(n - 1, 0), 0)),
                  pl.BlockSpec((BLK, 2 * KVH * HD), lambda b, n: (b * NB + n, 0))],
        out_specs=pl.BlockSpec((BLK, QD), lambda b, n: (b * NB + n, 0)),
        out_shape=jax.ShapeDtypeStruct((T, QD), BF16),
        compiler_params=_params("parallel", "arbitrary"),
        name="swa_attention",
    )(sinks, q, kv, kv)


KEY_BIG = 1 << 30


def _topk_rows(problems, k):
    tm = problems[0][0].shape[1]
    slot = lax.broadcasted_iota(jnp.int32, (k, tm), 0)
    state = [[vals, keys, [], [], jnp.zeros((k, tm), vals.dtype), jnp.zeros((k, tm), keys.dtype)]
             for vals, keys in problems]
    for it in range(k):
        for st in state:
            vals, keys = st[0], st[1]
            m = jnp.max(vals, axis=0, keepdims=True)
            kmin = jnp.min(jnp.where(vals == m, keys, KEY_BIG), axis=0, keepdims=True)
            st[0] = jnp.where(keys == kmin, -jnp.inf, vals)
            st[2].append(m)
            st[3].append(kmin)
            st[4] = jnp.where(slot == it, m, st[4])
            st[5] = jnp.where(slot == it, kmin, st[5])
    return [tuple(st[2:]) for st in state]


def _oddeven_merge_sort(n):
    pairs = []
    p = 1
    while p < n:
        k = p
        while k >= 1:
            for j in range(k % p, n - k, 2 * k):
                for i in range(min(k, n - j - k)):
                    if (i + j) // (2 * p) == (i + j + k) // (2 * p):
                        pairs.append((i + j, i + j + k))
            k //= 2
        p *= 2
    return pairs


def _topk_rows_distinct(problems, k):
    sub = V7X_SUBLANES
    tl = problems[0][0].shape[1]
    slot = lax.broadcasted_iota(jnp.int32, (k, tl), 0)
    state = []
    for vals, keys in problems:
        n = vals.shape[0] // sub
        v = [vals[j * sub:(j + 1) * sub] for j in range(n)]
        q = [keys[j * sub:(j + 1) * sub] for j in range(n)]
        size = 1
        while size < n:
            size *= 2
        for i, j in _oddeven_merge_sort(size):
            if j < n:
                first = v[i] >= v[j]
                v[i], v[j] = jnp.maximum(v[i], v[j]), jnp.minimum(v[i], v[j])
                q[i], q[j] = jnp.where(first, q[i], q[j]), jnp.where(first, q[j], q[i])
        state.append(dict(v=v, q=q, n=n, rows_v=[], rows_k=[], arr_v=jnp.zeros((k, tl), vals.dtype),
                          arr_k=jnp.zeros((k, tl), keys.dtype), tie=jnp.zeros((1, tl), jnp.bool_),
                          pops=jnp.zeros((sub, tl), F32)))
    for it in range(k):
        for st in state:
            v, q, n = st["v"], st["q"], st["n"]
            m = jnp.max(v[0], axis=0, keepdims=True)
            pop = v[0] == m
            key = jnp.max(jnp.where(pop, q[0], -1), axis=0, keepdims=True)
            st["pops"] = st["pops"] + pop.astype(F32)
            if st["rows_v"]:
                st["tie"] = st["tie"] | (m == st["rows_v"][-1])
            depth = min(n, k - it)
            for j in range(depth):
                v[j] = jnp.where(pop, v[j + 1] if j + 1 < n else -jnp.inf, v[j])
                if j + 1 < n:
                    q[j] = jnp.where(pop, q[j + 1], q[j])
            st["rows_v"].append(m)
            st["rows_k"].append(key)
            st["arr_v"] = jnp.where(slot == it, m, st["arr_v"])
            st["arr_k"] = jnp.where(slot == it, key, st["arr_k"])
    out = []
    for st in state:
        tie = (st["tie"] | (jnp.max(st["v"][0], axis=0, keepdims=True) == st["rows_v"][-1])
               | (jnp.sum(st["pops"], axis=0, keepdims=True) != float(k)))
        out.append(((st["rows_v"], st["rows_k"], st["arr_v"], st["arr_k"]), tie))
    return out


def _candidates(top1, top2, k):
    r1_v, r1_i, v1, i1 = top1
    r2_v, r2_i, v2, i2 = top2
    tm = v1.shape[1]
    sub = V7X_SUBLANES
    vals, keys = [], []

    def pack(code, e1, e2):
        return (code << 14) | (e1 << 7) | e2

    b_iota = lax.broadcasted_iota(jnp.int32, (k, tm), 0)
    vals.append(r1_v[0] + v2)
    keys.append(pack(b_iota, r1_i[0], i2))
    a = 1
    while a < k and k // (a + 1) >= 2:
        nb = k // (a + 1)
        rows = -(-nb // sub) * sub
        bi = lax.broadcasted_iota(jnp.int32, (rows, tm), 0)
        vals.append(jnp.where(bi < nb, r1_v[a] + v2[0:rows], -jnp.inf))
        keys.append(pack(a * k + bi, r1_i[a], i2[0:rows]))
        a += 1
    if a < k:
        a0 = (a // sub) * sub
        ai = lax.broadcasted_iota(jnp.int32, (k - a0, tm), 0) + a0
        vals.append(jnp.where(ai >= a, v1[a0:k] + r2_v[0], -jnp.inf))
        keys.append(pack(ai * k, i1[a0:k], r2_i[0]))
    return jnp.concatenate(vals, axis=0), jnp.concatenate(keys, axis=0)


def _peer_qproj_body(x_ref, g_ref, w_ref, xn_ref, q_ref):
    @pl.when(pl.program_id(1) == 0)
    def _():
        xn_ref[...] = _rms(x_ref[...], g_ref[...]).astype(BF16)

    q_ref[...] = jnp.dot(xn_ref[...], w_ref[...], preferred_element_type=F32).astype(q_ref.dtype)


def peer_qproj(h, gain, wq, tm=512, tn=2048):
    T, D = h.shape
    N = wq.shape[1]
    tm, tn = _tile(T, tm), _tile(N, tn)
    return pl.pallas_call(
        _peer_qproj_body,
        grid=(T // tm, N // tn),
        in_specs=[pl.BlockSpec((tm, D), lambda i, j: (i, 0)),
                  pl.BlockSpec((1, D), lambda i, j: (0, 0)),
                  pl.BlockSpec((D, tn), lambda i, j: (0, j))],
        out_specs=[pl.BlockSpec((tm, D), lambda i, j: (i, 0)),
                   pl.BlockSpec((tm, tn), lambda i, j: (i, j))],
        out_shape=[jax.ShapeDtypeStruct((T, D), BF16), jax.ShapeDtypeStruct((T, N), BF16)],
        compiler_params=_params("parallel", "arbitrary"),
        name="peer_qproj",
    )(h, gain.reshape(1, D), wq)


TOPK_TOKENS = 1024


def _key_scores(q_ref, k1, k2, s1_ref, s2_ref):
    half = k1.shape[1]
    s1_ref[...] = lax.dot_general(k1, q_ref[:, 0:half], NT_DIMS, preferred_element_type=F32)
    s2_ref[...] = lax.dot_general(k2, q_ref[:, half:2 * half], NT_DIMS, preferred_element_type=F32)


def _select_from_scores(s1_ref, s2_ref, gate_ref, e1_ref, e2_ref, *, K):
    NK = s1_ref.shape[0]
    tl = V7X_LANES
    key_iota = lax.broadcasted_iota(jnp.int32, (NK, tl), 0)
    groups = [slice(c * tl, (c + 1) * tl) for c in range(s1_ref.shape[1] // tl)]

    def emit(lanes, top):
        rows_s, _, top_s, top_key = top
        ex = jnp.exp(top_s - rows_s[0])
        gate_ref[:, lanes] = ex / jnp.sum(ex, axis=0, keepdims=True)
        e1_ref[:, lanes] = ((top_key >> 7) & 127).astype(F32)
        e2_ref[:, lanes] = (top_key & 127).astype(F32)

    stage1 = _topk_rows_distinct(
        [(ref[:, lanes], key_iota) for lanes in groups for ref in (s1_ref, s2_ref)], K)
    stage2 = _topk_rows_distinct(
        [_candidates(stage1[2 * c][0], stage1[2 * c + 1][0], K) for c in range(len(groups))], K)
    for c, lanes in enumerate(groups):
        emit(lanes, stage2[c][0])

    for c, lanes in enumerate(groups):
        any_tie = jnp.max((stage1[2 * c][1] | stage1[2 * c + 1][1] | stage2[c][1]).astype(jnp.int32))

        @pl.when(any_tie > 0)
        def _():
            top1, top2 = _topk_rows([(s1_ref[:, lanes], key_iota), (s2_ref[:, lanes], key_iota)], K)
            emit(lanes, _topk_rows([_candidates(top1, top2, K)], K)[0])


def _peer_topk_body(q_ref, k1_ref, k2_ref, gate_ref, e1_ref, e2_ref, s1_ref, s2_ref, *, K):
    hd = pl.program_id(1)
    _key_scores(q_ref, k1_ref[hd], k2_ref[hd], s1_ref, s2_ref)
    _select_from_scores(s1_ref, s2_ref, gate_ref, e1_ref, e2_ref, K=K)


def peer_topk(q, k1, k2):
    T = q.shape[0]
    PH, NK, HALF = k1.shape
    K = P_TOPK
    assert NK == V7X_LANES and HALF == V7X_LANES and q.shape[1] == PH * 2 * HALF
    tq = _tile(T, TOPK_TOKENS)
    k_spec = pl.BlockSpec((PH, NK, HALF), lambda c, hd: (0, 0, 0))
    sel_spec = pl.BlockSpec((K, tq), lambda c, hd: (hd, c))
    sel = jax.ShapeDtypeStruct((PH * K, T), F32)
    return pl.pallas_call(
        functools.partial(_peer_topk_body, K=K),
        grid=(T // tq, PH),
        in_specs=[pl.BlockSpec((tq, 2 * HALF), lambda c, hd: (c, hd)), k_spec, k_spec],
        out_specs=[sel_spec] * 3, out_shape=[sel] * 3,
        scratch_shapes=[pltpu.VMEM((NK, tq), F32)] * 2,
        compiler_params=_params("parallel", "arbitrary"),
        name="peer_topk",
    )(q, k1, k2)


SCATTER_PITCH_PAD = V7X_SUBLANES // 2
BF16_SUBLANES = 2 * V7X_SUBLANES


def _peer_scatter_body(gate_ref, e1_ref, e2_ref, w_ref, scr_ref, gt_ref, e1t_ref, *, NK):
    tmb, HK = gt_ref.shape
    pad = SCATTER_PITCH_PAD
    pitch = NK + pad
    gt_ref[...] = gate_ref[...].T
    e1t_ref[...] = e1_ref[...].T
    key_iota = lax.broadcasted_iota(jnp.int32, (NK, HK), 0).astype(F32)
    tall = NK + BF16_SUBLANES
    key_iota_odd = (lax.broadcasted_iota(jnp.int32, (tall, HK), 0) - pad).astype(F32)
    lane_iota = lax.broadcasted_iota(jnp.int32, (HK, NK), 1).astype(F32).astype(BF16)
    e2_cols = e2_ref[...].astype(BF16)
    one, zero = jnp.ones((), BF16), jnp.zeros((), BF16)
    for t in range(tmb):
        odd = t % 2
        g_row = gt_ref[t:t + 1, :]
        sel1 = jnp.where(e1t_ref[t:t + 1, :] == (key_iota_odd if odd else key_iota), g_row, 0.0)
        sel2t = jnp.where(e2_cols[:, t:t + 1] == lane_iota, one, zero)
        w_t = jnp.dot(sel1.astype(BF16), sel2t, preferred_element_type=F32)
        rows = NK + 2 * pad * odd
        start = t * pitch - pad * odd
        scr_ref[start:start + rows, :] = w_t[0:rows]
    for j in range(NK):
        w_ref[j] = scr_ref[pl.ds(j, tmb, stride=pitch), :].astype(w_ref.dtype)


def peer_scatter(gate, e1, e2, NK, tmb=256):
    HK, T = gate.shape
    tmb = _tile(T, tmb)
    assert tmb % 2 == 0
    spec = pl.BlockSpec((HK, tmb), lambda i: (0, i))
    return pl.pallas_call(
        functools.partial(_peer_scatter_body, NK=NK),
        grid=(T // tmb,),
        in_specs=[spec, spec, spec],
        out_specs=pl.BlockSpec((NK, tmb, NK), lambda i: (0, i, 0)),
        out_shape=jax.ShapeDtypeStruct((NK, T, NK), BF16),
        scratch_shapes=[pltpu.VMEM((tmb * (NK + SCATTER_PITCH_PAD), NK), F32),
                        pltpu.VMEM((tmb, HK), F32),
                        pltpu.VMEM((tmb, HK), F32)],
        compiler_params=_params("parallel"),
        name="peer_scatter",
    )(gate, e1, e2)


def _gelu(x):
    return 0.5 * x * (1.0 + lax.erf(x * (2.0 ** -0.5)))


def _peer_dense_body(xn_ref, u_ref, v_ref, w_ref, o_ref, *, NK):
    @pl.when(pl.program_id(1) == 0)
    def _():
        o_ref[...] = jnp.zeros_like(o_ref)

    act = lax.dot_general(xn_ref[...], u_ref[...], NT_DIMS, preferred_element_type=F32)
    coef = jnp.concatenate(
        [(w_ref[c].astype(F32) * _gelu(act[:, c * NK:(c + 1) * NK])).astype(BF16)
         for c in range(w_ref.shape[0])], axis=1)
    o_ref[...] += jnp.dot(coef, v_ref[...], preferred_element_type=F32)


def peer_dense(xn, u_all, v_all, layer, w, tm=1024, te=1024):
    T, D = xn.shape
    E = u_all.shape[1]
    NK = w.shape[0]
    tm, te = _tile(T, tm), _tile(E, te)
    assert te % NK == 0 and E == NK * NK
    return pl.pallas_call(
        functools.partial(_peer_dense_body, NK=NK),
        grid=(T // tm, E // te),
        in_specs=[pl.BlockSpec((tm, D), lambda i, j: (i, 0)),
                  pl.BlockSpec((None, te, D), lambda i, j: (layer, j, 0)),
                  pl.BlockSpec((None, te, D), lambda i, j: (layer, j, 0)),
                  pl.BlockSpec((te // NK, tm, NK), lambda i, j: (j, i, 0))],
        out_specs=pl.BlockSpec((tm, D), lambda i, j: (i, 0)),
        out_shape=jax.ShapeDtypeStruct((T, D), F32),
        compiler_params=pltpu.CompilerParams(dimension_semantics=("parallel", "arbitrary"),
                                             vmem_limit_bytes=VMEM_LIMIT_LARGE),
        name="peer_dense",
    )(xn, u_all, v_all, w)


def peer_layer(h, gain, wq, k1, k2, u_all, v_all, layer):
    xn, q = peer_qproj(h, gain, wq)
    gate, e1, e2 = peer_topk(q, k1, k2)
    w = peer_scatter(gate, e1, e2, k1.shape[1])
    return peer_dense(xn, u_all, v_all, layer, w)


def kernel(x, p, a_norm, a_w_in, a_gate_bias, a_head_norm, a_w_out, kv_norm, w_kv, b_norm, b_w_q, b_sinks,
           b_w_out, c_norm, peer_w_q, peer_k1, peer_k2, peer_u, peer_v, ple_norm, ple_w_gate, ple_w_proj,
           final_norm):
    B, S, D = x.shape
    T = B * S
    depth = p.shape[0]
    n_a = a_norm.shape[0]
    bf = lambda t: t.astype(BF16)
    h = x.reshape(T, D)
    u_all, v_all = bf(peer_u), bf(peer_v)
    p_all = p.reshape(depth, T, -1)
    for i in range(depth):
        if i < n_a:
            w_in = a_w_in[i]
            n_main = w_in.shape[1] - 2 * M_HEADS
            w_gates = jnp.pad(w_in[:, n_main:], ((0, 0), (0, V7X_LANES - 2 * M_HEADS)))
            proj, gates = rms_matmul(h, a_norm[i], bf(w_in), BF16, tm=1024, tn=1024, n_cols=n_main,
                                     side=(a_norm[i], bf(w_gates), F32))
            y = mlstm_scan(proj, gates, a_gate_bias[i], a_head_norm[i], B, S)
            h = matmul_residual(y, bf(a_w_out[i]), h)
        else:
            j = i - n_a
            if j == 0:
                q, kv = rms_matmul(h, b_norm[j], bf(b_w_q[j]), BF16, tm=512, tn=2048,
                                   side=(kv_norm, bf(w_kv), BF16))
            else:
                q = rms_matmul(h, b_norm[j], bf(b_w_q[j]), BF16, tm=512, tn=2048)
            o = swa_attention(q, kv, b_sinks[j], B, S)
            h = matmul_residual(o, bf(b_w_out[j]), h)
        peer_out = peer_layer(h, c_norm[i], bf(peer_w_q[i]), bf(peer_k1[i]), bf(peer_k2[i]), u_all, v_all, i)
        h = ple_gate(h, peer_out, ple_norm[i], bf(ple_w_gate[i]), p_all, i, bf(ple_w_proj[i]),
                     final_norm, final=(i == depth - 1))
    return h.reshape(B, S, D)
```

```python
import functools

import jax
import jax.numpy as jnp
from jax import lax
from jax.experimental import pallas as pl
from jax.experimental.pallas import tpu as pltpu

F32 = jnp.float32
BF16 = jnp.bfloat16

EPS = 1e-6
GATE_CAP = 15.0

M_HEADS = 4
A_HEAD_DIM = 64
A_GROUP = 8
WINDOW = 128
P_HEADS = 8
P_NKEYS = 128
P_TOPK = 16

V7X_LANES = 128
V7X_SUBLANES = 8
V7X_VMEM_BYTES = 64 * 1024 * 1024
VMEM_LIMIT = (V7X_VMEM_BYTES * 3) // 4
VMEM_LIMIT_LARGE = (V7X_VMEM_BYTES * 7) // 8

NT_DIMS = (((1,), (1,)), ((), ()))
TN_DIMS = (((0,), (0,)), ((), ()))


def _params(*sem):
    return pltpu.CompilerParams(dimension_semantics=sem, vmem_limit_bytes=VMEM_LIMIT)


def _rms(x, gain):
    ms = jnp.mean(x * x, axis=-1, keepdims=True)
    return x * lax.rsqrt(ms + EPS) * gain


def _tile(n, pref):
    t = min(n, pref)
    assert n % t == 0, (n, pref)
    return t


def _rms_matmul_body(x_ref, g_ref, w_ref, *rest, side):
    if side:
        gs_ref, ws_ref, o_ref, os_ref, xn_ref = rest
    else:
        o_ref, xn_ref = rest

    @pl.when(pl.program_id(1) == 0)
    def _():
        x = x_ref[...]
        y = x * lax.rsqrt(jnp.mean(x * x, axis=-1, keepdims=True) + EPS)
        xn_ref[...] = (y * g_ref[...]).astype(BF16)
        if side:
            os_ref[...] = jnp.dot((y * gs_ref[...]).astype(BF16), ws_ref[...],
                                  preferred_element_type=F32).astype(os_ref.dtype)

    o_ref[...] = jnp.dot(xn_ref[...], w_ref[...], preferred_element_type=F32).astype(o_ref.dtype)


def rms_matmul(x, gain, w, out_dtype, tm=512, tn=512, side=None, n_cols=None):
    T, D = x.shape
    N = w.shape[1] if n_cols is None else n_cols
    tm, tn = _tile(T, tm), _tile(N, tn)
    in_specs = [pl.BlockSpec((tm, D), lambda i, j: (i, 0)),
                pl.BlockSpec((1, D), lambda i, j: (0, 0)),
                pl.BlockSpec((D, tn), lambda i, j: (0, j))]
    out_specs = [pl.BlockSpec((tm, tn), lambda i, j: (i, j))]
    out_shape = [jax.ShapeDtypeStruct((T, N), out_dtype)]
    args = [x, gain.reshape(1, D), w]
    if side:
        gain_s, w_s, dtype_s = side
        NS = w_s.shape[1]
        in_specs += [pl.BlockSpec((1, D), lambda i, j: (0, 0)), pl.BlockSpec((D, NS), lambda i, j: (0, 0))]
        out_specs.append(pl.BlockSpec((tm, NS), lambda i, j: (i, 0)))
        out_shape.append(jax.ShapeDtypeStruct((T, NS), dtype_s))
        args += [gain_s.reshape(1, D), w_s]
    res = pl.pallas_call(
        functools.partial(_rms_matmul_body, side=bool(side)),
        grid=(T // tm, N // tn),
        in_specs=in_specs, out_specs=out_specs, out_shape=out_shape,
        scratch_shapes=[pltpu.VMEM((tm, D), BF16)],
        compiler_params=_params("parallel", "arbitrary"),
        name="rms_matmul",
    )(*args)
    return res if side else res[0]


def _matmul_residual_body(y_ref, w_ref, h_ref, o_ref):
    o_ref[...] = h_ref[...] + jnp.dot(y_ref[...], w_ref[...], preferred_element_type=F32)


def matmul_residual(y, w, h, tm=512, tn=2048):
    T, K = y.shape
    N = w.shape[1]
    tm, tn = _tile(T, tm), _tile(N, tn)
    return pl.pallas_call(
        _matmul_residual_body,
        grid=(T // tm, N // tn),
        in_specs=[pl.BlockSpec((tm, K), lambda i, j: (i, 0)),
                  pl.BlockSpec((K, tn), lambda i, j: (0, j)),
                  pl.BlockSpec((tm, tn), lambda i, j: (i, j))],
        out_specs=pl.BlockSpec((tm, tn), lambda i, j: (i, j)),
        out_shape=jax.ShapeDtypeStruct((T, N), F32),
        compiler_params=_params("parallel", "arbitrary"),
        name="matmul_residual",
    )(y, w, h)


def _ple_body(h_ref, d_ref, g_ref, wg_ref, p_ref, wp_ref, fn_ref, o_ref, *, final):
    h = h_ref[...] + d_ref[...]
    xn = _rms(h, g_ref[...]).astype(BF16)
    gate = jax.nn.sigmoid(jnp.dot(xn, wg_ref[...], preferred_element_type=F32))
    emb = jnp.dot(p_ref[...].astype(BF16), wp_ref[...], preferred_element_type=F32)
    out = h + gate * emb
    if final:
        out = _rms(out, fn_ref[...])
    o_ref[...] = out


def ple_gate(h, peer_out, gain, wg, p_all, layer, wp, final_gain, final, tm=512):
    T, D = h.shape
    PD = p_all.shape[2]
    tm = _tile(T, tm)
    resident = pl.Buffered(1)
    return pl.pallas_call(
        functools.partial(_ple_body, final=final),
        grid=(T // tm,),
        in_specs=[pl.BlockSpec((tm, D), lambda i: (i, 0)),
                  pl.BlockSpec((tm, D), lambda i: (i, 0)),
                  pl.BlockSpec((1, D), lambda i: (0, 0)),
                  pl.BlockSpec((D, D), lambda i: (0, 0), pipeline_mode=resident),
                  pl.BlockSpec((None, tm, PD), lambda i: (layer, i, 0)),
                  pl.BlockSpec((PD, D), lambda i: (0, 0), pipeline_mode=resident),
                  pl.BlockSpec((1, D), lambda i: (0, 0))],
        out_specs=pl.BlockSpec((tm, D), lambda i: (i, 0)),
        out_shape=jax.ShapeDtypeStruct((T, D), F32),
        compiler_params=pltpu.CompilerParams(dimension_semantics=("parallel",), vmem_limit_bytes=VMEM_LIMIT_LARGE),
        name="ple_gate",
    )(h, peer_out, gain.reshape(1, D), wg, p_all, wp, final_gain.reshape(1, D))


def _soft_cap(t):
    return GATE_CAP * jnp.tanh(t / GATE_CAP)


def _log_sigmoid(t):
    return jnp.minimum(t, 0.0) - jnp.log(1.0 + jnp.exp(-jnp.abs(t)))


def _split3(x):
    hi = x.astype(BF16)
    r = x - hi.astype(F32)
    mid = r.astype(BF16)
    lo = (r - mid.astype(F32)).astype(BF16)
    return hi, mid, lo


MLSTM_HEADS_PER_STEP = 4


def _mlstm_body(q_ref, k_ref, v_ref, og_ref, gc_ref, gr_ref, bc_ref, br_ref, hn_ref, o_ref, *state, L, DK, HP):
    @pl.when(pl.program_id(1) == 0)
    def _():
        for ref in state:
            ref[...] = jnp.zeros_like(ref)

    DV = o_ref.shape[1] // HP
    row = lax.broadcasted_iota(jnp.int32, (L, L), 0)
    col = lax.broadcasted_iota(jnp.int32, (L, L), 1)
    causal = row >= col
    tri_l = causal.astype(BF16)
    tri_u = (row <= col).astype(BF16)
    gate_cols = gc_ref[...] + bc_ref[...]
    for i in range(HP):
        dk, dv = slice(i * DK, (i + 1) * DK), slice(i * DV, (i + 1) * DV)
        _mlstm_head(q_ref.at[:, dk], k_ref.at[:, dk], v_ref.at[:, dv], og_ref.at[:, dv],
                    gate_cols[:, i:i + 1], gate_cols[:, HP + i:HP + i + 1], gr_ref.at[i], br_ref.at[i],
                    hn_ref.at[:, dv], o_ref.at[:, dv], *state[3 * i:3 * i + 3],
                    causal, tri_l, tri_u, L=L, DK=DK)


def _mlstm_head(q_ref, k_ref, v_ref, og_ref, gi_col, gf_col, gr_ref, br_ref, hn_ref, o_ref,
                c_ref, n_ref, m_ref, causal, tri_l, tri_u, *, L, DK):
    q = (q_ref[...].astype(F32) * (DK ** -0.5)).astype(BF16)
    k = k_ref[...]
    v = v_ref[...]

    gr = gr_ref[...] + br_ref[...]
    li_c = _soft_cap(gi_col)
    lf_c = _log_sigmoid(_soft_cap(gf_col))
    li_r = _soft_cap(gr[0:1, :])
    lf_r = _log_sigmoid(_soft_cap(gr[1:2, :]))

    b_c = sum(jnp.dot(tri_l, part, preferred_element_type=F32)
              for part in _split3(jnp.broadcast_to(lf_c, (L, V7X_LANES))))[:, 0:1]
    b_r = sum(jnp.dot(part, tri_u, preferred_element_type=F32)
              for part in _split3(jnp.broadcast_to(lf_r, (V7X_SUBLANES, L))))[0:1, :]

    m_prev = m_ref[0:1, 0:1]
    dmat = jnp.where(causal, b_c - b_r + li_r, -jnp.inf)
    inter = b_c + m_prev
    m_t = jnp.maximum(inter, jnp.max(dmat, axis=-1, keepdims=True))
    w_intra = jnp.exp(dmat - m_t)
    w_inter = jnp.exp(inter - m_t)

    s = lax.dot_general(q, k, NT_DIMS, preferred_element_type=F32) * w_intra
    c_old = c_ref[...]
    n_old = n_ref[...]
    num = (w_inter * jnp.dot(q, c_old.astype(BF16), preferred_element_type=F32)
           + jnp.dot(s.astype(BF16), v, preferred_element_type=F32))
    qn = lax.dot_general(q, jnp.broadcast_to(n_old, (V7X_LANES, DK)).astype(BF16), NT_DIMS,
                         preferred_element_type=F32)[:, 0:1]
    den = w_inter * qn + jnp.sum(s, axis=-1, keepdims=True)
    hh = num / jnp.maximum(jnp.abs(den), jnp.exp(-m_t))

    m_new = m_t[L - 1:L, :]
    b_last = b_c[L - 1:L, :]
    w_state = jnp.exp(b_last - b_c + li_c - m_new)
    decay = jnp.exp(b_last + m_prev - m_new)
    kw = k.astype(F32) * w_state
    c_ref[...] = decay * c_old + lax.dot_general(kw.astype(BF16), v, TN_DIMS, preferred_element_type=F32)
    n_ref[...] = decay * n_old + jnp.sum(kw, axis=0, keepdims=True)
    m_ref[...] = jnp.broadcast_to(m_new, m_ref.shape)

    ms = jnp.mean(hh * hh, axis=-1, keepdims=True)
    y = hh * lax.rsqrt(ms + EPS) * hn_ref[...] * jax.nn.sigmoid(og_ref[...].astype(F32))
    o_ref[...] = y.astype(o_ref.dtype)


def mlstm_scan(proj, gates, gate_bias, head_norm, B, S, chunk=256):
    T = proj.shape[0]
    H = M_HEADS
    HDV = head_norm.shape[0]
    DV = HDV // H
    DK = (proj.shape[1] - 2 * HDV) // (2 * H)
    L = _tile(S, chunk)
    NC = S // L
    assert DV % DK == 0 or DK % DV == 0
    k_off = (H * DK) // DK
    v_off = (2 * H * DK) // DV
    og_off = (2 * H * DK + H * DV) // DV
    GW = gates.shape[1]
    g_row = gates[:, :2 * H].reshape(T, 2, H).transpose(2, 1, 0)
    b_col = jnp.pad(gate_bias.reshape(1, 2 * H), ((0, 0), (0, GW - 2 * H)))
    b_row = gate_bias.T.reshape(H, 2, 1)

    HP = MLSTM_HEADS_PER_STEP
    assert H == HP and k_off % HP == 0 and v_off % HP == 0 and og_off % HP == 0
    groups = H // HP

    def tok(g, c):
        return (g // groups) * NC + c

    return pl.pallas_call(
        functools.partial(_mlstm_body, L=L, DK=DK, HP=HP),
        grid=(B * groups, NC),
        in_specs=[pl.BlockSpec((L, HP * DK), lambda g, c: (tok(g, c), g % groups)),
                  pl.BlockSpec((L, HP * DK), lambda g, c: (tok(g, c), k_off // HP + g % groups)),
                  pl.BlockSpec((L, HP * DV), lambda g, c: (tok(g, c), v_off // HP + g % groups)),
                  pl.BlockSpec((L, HP * DV), lambda g, c: (tok(g, c), og_off // HP + g % groups)),
                  pl.BlockSpec((L, GW), lambda g, c: (tok(g, c), 0)),
                  pl.BlockSpec((HP, 2, L), lambda g, c: (g % groups, 0, tok(g, c))),
                  pl.BlockSpec((1, GW), lambda g, c: (0, 0)),
                  pl.BlockSpec((HP, 2, 1), lambda g, c: (g % groups, 0, 0)),
                  pl.BlockSpec((1, HP * DV), lambda g, c: (0, g % groups))],
        out_specs=pl.BlockSpec((L, HP * DV), lambda g, c: (tok(g, c), g % groups)),
        out_shape=jax.ShapeDtypeStruct((T, HDV), BF16),
        scratch_shapes=[pltpu.VMEM((DK, DV), F32),
                        pltpu.VMEM((1, DK), F32),
                        pltpu.VMEM((V7X_SUBLANES, V7X_LANES), F32)] * HP,
        compiler_params=_params("parallel", "arbitrary"),
        name="mlstm_scan",
    )(proj, proj, proj, proj, gates, g_row, b_col, b_row, head_norm.reshape(1, HDV))


def _swa_body(sink_ref, q_ref, kp_ref, kc_ref, o_ref, *, BLK, KVH, GROUP, HD):
    first = pl.program_id(1) == 0
    ri = lax.broadcasted_iota(jnp.int32, (BLK, BLK), 0)
    ci = lax.broadcasted_iota(jnp.int32, (BLK, BLK), 1)
    from_cur = ci <= ri
    prev_bias = jnp.where(jnp.logical_and(first, ci > ri), -jnp.inf, 0.0)
    lo = lax.broadcasted_iota(jnp.int32, (1, 2 * HD), 1) < HD
    scale = HD ** -0.5

    def head_pair_operands(kv, pair):
        xk = kv[:, pair * 2 * HD:(pair + 1) * 2 * HD]
        xv = kv[:, (KVH + pair * 2) * HD:(KVH + pair * 2 + 2) * HD]
        xk_r = pltpu.roll(xk, HD, 1)
        xv_r = pltpu.roll(xv, HD, 1)
        head0 = (jnp.where(lo, xk, xk_r), jnp.where(lo, xv, 0.0), jnp.where(lo, 0.0, xv_r))
        head1 = (jnp.where(lo, xk_r, xk), jnp.where(lo, xv_r, 0.0), jnp.where(lo, 0.0, xv))
        return [tuple(t.astype(BF16) for t in head) for head in (head0, head1)]

    kv = jnp.concatenate([kp_ref[...], kc_ref[...]], axis=0).astype(F32)
    for pair in range(KVH // 2):
        operands = head_pair_operands(kv, pair)
        for sub in range(2):
            kh = 2 * pair + sub
            k2, *v_halves = operands[sub]
            for gp in range(GROUP // 2):
                h0 = kh * GROUP + 2 * gp
                q2 = (q_ref[:, h0 * HD:(h0 + 2) * HD].astype(F32) * scale).astype(BF16)
                acc = None
                for e in range(2):
                    qm = jnp.where(lo if e == 0 else jnp.logical_not(lo), q2, jnp.zeros_like(q2))
                    s2 = lax.dot_general(qm, k2, NT_DIMS, preferred_element_type=F32)
                    s = jnp.where(from_cur, s2[:, BLK:], s2[:, :BLK] + prev_bias)
                    sink = sink_ref[h0 + e]
                    mx = jnp.maximum(jnp.max(s, axis=-1, keepdims=True), sink)
                    pr = jnp.exp(s - mx)
                    den = jnp.sum(pr, axis=-1, keepdims=True) + jnp.exp(sink - mx)
                    p2 = jnp.concatenate([jnp.where(from_cur, 0.0, pr), jnp.where(from_cur, pr, 0.0)], axis=1)
                    o = jnp.dot(p2.astype(BF16), v_halves[e], preferred_element_type=F32) / den
                    acc = o if acc is None else acc + o
                o_ref[:, h0 * HD:(h0 + 2) * HD] = acc.astype(o_ref.dtype)


def swa_attention(q, kv, sinks, B, S):
    T, QD = q.shape
    HD, GROUP, BLK = A_HEAD_DIM, A_GROUP, WINDOW
    KVH = kv.shape[1] // (2 * HD)
    assert QD == KVH * GROUP * HD and 2 * HD == V7X_LANES and KVH % 2 == 0 and GROUP % 2 == 0
    assert HD in (4, 16, 64, 256)
    NB = S // BLK
    return pl.pallas_call(
        functools.partial(_swa_body, BLK=BLK, KVH=KVH, GROUP=GROUP, HD=HD),
        grid=(B, NB),
        in_specs=[pl.BlockSpec(memory_space=pltpu.SMEM),
                  pl.BlockSpec((BLK, QD), lambda b, n: (b * NB + n, 0)),
                  pl.BlockSpec((BLK, 2 * KVH * HD), lambda b, n: (b * NB + jnp.maximum(n - 1, 0), 0)),
                  pl.BlockSpec((BLK, 2 * KVH * HD), lambda b, n: (b * NB + n, 0))],
        out_specs=pl.BlockSpec((BLK, QD), lambda b, n: (b * NB + n, 0)),
        out_shape=jax.ShapeDtypeStruct((T, QD), BF16),
        compiler_params=_params("parallel", "arbitrary"),
        name="swa_attention",
    )(sinks, q, kv, kv)


KEY_BIG = 1 << 30


def _topk_rows(problems, k):
    tm = problems[0][0].shape[1]
    slot = lax.broadcasted_iota(jnp.int32, (k, tm), 0)
    state = [[vals, keys, [], [], jnp.zeros((k, tm), vals.dtype), jnp.zeros((k, tm), keys.dtype)]
             for vals, keys in problems]
    for it in range(k):
        for st in state:
            vals, keys = st[0], st[1]
            m = jnp.max(vals, axis=0, keepdims=True)
            kmin = jnp.min(jnp.where(vals == m, keys, KEY_BIG), axis=0, keepdims=True)
            st[0] = jnp.where(keys == kmin, -jnp.inf, vals)
            st[2].append(m)
            st[3].append(kmin)
            st[4] = jnp.where(slot == it, m, st[4])
            st[5] = jnp.where(slot == it, kmin, st[5])
    return [tuple(st[2:]) for st in state]


def _oddeven_merge_sort(n):
    pairs = []
    p = 1
    while p < n:
        k = p
        while k >= 1:
            for j in range(k % p, n - k, 2 * k):
                for i in range(min(k, n - j - k)):
                    if (i + j) // (2 * p) == (i + j + k) // (2 * p):
                        pairs.append((i + j, i + j + k))
            k //= 2
        p *= 2
    return pairs


def _topk_rows_distinct(problems, k):
    sub = V7X_SUBLANES
    tl = problems[0][0].shape[1]
    slot = lax.broadcasted_iota(jnp.int32, (k, tl), 0)
    state = []
    for vals, keys in problems:
        n = vals.shape[0] // sub
        v = [vals[j * sub:(j + 1) * sub] for j in range(n)]
        q = [keys[j * sub:(j + 1) * sub] for j in range(n)]
        size = 1
        while size < n:
            size *= 2
        for i, j in _oddeven_merge_sort(size):
            if j < n:
                first = v[i] >= v[j]
                v[i], v[j] = jnp.maximum(v[i], v[j]), jnp.minimum(v[i], v[j])
                q[i], q[j] = jnp.where(first, q[i], q[j]), jnp.where(first, q[j], q[i])
        state.append(dict(v=v, q=q, n=n, rows_v=[], rows_k=[], arr_v=jnp.zeros((k, tl), vals.dtype),
                          arr_k=jnp.zeros((k, tl), keys.dtype), tie=jnp.zeros((1, tl), jnp.bool_),
                          pops=jnp.zeros((sub, tl), F32)))
    for it in range(k):
        for st in state:
            v, q, n = st["v"], st["q"], st["n"]
            m = jnp.max(v[0], axis=0, keepdims=True)
            pop = v[0] == m
            key = jnp.max(jnp.where(pop, q[0], -1), axis=0, keepdims=True)
            st["pops"] = st["pops"] + pop.astype(F32)
            if st["rows_v"]:
                st["tie"] = st["tie"] | (m == st["rows_v"][-1])
            depth = min(n, k - it)
            for j in range(depth):
                v[j] = jnp.where(pop, v[j + 1] if j + 1 < n else -jnp.inf, v[j])
                if j + 1 < n:
                    q[j] = jnp.where(pop, q[j + 1], q[j])
            st["rows_v"].append(m)
            st["rows_k"].append(key)
            st["arr_v"] = jnp.where(slot == it, m, st["arr_v"])
            st["arr_k"] = jnp.where(slot == it, key, st["arr_k"])
    out = []
    for st in state:
        tie = (st["tie"] | (jnp.max(st["v"][0], axis=0, keepdims=True) == st["rows_v"][-1])
               | (jnp.sum(st["pops"], axis=0, keepdims=True) != float(k)))
        out.append(((st["rows_v"], st["rows_k"], st["arr_v"], st["arr_k"]), tie))
    return out


def _candidates(top1, top2, k):
    r1_v, r1_i, v1, i1 = top1
    r2_v, r2_i, v2, i2 = top2
    tm = v1.shape[1]
    sub = V7X_SUBLANES
    vals, keys = [], []

    def pack(code, e1, e2):
        return (code << 14) | (e1 << 7) | e2

    b_iota = lax.broadcasted_iota(jnp.int32, (k, tm), 0)
    vals.append(r1_v[0] + v2)
    keys.append(pack(b_iota, r1_i[0], i2))
    a = 1
    while a < k and k // (a + 1) >= 2:
        nb = k // (a + 1)
        rows = -(-nb // sub) * sub
        bi = lax.broadcasted_iota(jnp.int32, (rows, tm), 0)
        vals.append(jnp.where(bi < nb, r1_v[a] + v2[0:rows], -jnp.inf))
        keys.append(pack(a * k + bi, r1_i[a], i2[0:rows]))
        a += 1
    if a < k:
        a0 = (a // sub) * sub
        ai = lax.broadcasted_iota(jnp.int32, (k - a0, tm), 0) + a0
        vals.append(jnp.where(ai >= a, v1[a0:k] + r2_v[0], -jnp.inf))
        keys.append(pack(ai * k, i1[a0:k], r2_i[0]))
    return jnp.concatenate(vals, axis=0), jnp.concatenate(keys, axis=0)


def _peer_qproj_body(x_ref, g_ref, w_ref, xn_ref, q_ref):
    @pl.when(pl.program_id(1) == 0)
    def _():
        xn_ref[...] = _rms(x_ref[...], g_ref[...]).astype(BF16)

    q_ref[...] = jnp.dot(xn_ref[...], w_ref[...], preferred_element_type=F32).astype(q_ref.dtype)


def peer_qproj(h, gain, wq, tm=512, tn=2048):
    T, D = h.shape
    N = wq.shape[1]
    tm, tn = _tile(T, tm), _tile(N, tn)
    return pl.pallas_call(
        _peer_qproj_body,
        grid=(T // tm, N // tn),
        in_specs=[pl.BlockSpec((tm, D), lambda i, j: (i, 0)),
                  pl.BlockSpec((1, D), lambda i, j: (0, 0)),
                  pl.BlockSpec((D, tn), lambda i, j: (0, j))],
        out_specs=[pl.BlockSpec((tm, D), lambda i, j: (i, 0)),
                   pl.BlockSpec((tm, tn), lambda i, j: (i, j))],
        out_shape=[jax.ShapeDtypeStruct((T, D), BF16), jax.ShapeDtypeStruct((T, N), BF16)],
        compiler_params=_params("parallel", "arbitrary"),
        name="peer_qproj",
    )(h, gain.reshape(1, D), wq)


TOPK_TOKENS = 1024


def _key_scores(q_ref, k1, k2, s1_ref, s2_ref):
    half = k1.shape[1]
    s1_ref[...] = lax.dot_general(k1, q_ref[:, 0:half], NT_DIMS, preferred_element_type=F32)
    s2_ref[...] = lax.dot_general(k2, q_ref[:, half:2 * half], NT_DIMS, preferred_element_type=F32)


def _select_from_scores(s1_ref, s2_ref, gate_ref, e1_ref, e2_ref, *, K):
    NK = s1_ref.shape[0]
    tl = V7X_LANES
    key_iota = lax.broadcasted_iota(jnp.int32, (NK, tl), 0)
    groups = [slice(c * tl, (c + 1) * tl) for c in range(s1_ref.shape[1] // tl)]

    def emit(lanes, top):
        rows_s, _, top_s, top_key = top
        ex = jnp.exp(top_s - rows_s[0])
        gate_ref[:, lanes] = ex / jnp.sum(ex, axis=0, keepdims=True)
        e1_ref[:, lanes] = ((top_key >> 7) & 127).astype(F32)
        e2_ref[:, lanes] = (top_key & 127).astype(F32)

    stage1 = _topk_rows_distinct(
        [(ref[:, lanes], key_iota) for lanes in groups for ref in (s1_ref, s2_ref)], K)
    stage2 = _topk_rows_distinct(
        [_candidates(stage1[2 * c][0], stage1[2 * c + 1][0], K) for c in range(len(groups))], K)
    for c, lanes in enumerate(groups):
        emit(lanes, stage2[c][0])

    for c, lanes in enumerate(groups):
        any_tie = jnp.max((stage1[2 * c][1] | stage1[2 * c + 1][1] | stage2[c][1]).astype(jnp.int32))

        @pl.when(any_tie > 0)
        def _():
            top1, top2 = _topk_rows([(s1_ref[:, lanes], key_iota), (s2_ref[:, lanes], key_iota)], K)
            emit(lanes, _topk_rows([_candidates(top1, top2, K)], K)[0])


def _peer_topk_body(q_ref, k1_ref, k2_ref, gate_ref, e1_ref, e2_ref, s1_ref, s2_ref, *, K):
    hd = pl.program_id(1)
    _key_scores(q_ref, k1_ref[hd], k2_ref[hd], s1_ref, s2_ref)
    _select_from_scores(s1_ref, s2_ref, gate_ref, e1_ref, e2_ref, K=K)


def peer_topk(q, k1, k2):
    T = q.shape[0]
    PH, NK, HALF = k1.shape
    K = P_TOPK
    assert NK == V7X_LANES and HALF == V7X_LANES and q.shape[1] == PH * 2 * HALF
    tq = _tile(T, TOPK_TOKENS)
    k_spec = pl.BlockSpec((PH, NK, HALF), lambda c, hd: (0, 0, 0))
    sel_spec = pl.BlockSpec((K, tq), lambda c, hd: (hd, c))
    sel = jax.ShapeDtypeStruct((PH * K, T), F32)
    return pl.pallas_call(
        functools.partial(_peer_topk_body, K=K),
        grid=(T // tq, PH),
        in_specs=[pl.BlockSpec((tq, 2 * HALF), lambda c, hd: (c, hd)), k_spec, k_spec],
        out_specs=[sel_spec] * 3, out_shape=[sel] * 3,
        scratch_shapes=[pltpu.VMEM((NK, tq), F32)] * 2,
        compiler_params=_params("parallel", "arbitrary"),
        name="peer_topk",
    )(q, k1, k2)


SCATTER_PITCH_PAD = V7X_SUBLANES // 2
BF16_SUBLANES = 2 * V7X_SUBLANES


def _peer_scatter_body(gate_ref, e1_ref, e2_ref, w_ref, scr_ref, gt_ref, e1t_ref, *, NK):
    tmb, HK = gt_ref.shape
    pad = SCATTER_PITCH_PAD
    pitch = NK + pad
    gt_ref[...] = gate_ref[...].T
    e1t_ref[...] = e1_ref[...].T
    key_iota = lax.broadcasted_iota(jnp.int32, (NK, HK), 0).astype(F32)
    tall = NK + BF16_SUBLANES
    key_iota_odd = (lax.broadcasted_iota(jnp.int32, (tall, HK), 0) - pad).astype(F32)
    lane_iota = lax.broadcasted_iota(jnp.int32, (HK, NK), 1).astype(F32).astype(BF16)
    e2_cols = e2_ref[...].astype(BF16)
    one, zero = jnp.ones((), BF16), jnp.zeros((), BF16)
    for t in range(tmb):
        odd = t % 2
        g_row = gt_ref[t:t + 1, :]
        sel1 = jnp.where(e1t_ref[t:t + 1, :] == (key_iota_odd if odd else key_iota), g_row, 0.0)
        sel2t = jnp.where(e2_cols[:, t:t + 1] == lane_iota, one, zero)
        w_t = jnp.dot(sel1.astype(BF16), sel2t, preferred_element_type=F32)
        rows = NK + 2 * pad * odd
        start = t * pitch - pad * odd
        scr_ref[start:start + rows, :] = w_t[0:rows]
    for j in range(NK):
        w_ref[j] = scr_ref[pl.ds(j, tmb, stride=pitch), :].astype(w_ref.dtype)


def peer_scatter(gate, e1, e2, NK, tmb=256):
    HK, T = gate.shape
    tmb = _tile(T, tmb)
    assert tmb % 2 == 0
    spec = pl.BlockSpec((HK, tmb), lambda i: (0, i))
    return pl.pallas_call(
        functools.partial(_peer_scatter_body, NK=NK),
        grid=(T // tmb,),
        in_specs=[spec, spec, spec],
        out_specs=pl.BlockSpec((NK, tmb, NK), lambda i: (0, i, 0)),
        out_shape=jax.ShapeDtypeStruct((NK, T, NK), BF16),
        scratch_shapes=[pltpu.VMEM((tmb * (NK + SCATTER_PITCH_PAD), NK), F32),
                        pltpu.VMEM((tmb, HK), F32),
                        pltpu.VMEM((tmb, HK), F32)],
        compiler_params=_params("parallel"),
        name="peer_scatter",
    )(gate, e1, e2)


def _gelu(x):
    return 0.5 * x * (1.0 + lax.erf(x * (2.0 ** -0.5)))


def _peer_dense_body(xn_ref, u_ref, v_ref, w_ref, o_ref, *, NK):
    @pl.when(pl.program_id(1) == 0)
    def _():
        o_ref[...] = jnp.zeros_like(o_ref)

    act = lax.dot_general(xn_ref[...], u_ref[...], NT_DIMS, preferred_element_type=F32)
    coef = jnp.concatenate(
        [(w_ref[c].astype(F32) * _gelu(act[:, c * NK:(c + 1) * NK])).astype(BF16)
         for c in range(w_ref.shape[0])], axis=1)
    o_ref[...] += jnp.dot(coef, v_ref[...], preferred_element_type=F32)


def peer_dense(xn, u_all, v_all, layer, w, tm=1024, te=1024):
    T, D = xn.shape
    E = u_all.shape[1]
    NK = w.shape[0]
    tm, te = _tile(T, tm), _tile(E, te)
    assert te % NK == 0 and E == NK * NK
    return pl.pallas_call(
        functools.partial(_peer_dense_body, NK=NK),
        grid=(T // tm, E // te),
        in_specs=[pl.BlockSpec((tm, D), lambda i, j: (i, 0)),
                  pl.BlockSpec((None, te, D), lambda i, j: (layer, j, 0)),
                  pl.BlockSpec((None, te, D), lambda i, j: (layer, j, 0)),
                  pl.BlockSpec((te // NK, tm, NK), lambda i, j: (j, i, 0))],
        out_specs=pl.BlockSpec((tm, D), lambda i, j: (i, 0)),
        out_shape=jax.ShapeDtypeStruct((T, D), F32),
        compiler_params=pltpu.CompilerParams(dimension_semantics=("parallel", "arbitrary"),
                                             vmem_limit_bytes=VMEM_LIMIT_LARGE),
        name="peer_dense",
    )(xn, u_all, v_all, w)


def peer_layer(h, gain, wq, k1, k2, u_all, v_all, layer):
    xn, q = peer_qproj(h, gain, wq)
    gate, e1, e2 = peer_topk(q, k1, k2)
    w = peer_scatter(gate, e1, e2, k1.shape[1])
    return peer_dense(xn, u_all, v_all, layer, w)


def kernel(x, p, a_norm, a_w_in, a_gate_bias, a_head_norm, a_w_out, kv_norm, w_kv, b_norm, b_w_q, b_sinks,
           b_w_out, c_norm, peer_w_q, peer_k1, peer_k2, peer_u, peer_v, ple_norm, ple_w_gate, ple_w_proj,
           final_norm):
    B, S, D = x.shape
    T = B * S
    depth = p.shape[0]
    n_a = a_norm.shape[0]
    bf = lambda t: t.astype(BF16)
    h = x.reshape(T, D)
    u_all, v_all = bf(peer_u), bf(peer_v)
    p_all = p.reshape(depth, T, -1)
    for i in range(depth):
        if i < n_a:
            w_in = a_w_in[i]
            n_main = w_in.shape[1] - 2 * M_HEADS
            w_gates = jnp.pad(w_in[:, n_main:], ((0, 0), (0, V7X_LANES - 2 * M_HEADS)))
            proj, gates = rms_matmul(h, a_norm[i], bf(w_in), BF16, tm=1024, tn=1024, n_cols=n_main,
                                     side=(a_norm[i], bf(w_gates), F32))
            y = mlstm_scan(proj, gates, a_gate_bias[i], a_head_norm[i], B, S)
            h = matmul_residual(y, bf(a_w_out[i]), h)
        else:
            j = i - n_a
            if j == 0:
                q, kv = rms_matmul(h, b_norm[j], bf(b_w_q[j]), BF16, tm=512, tn=2048,
                                   side=(kv_norm, bf(w_kv), BF16))
            else:
                q = rms_matmul(h, b_norm[j], bf(b_w_q[j]), BF16, tm=512, tn=2048)
            o = swa_attention(q, kv, b_sinks[j], B, S)
            h = matmul_residual(o, bf(b_w_out[j]), h)
        peer_out = peer_layer(h, c_norm[i], bf(peer_w_q[i]), bf(peer_k1[i]), bf(peer_k2[i]), u_all, v_all, i)
        h = ple_gate(h, peer_out, ple_norm[i], bf(ple_w_gate[i]), p_all, i, bf(ple_w_proj[i]),
                     final_norm, final=(i == depth - 1))
    return h.reshape(B, S, D)
```
